```python
import jax, jax.numpy as jnp
from jax import lax
import numpy as np

D_MODEL = 2048
BATCH = 4
SEQ = 2048
DEPTH = 2
DEC_BATCH = 128
DEC_SEQ = 4
PAST_LEN = 16384
PAGE_SIZE = 128

GROUP = 128
POOL_WINDOWS = (2, 4, 8, 16)
N_POOL_GROUPS = len(POOL_WINDOWS)
D_POOL = GROUP * N_POOL_GROUPS
POOL_STATE = max(POOL_WINDOWS) - 1
D_SC = 512
SC_WIDTH = 3
D_CM = 512
CM_WIDTH = 31
D_SG = 512
SG_GROUPS = D_SG // GROUP
CHUNK = 128
N_BRANCH = 4
N_IN = D_POOL + 3 * D_SC + 2 * D_CM + 2 * D_SG + N_BRANCH * D_MODEL
MEM_LEN = 256
X_HEADS = 4
X_HEAD_DIM = 128
D_X = X_HEADS * X_HEAD_DIM
PEER_HEADS = 8
PEER_DKEY = 256
PEER_NKEYS = 128
PEER_TOPK = 16
N_EXPERTS = PEER_NKEYS * PEER_NKEYS
PEER_BLOCK = 128
EPS = 1e-6

kernel_name = 'hybrid_pool_conv_sgmlp_peer_decoder_step'


def rmsnorm(x, g):
    xf = x.astype(jnp.float32)
    y = xf * lax.rsqrt(jnp.mean(xf * xf, axis=-1, keepdims=True) + EPS)
    return (y * g.astype(jnp.float32)).astype(x.dtype)


def layernorm(x, g, b):
    xf = x.astype(jnp.float32)
    mu = jnp.mean(xf, axis=-1, keepdims=True)
    var = jnp.mean(jnp.square(xf - mu), axis=-1, keepdims=True)
    y = (xf - mu) * lax.rsqrt(var + EPS) * g.astype(jnp.float32) + b.astype(jnp.float32)
    return y.astype(x.dtype)


def causal_dwconv(ext, w):
    return lax.conv_general_dilated(ext, w[:, None, :].astype(ext.dtype), (1,), 'VALID',
                                    dimension_numbers=('NWC', 'WIO', 'NWC'),
                                    feature_group_count=ext.shape[-1])


def pool_mixer(a, prev, start_pos, pool_w, pool_scale):
    n, t, _ = a.shape
    ext = jnp.concatenate([prev, a], axis=1)
    cs = jnp.cumsum(ext.astype(jnp.float32), axis=1)
    cs = jnp.concatenate([jnp.zeros_like(cs[:, :1]), cs], axis=1)
    pos = start_pos + jnp.arange(t)
    end = POOL_STATE + 1
    outs = []
    for gi, w in enumerate(POOL_WINDOWS):
        sl = slice(gi * GROUP, (gi + 1) * GROUP)
        s = cs[:, end:end + t, sl] - cs[:, end - w:end - w + t, sl]
        cnt = jnp.minimum(w, pos + 1).astype(jnp.float32)[None, :, None]
        outs.append(s / cnt - a[:, :, sl].astype(jnp.float32))
    p = jnp.stack(outs, axis=2).astype(a.dtype)
    p = jnp.einsum('ntgc,gcd->ntgd', p, pool_w).reshape(n, t, D_POOL) * pool_scale
    return p, ext[:, -POOL_STATE:]


def chunk_mixer(u, v, ln_g, ln_b, sg_w, sg_b):
    n, t, _ = u.shape
    vn = layernorm(v, ln_g, ln_b)
    tp = -(-t // CHUNK) * CHUNK
    vp = jnp.pad(vn, ((0, 0), (0, tp - t), (0, 0))).reshape(n, tp // CHUNK, CHUNK, SG_GROUPS, GROUP)
    mask = jnp.tril(jnp.ones((CHUNK, CHUNK), dtype=bool))
    w = jnp.where(mask[None], sg_w, jnp.zeros((), sg_w.dtype))
    mixed = jnp.einsum('gts,ncsgd->nctgd', w, vp) + sg_b.T[None, None, :, :, None]
    mixed = mixed.reshape(n, tp, D_SG)[:, :t]
    return u * mixed, vn


def mixer_block(h, pool_prev, sc_prev, cm_prev, start_pos, lp):
    n, t, _ = h.shape
    z = h @ lp['w_in']
    cuts = np.cumsum([D_POOL, D_SC, D_SC, D_SC, 2 * D_CM, D_SG, D_SG]).tolist()
    a, bg, cg, hb, glu, u, v, gates = jnp.split(z, cuts, axis=-1)
    pa, pool_new = pool_mixer(a, pool_prev, start_pos, lp['pool_w'], lp['pool_scale'])
    ya = pa @ lp['w_pool_out']
    ext_b = jnp.concatenate([sc_prev, cg * hb], axis=1)
    yb = (bg * causal_dwconv(ext_b, lp['sc_w'])) @ lp['w_sc_out']
    sc_new = ext_b[:, -(SC_WIDTH - 1):]
    g1, g2 = jnp.split(glu, 2, axis=-1)
    ext_c = jnp.concatenate([cm_prev, g1 * jax.nn.sigmoid(g2)], axis=1)
    c = causal_dwconv(ext_c, lp['cm_w']) + lp['cm_b']
    yc = jax.nn.silu(layernorm(c, lp['cm_ln_g'], lp['cm_ln_b'])) @ lp['w_cm_out']
    cm_new = ext_c[:, -(CM_WIDTH - 1):]
    pd, vn = chunk_mixer(u, v, lp['sg_ln_g'], lp['sg_ln_b'], lp['sg_w'], lp['sg_b'])
    yd = pd @ lp['w_sg_out']
    sig = jax.nn.sigmoid(gates.reshape(n, t, N_BRANCH, D_MODEL) + lp['b_gate'])
    merged = sig[:, :, 0] * ya + sig[:, :, 1] * yb + sig[:, :, 2] * yc + sig[:, :, 3] * yd
    return merged @ lp['w_o'], pool_new, sc_new, cm_new, vn


def memory_kv(mem, g_mem, w_xk, w_xv):
    n = mem.shape[0]
    m = rmsnorm(mem, g_mem)
    k = (m @ w_xk).reshape(n, MEM_LEN, X_HEADS, X_HEAD_DIM)
    v = (m @ w_xv).reshape(n, MEM_LEN, X_HEADS, X_HEAD_DIM)
    return k, v


def memory_attend(h, k, v, w_xq, w_xo):
    n, t, _ = h.shape
    q = (h @ w_xq).reshape(n, t, X_HEADS, X_HEAD_DIM)
    s = jnp.einsum('nthd,nmhd->nhtm', q, k).astype(jnp.float32) * (X_HEAD_DIM ** -0.5)
    p = jax.nn.softmax(s, axis=-1).astype(h.dtype)
    o = jnp.einsum('nhtm,nmhd->nthd', p, v).reshape(n, t, D_X)
    return o @ w_xo


def peer(h, w_pq, keys, tab_u, tab_v):
    n, t, d = h.shape
    blk = PEER_BLOCK if t % PEER_BLOCK == 0 else t
    hb = h.reshape(-1, blk, d)

    def one_block(xb):
        q = (xb @ w_pq).reshape(blk, PEER_HEADS, 2, PEER_DKEY // 2).astype(jnp.float32)
        s = jnp.einsum('thpc,hpnc->thpn', q, keys.astype(jnp.float32))
        sv, si = lax.top_k(s, PEER_TOPK)
        cand = sv[:, :, 0, :, None] + sv[:, :, 1, None, :]
        cv, ci = lax.top_k(cand.reshape(blk, PEER_HEADS, PEER_TOPK * PEER_TOPK), PEER_TOPK)
        i1 = jnp.take_along_axis(si[:, :, 0], ci // PEER_TOPK, axis=-1)
        i2 = jnp.take_along_axis(si[:, :, 1], ci % PEER_TOPK, axis=-1)
        eidx = i1 * PEER_NKEYS + i2
        gw = jax.nn.softmax(cv, axis=-1).astype(xb.dtype)
        ue = jnp.take(tab_u, eidx, axis=0)
        act = jax.nn.gelu(jnp.einsum('td,thkd->thk', xb, ue), approximate=False)
        ve = jnp.take(tab_v, eidx, axis=0)
        return jnp.einsum('thk,thkd->td', gw * act, ve)

    return lax.map(one_block, hb).reshape(n, t, d)


def trunk_layer(x, k_mem, v_mem, pool_prev, sc_prev, cm_prev, start_pos, lp):
    y, pool_new, sc_new, cm_new, vn = mixer_block(rmsnorm(x, lp['g_mix']), pool_prev, sc_prev,
                                                  cm_prev, start_pos, lp)
    x = x + y
    x = x + memory_attend(rmsnorm(x, lp['g_x']), k_mem, v_mem, lp['w_xq'], lp['w_xo'])
    x = x + peer(rmsnorm(x, lp['g_peer']), lp['w_pq'], lp['peer_keys'], lp['peer_u'], lp['peer_v'])
    return x, pool_new, sc_new, cm_new, vn


def setup_inputs(seed: int = 0) -> dict:
    key = jax.random.key(seed)
    ks = iter(jax.random.split(key, 48))
    D = D_MODEL
    L = DEPTH

    def nrm(shape, scale=1.0):
        return jax.random.normal(next(ks), shape, jnp.float32) * scale

    def gain(shape):
        return 1.0 + nrm(shape, 0.02)

    return {
        'x_prompt': nrm((BATCH, SEQ, D)),
        'x_sample': nrm((DEC_BATCH, DEC_SEQ, D)),
        'mem_prompt': nrm((BATCH, MEM_LEN, D)),
        'cache_mem_k': nrm((L, DEC_BATCH, MEM_LEN, X_HEADS, X_HEAD_DIM)),
        'cache_mem_v': nrm((L, DEC_BATCH, MEM_LEN, X_HEADS, X_HEAD_DIM)),
        'state_pool': nrm((L, DEC_BATCH, POOL_STATE, D_POOL)),
        'state_sconv': nrm((L, DEC_BATCH, SC_WIDTH - 1, D_SC)),
        'state_cconv': nrm((L, DEC_BATCH, CM_WIDTH - 1, D_CM)),
        'g_mix': gain((L, D)),
        'w_in': nrm((L, D, N_IN), D ** -0.5),
        'b_gate': nrm((L, N_BRANCH, D), 0.02),
        'pool_w': nrm((L, N_POOL_GROUPS, GROUP, GROUP), GROUP ** -0.5),
        'pool_scale': gain((L, D_POOL)),
        'w_pool_out': nrm((L, D_POOL, D), D_POOL ** -0.5),
        'sc_w': nrm((L, SC_WIDTH, D_SC), SC_WIDTH ** -0.5),
        'w_sc_out': nrm((L, D_SC, D), D_SC ** -0.5),
        'cm_w': nrm((L, CM_WIDTH, D_CM), CM_WIDTH ** -0.5),
        'cm_b': nrm((L, D_CM), 0.02),
        'cm_ln_g': gain((L, D_CM)),
        'cm_ln_b': nrm((L, D_CM), 0.02),
        'w_cm_out': nrm((L, D_CM, D), D_CM ** -0.5),
        'sg_ln_g': gain((L, D_SG)),
        'sg_ln_b': nrm((L, D_SG), 0.02),
        'sg_w': nrm((L, SG_GROUPS, CHUNK, CHUNK), CHUNK ** -0.5),
        'sg_b': gain((L, SG_GROUPS, CHUNK)),
        'w_sg_out': nrm((L, D_SG, D), D_SG ** -0.5),
        'w_o': nrm((L, D, D), D ** -0.5),
        'g_x': gain((L, D)),
        'g_mem': gain((L, D)),
        'w_xq': nrm((L, D, D_X), D ** -0.5),
        'w_xk': nrm((L, D, D_X), D ** -0.5),
        'w_xv': nrm((L, D, D_X), D ** -0.5),
        'w_xo': nrm((L, D_X, D), D_X ** -0.5),
        'g_peer': gain((L, D)),
        'w_pq': nrm((L, D, PEER_HEADS * PEER_DKEY), D ** -0.5),
        'peer_keys': nrm((L, PEER_HEADS, 2, PEER_NKEYS, PEER_DKEY // 2), (PEER_DKEY // 2) ** -0.5),
        'peer_u': nrm((L, N_EXPERTS, D), D ** -0.5),
        'peer_v': nrm((L, N_EXPERTS, D), (PEER_HEADS * PEER_TOPK) ** -0.5),
        'g_final': gain((D,)),
    }


def reference(x_prompt, x_sample, mem_prompt, cache_mem_k, cache_mem_v, state_pool, state_sconv,
              state_cconv, g_mix, w_in, b_gate, pool_w, pool_scale, w_pool_out, sc_w, w_sc_out,
              cm_w, cm_b, cm_ln_g, cm_ln_b, w_cm_out, sg_ln_g, sg_ln_b, sg_w, sg_b, w_sg_out, w_o,
              g_x, g_mem, w_xq, w_xk, w_xv, w_xo, g_peer, w_pq, peer_keys, peer_u, peer_v, g_final):
    xp, xs = x_prompt, x_sample
    nb = x_prompt.shape[0]
    mk_p, mv_p, pool_p, sc_p, cm_p = [], [], [], [], []
    pool_s, sc_s, cm_s, cv_s = [], [], [], []
    for l in range(DEPTH):
        lp = dict(g_mix=g_mix[l], w_in=w_in[l], b_gate=b_gate[l], pool_w=pool_w[l],
                  pool_scale=pool_scale[l], w_pool_out=w_pool_out[l], sc_w=sc_w[l],
                  w_sc_out=w_sc_out[l], cm_w=cm_w[l], cm_b=cm_b[l], cm_ln_g=cm_ln_g[l],
                  cm_ln_b=cm_ln_b[l], w_cm_out=w_cm_out[l], sg_ln_g=sg_ln_g[l], sg_ln_b=sg_ln_b[l],
                  sg_w=sg_w[l], sg_b=sg_b[l], w_sg_out=w_sg_out[l], w_o=w_o[l], g_x=g_x[l],
                  w_xq=w_xq[l], w_xo=w_xo[l], g_peer=g_peer[l], w_pq=w_pq[l],
                  peer_keys=peer_keys[l], peer_u=peer_u[l], peer_v=peer_v[l])
        k_p, v_p = memory_kv(mem_prompt, g_mem[l], w_xk[l], w_xv[l])
        xp, st_pool, st_sc, st_cm, _ = trunk_layer(
            xp, k_p, v_p,
            jnp.zeros((nb, POOL_STATE, D_POOL), xp.dtype),
            jnp.zeros((nb, SC_WIDTH - 1, D_SC), xp.dtype),
            jnp.zeros((nb, CM_WIDTH - 1, D_CM), xp.dtype),
            0, lp)
        mk_p.append(k_p)
        mv_p.append(v_p)
        pool_p.append(st_pool)
        sc_p.append(st_sc)
        cm_p.append(st_cm)
        xs, st_pool, st_sc, st_cm, vn = trunk_layer(
            xs, cache_mem_k[l], cache_mem_v[l], state_pool[l], state_sconv[l], state_cconv[l],
            PAST_LEN, lp)
        pool_s.append(st_pool)
        sc_s.append(st_sc)
        cm_s.append(st_cm)
        cv_s.append(vn)
    y_prompt = rmsnorm(xp, g_final)
    y_sample = rmsnorm(xs, g_final)
    return (y_prompt, y_sample, jnp.stack(mk_p), jnp.stack(mv_p), jnp.stack(pool_p), jnp.stack(sc_p),
            jnp.stack(cm_p), jnp.stack(pool_s), jnp.stack(sc_s), jnp.stack(cm_s), jnp.stack(cv_s))
```

```python
import functools
import math

import jax
import jax.numpy as jnp
from jax import lax
from jax.experimental import pallas as pl
from jax.experimental.pallas import tpu as pltpu

EPS = 1e-6
GROUP = 128
POOL_WINDOWS = (2, 4, 8, 16)
POOL_STATE = max(POOL_WINDOWS) - 1
SC_WIDTH = 3
CM_WIDTH = 31
D_BR = 4 * GROUP
N_BRANCH = 4
CHUNK = 128
X_HEADS = 4
PEER_HEADS = 8
PEER_TOPK = 16
HALO = 32
LANES = 128
VMEM_LIMIT = 56 * 2 ** 20

BF16 = jnp.bfloat16
F32 = jnp.float32
NEG_INF = float("-inf")


def _tile(n, pref, mult=8):
    best = None
    for t in range(mult, min(n, pref) + 1, mult):
        if n % t == 0:
            best = t
    assert best is not None, (n, pref, mult)
    return best


def _params(*sem):
    return pltpu.CompilerParams(dimension_semantics=sem, vmem_limit_bytes=VMEM_LIMIT)


def _rms(x, g):
    ms = jnp.mean(x * x, axis=-1, keepdims=True)
    return x * lax.rsqrt(ms + EPS) * g


def _ln(x, g, b):
    mu = jnp.mean(x, axis=-1, keepdims=True)
    xc = x - mu
    var = jnp.mean(xc * xc, axis=-1, keepdims=True)
    return xc * lax.rsqrt(var + EPS) * g + b


def _sigmoid(x):
    return 1.0 / (1.0 + jnp.exp(-x))


def _dot(a, b):
    return jnp.dot(a, b, preferred_element_type=F32)


def _norm_mm_kernel(x_ref, g_ref, w_ref, o_ref, hn_ref):
    @pl.when(pl.program_id(1) == 0)
    def _():
        hn_ref[...] = _rms(x_ref[...], g_ref[...]).astype(BF16)

    o_ref[...] = _dot(hn_ref[...], w_ref[...])


def norm_mm(x, g, w, *, tm, tn):
    t, d = x.shape
    n = w.shape[1]
    return pl.pallas_call(
        _norm_mm_kernel,
        grid=(t // tm, n // tn),
        in_specs=[pl.BlockSpec((tm, d), lambda i, j: (i, 0)),
                  pl.BlockSpec((1, d), lambda i, j: (0, 0)),
                  pl.BlockSpec((d, tn), lambda i, j: (0, j))],
        out_specs=pl.BlockSpec((tm, tn), lambda i, j: (i, j)),
        out_shape=jax.ShapeDtypeStruct((t, n), F32),
        scratch_shapes=[pltpu.VMEM((tm, d), BF16)],
        compiler_params=_params("parallel", "arbitrary"),
        name="norm_mm",
    )(x, g.reshape(1, d), w)


def _mm_res_kernel(a_ref, w_ref, r_ref, o_ref):
    o_ref[...] = r_ref[...] + _dot(a_ref[...].astype(BF16), w_ref[...])


def mm_res(a, w, res, *, tm, tn):
    t, k = a.shape
    n = w.shape[1]
    return pl.pallas_call(
        _mm_res_kernel,
        grid=(t // tm, n // tn),
        in_specs=[pl.BlockSpec((tm, k), lambda i, j: (i, 0)),
                  pl.BlockSpec((k, tn), lambda i, j: (0, j)),
                  pl.BlockSpec((tm, tn), lambda i, j: (i, j))],
        out_specs=pl.BlockSpec((tm, tn), lambda i, j: (i, j)),
        out_shape=jax.ShapeDtypeStruct((t, n), F32),
        compiler_params=_params("parallel", "parallel"),
        name="mm_res",
    )(a, w, res)


def _pool_branch(window_sums, a, cnts, poolw_ref, pscale):
    outs = []
    for gi in range(len(POOL_WINDOWS)):
        sl = slice(gi * GROUP, (gi + 1) * GROUP)
        pg = window_sums[gi] / cnts[gi] - a[:, sl]
        outs.append(_dot(pg.astype(BF16), poolw_ref[gi]))
    return jnp.concatenate(outs, axis=1) * pscale


def _mixer_prompt_kernel(z_ref, poolw_ref, pscale_ref, scw_ref, cmw_ref, cmb_ref, cmg_ref,
                         cmbeta_ref, sgg_ref, sgbeta_ref, sgw_ref, sgbias_ref,
                         p_ref, pool_o, sc_o, cm_o, ext_ref, *, ts):
    tb = pl.program_id(1)
    h0 = HALO

    @pl.when(tb == 0)
    def _():
        ext_ref[:, 0:h0, :] = jnp.zeros((3, h0, D_BR), F32)

    a = z_ref[:, 0:D_BR]
    bg = z_ref[:, D_BR:2 * D_BR]
    gated = z_ref[:, 2 * D_BR:3 * D_BR] * z_ref[:, 3 * D_BR:4 * D_BR]
    glu = z_ref[:, 4 * D_BR:5 * D_BR] * _sigmoid(z_ref[:, 5 * D_BR:6 * D_BR])
    ext_ref[0, h0:h0 + ts, :] = a
    ext_ref[1, h0:h0 + ts, :] = gated
    ext_ref[2, h0:h0 + ts, :] = glu

    pos = tb * ts + lax.broadcasted_iota(jnp.int32, (ts, 1), 0)
    sums, cnts = [], []
    for gi, w in enumerate(POOL_WINDOWS):
        sl = slice(gi * GROUP, (gi + 1) * GROUP)
        s = a[:, sl]
        for j in range(1, w):
            s = s + ext_ref[0, h0 - j:h0 - j + ts, sl]
        sums.append(s)
        cnts.append(jnp.minimum(w, pos + 1).astype(F32))
    p_ref[:, 0:D_BR] = _pool_branch(sums, a, cnts, poolw_ref, pscale_ref[...])

    conv = scw_ref[SC_WIDTH - 1:SC_WIDTH, :] * gated
    for k in range(SC_WIDTH - 1):
        off = h0 - (SC_WIDTH - 1) + k
        conv = conv + scw_ref[k:k + 1, :] * ext_ref[1, off:off + ts, :]
    p_ref[:, D_BR:2 * D_BR] = bg * conv

    c = cmw_ref[CM_WIDTH - 1:CM_WIDTH, :] * glu
    for k in range(CM_WIDTH - 1):
        off = h0 - (CM_WIDTH - 1) + k
        c = c + cmw_ref[k:k + 1, :] * ext_ref[2, off:off + ts, :]
    c = _ln(c + cmb_ref[...], cmg_ref[...], cmbeta_ref[...])
    p_ref[:, 2 * D_BR:3 * D_BR] = c * _sigmoid(c)

    vn = _ln(z_ref[:, 7 * D_BR:8 * D_BR], sgg_ref[...], sgbeta_ref[...])
    row = lax.broadcasted_iota(jnp.int32, (CHUNK, CHUNK), 0)
    col = lax.broadcasted_iota(jnp.int32, (CHUNK, CHUNK), 1)
    tril = row >= col
    for gi in range(D_BR // GROUP):
        sl = slice(gi * GROUP, (gi + 1) * GROUP)
        wg = jnp.where(tril, sgw_ref[gi], 0.0).astype(BF16)
        for ci in range(ts // CHUNK):
            rows = slice(ci * CHUNK, (ci + 1) * CHUNK)
            mixed = _dot(wg, vn[rows, sl].astype(BF16)) + sgbias_ref[:, sl]
            p_ref[rows, 3 * D_BR + gi * GROUP:3 * D_BR + (gi + 1) * GROUP] = (
                z_ref[rows, 6 * D_BR + gi * GROUP:6 * D_BR + (gi + 1) * GROUP] * mixed)

    end = h0 + ts
    pool_o[0] = ext_ref[0, end - POOL_STATE:end, :]
    sc_o[0] = ext_ref[1, end - (SC_WIDTH - 1):end, :]
    cm_o[0] = ext_ref[2, end - (CM_WIDTH - 1):end, :]
    ext_ref[:, 0:h0, :] = ext_ref[:, ts:ts + h0, :]


def mixer_prompt(z, lp, *, n_seq, seq, t_all, ts):
    nb = seq // ts
    full = lambda *shape: pl.BlockSpec(shape, lambda n, t: (0,) * len(shape))
    state = lambda rows: pl.BlockSpec((1, rows, D_BR), lambda n, t: (n, 0, 0))
    return pl.pallas_call(
        functools.partial(_mixer_prompt_kernel, ts=ts),
        grid=(n_seq, nb),
        in_specs=[pl.BlockSpec((ts, 8 * D_BR), lambda n, t: (n * nb + t, 0)),
                  full(4, GROUP, GROUP), full(1, D_BR), full(SC_WIDTH, D_BR),
                  full(CM_WIDTH, D_BR), full(1, D_BR), full(1, D_BR), full(1, D_BR),
                  full(1, D_BR), full(1, D_BR), full(4, CHUNK, CHUNK), full(CHUNK, D_BR)],
        out_specs=[pl.BlockSpec((ts, N_BRANCH * D_BR), lambda n, t: (n * nb + t, 0)),
                   state(POOL_STATE), state(SC_WIDTH - 1), state(CM_WIDTH - 1)],
        out_shape=[jax.ShapeDtypeStruct((t_all, N_BRANCH * D_BR), F32),
                   jax.ShapeDtypeStruct((n_seq, POOL_STATE, D_BR), F32),
                   jax.ShapeDtypeStruct((n_seq, SC_WIDTH - 1, D_BR), F32),
                   jax.ShapeDtypeStruct((n_seq, CM_WIDTH - 1, D_BR), F32)],
        scratch_shapes=[pltpu.VMEM((3, HALO + ts, D_BR), F32)],
        compiler_params=_params("parallel", "arbitrary"),
        name="mixer_prompt",
    )(z, lp["pool_w"], lp["pool_scale"], lp["sc_w"], lp["cm_w"], lp["cm_b"], lp["cm_ln_g"],
      lp["cm_ln_b"], lp["sg_ln_g"], lp["sg_ln_b"], lp["sg_w"], lp["sg_bias"])


def _mixer_sample_kernel(z_ref, pool_ref, sc_ref, cm_ref, poolw_ref, pscale_ref, scw_ref, cmw_ref,
                         cmb_ref, cmg_ref, cmbeta_ref, sgg_ref, sgbeta_ref, sgwts_ref, sgbias_ref,
                         p_hbm, p_ref, pool_o, sc_o, cm_o, vn_o, *, td, ns, start_pos):
    del p_hbm
    rows = lambda t: slice(t * ns, (t + 1) * ns)
    col = lambda k: slice(k * D_BR, (k + 1) * D_BR)

    ext_a = [pool_ref[j] for j in range(POOL_STATE)] + [z_ref[rows(t), col(0)] for t in range(td)]
    a_all = jnp.concatenate(ext_a[POOL_STATE:], axis=0)
    sums, cnts = [], []
    for gi, w in enumerate(POOL_WINDOWS):
        sl = slice(gi * GROUP, (gi + 1) * GROUP)
        per_t = []
        for t in range(td):
            s = ext_a[POOL_STATE + t][:, sl]
            for j in range(1, w):
                s = s + ext_a[POOL_STATE + t - j][:, sl]
            per_t.append(s)
        sums.append(jnp.concatenate(per_t, axis=0))
        cnt = jnp.concatenate(
            [jnp.full((ns, 1), float(min(w, start_pos + t + 1)), F32) for t in range(td)], axis=0)
        cnts.append(cnt)
    p_ref[:, col(0)] = _pool_branch(sums, a_all, cnts, poolw_ref, pscale_ref[...])
    for j in range(POOL_STATE):
        pool_o[j] = ext_a[td + j]

    ext_b = [sc_ref[j] for j in range(SC_WIDTH - 1)]
    ext_b += [z_ref[rows(t), col(2)] * z_ref[rows(t), col(3)] for t in range(td)]
    for t in range(td):
        conv = scw_ref[0:1, :] * ext_b[t]
        for k in range(1, SC_WIDTH):
            conv = conv + scw_ref[k:k + 1, :] * ext_b[t + k]
        p_ref[rows(t), col(1)] = z_ref[rows(t), col(1)] * conv
    for j in range(SC_WIDTH - 1):
        sc_o[j] = ext_b[td + j]

    ext_c = [cm_ref[j] for j in range(CM_WIDTH - 1)]
    ext_c += [z_ref[rows(t), col(4)] * _sigmoid(z_ref[rows(t), col(5)]) for t in range(td)]
    for t in range(td):
        c = cmw_ref[0:1, :] * ext_c[t]
        for k in range(1, CM_WIDTH):
            c = c + cmw_ref[k:k + 1, :] * ext_c[t + k]
        c = _ln(c + cmb_ref[...], cmg_ref[...], cmbeta_ref[...])
        p_ref[rows(t), col(2)] = c * _sigmoid(c)
    for j in range(CM_WIDTH - 1):
        cm_o[j] = ext_c[td + j]

    vn = [_ln(z_ref[rows(t), col(7)], sgg_ref[...], sgbeta_ref[...]) for t in range(td)]
    for t in range(td):
        vn_o[t] = vn[t]
        mixed = sgbias_ref[t:t + 1, :]
        for s in range(t + 1):
            mixed = mixed + sgwts_ref[t, s:s + 1, :] * vn[s]
        p_ref[rows(t), col(3)] = z_ref[rows(t), col(6)] * mixed


def mixer_sample(z, p_all, pool_tm, sc_tm, cm_tm, lp, *, td, ns, t_prompt, start_pos):
    blk = t_prompt // (td * ns)
    full = lambda *shape: pl.BlockSpec(shape, lambda i: (0,) * len(shape))
    tm_shape = lambda rows: jax.ShapeDtypeStruct((rows, ns, D_BR), F32)
    return pl.pallas_call(
        functools.partial(_mixer_sample_kernel, td=td, ns=ns, start_pos=start_pos),
        grid=(1,),
        in_specs=[pl.BlockSpec((td * ns, 8 * D_BR), lambda i: (blk, 0)),
                  full(POOL_STATE, ns, D_BR), full(SC_WIDTH - 1, ns, D_BR),
                  full(CM_WIDTH - 1, ns, D_BR),
                  full(4, GROUP, GROUP), full(1, D_BR), full(SC_WIDTH, D_BR),
                  full(CM_WIDTH, D_BR), full(1, D_BR), full(1, D_BR), full(1, D_BR),
                  full(1, D_BR), full(1, D_BR), full(td, td, D_BR), full(CHUNK, D_BR),
                  pl.BlockSpec(memory_space=pl.ANY)],
        out_specs=[pl.BlockSpec((td * ns, N_BRANCH * D_BR), lambda i: (blk, 0)),
                   full(POOL_STATE, ns, D_BR), full(SC_WIDTH - 1, ns, D_BR),
                   full(CM_WIDTH - 1, ns, D_BR), full(td, ns, D_BR)],
        out_shape=[jax.ShapeDtypeStruct(p_all.shape, F32), tm_shape(POOL_STATE),
                   tm_shape(SC_WIDTH - 1), tm_shape(CM_WIDTH - 1), tm_shape(td)],
        input_output_aliases={15: 0},
        compiler_params=_params("arbitrary"),
        name="mixer_sample",
    )(z, pool_tm, sc_tm, cm_tm, lp["pool_w"], lp["pool_scale"], lp["sc_w"], lp["cm_w"],
      lp["cm_b"], lp["cm_ln_g"], lp["cm_ln_b"], lp["sg_ln_g"], lp["sg_ln_b"], lp["sg_wts"],
      lp["sg_bias"], p_all)


def _gate_merge_kernel(g0_ref, g1_ref, g2_ref, g3_ref, p_ref, w_ref, b_ref, o_ref):
    acc = None
    for i, g_ref in enumerate((g0_ref, g1_ref, g2_ref, g3_ref)):
        y = _dot(p_ref[:, i * D_BR:(i + 1) * D_BR].astype(BF16), w_ref[i])
        term = _sigmoid(g_ref[...] + b_ref[i:i + 1, :]) * y
        acc = term if acc is None else acc + term
    o_ref[...] = acc.astype(BF16)


def gate_merge(z, p_all, w_out, b_gate, *, d, tm, tn):
    t = z.shape[0]
    first = N_BRANCH * 2 * D_BR // tn
    gate_spec = lambda i: pl.BlockSpec((tm, tn), lambda r, c: (r, first + i * (d // tn) + c))
    return pl.pallas_call(
        _gate_merge_kernel,
        grid=(t // tm, d // tn),
        in_specs=[gate_spec(0), gate_spec(1), gate_spec(2), gate_spec(3),
                  pl.BlockSpec((tm, N_BRANCH * D_BR), lambda r, c: (r, 0)),
                  pl.BlockSpec((N_BRANCH, D_BR, tn), lambda r, c: (0, 0, c)),
                  pl.BlockSpec((N_BRANCH, tn), lambda r, c: (0, c))],
        out_specs=pl.BlockSpec((tm, tn), lambda r, c: (r, c)),
        out_shape=jax.ShapeDtypeStruct((t, d), BF16),
        compiler_params=_params("parallel", "parallel"),
        name="gate_merge",
    )(z, z, z, z, p_all, w_out, b_gate)


def _softmax_rows(s):
    m = jnp.max(s, axis=-1, keepdims=True)
    p = jnp.exp(s - m)
    return p / jnp.sum(p, axis=-1, keepdims=True)


def _attn_prompt_kernel(q_ref, k_ref, v_ref, o_ref, *, scale):
    for h in range(X_HEADS):
        sl = slice(h * GROUP, (h + 1) * GROUP)
        s = lax.dot_general(q_ref[:, sl].astype(BF16), k_ref[0, :, sl].astype(BF16),
                            (((1,), (1,)), ((), ())), preferred_element_type=F32) * scale
        o_ref[:, sl] = _dot(_softmax_rows(s).astype(BF16), v_ref[0, :, sl].astype(BF16))


def attn_prompt(q, k, v, *, n_seq, seq, tq):
    t_all, dx = q.shape
    mem = k.shape[1]
    nb = seq // tq
    return pl.pallas_call(
        functools.partial(_attn_prompt_kernel, scale=GROUP ** -0.5),
        grid=(n_seq, nb),
        in_specs=[pl.BlockSpec((tq, dx), lambda n, t: (n * nb + t, 0)),
                  pl.BlockSpec((1, mem, dx), lambda n, t: (n, 0, 0)),
                  pl.BlockSpec((1, mem, dx), lambda n, t: (n, 0, 0))],
        out_specs=pl.BlockSpec((tq, dx), lambda n, t: (n * nb + t, 0)),
        out_shape=jax.ShapeDtypeStruct((t_all, dx), F32),
        compiler_params=_params("parallel", "parallel"),
        name="attn_prompt",
    )(q, k, v)


def _attn_sample_kernel(q_ref, k_ref, v_ref, o_ref, *, scale):
    for h in range(X_HEADS):
        sl = slice(h * GROUP, (h + 1) * GROUP)
        s = jnp.einsum("bqd,bkd->bqk", q_ref[:, :, sl].astype(BF16), k_ref[:, :, sl].astype(BF16),
                       preferred_element_type=F32) * scale
        o_ref[:, :, sl] = jnp.einsum("bqk,bkd->bqd", _softmax_rows(s).astype(BF16),
                                     v_ref[:, :, sl].astype(BF16), preferred_element_type=F32)


def attn_sample(q, k, v, *, sb):
    n_seq, rows, dx = q.shape
    mem = k.shape[1]
    return pl.pallas_call(
        functools.partial(_attn_sample_kernel, scale=GROUP ** -0.5),
        grid=(n_seq // sb,),
        in_specs=[pl.BlockSpec((sb, rows, dx), lambda i: (i, 0, 0)),
                  pl.BlockSpec((sb, mem, dx), lambda i: (i, 0, 0)),
                  pl.BlockSpec((sb, mem, dx), lambda i: (i, 0, 0))],
        out_specs=pl.BlockSpec((sb, rows, dx), lambda i: (i, 0, 0)),
        out_shape=jax.ShapeDtypeStruct((n_seq, rows, dx), F32),
        compiler_params=_params("parallel"),
        name="attn_sample",
    )(q, k, v)


def _norm_t_kernel(x_ref, g_ref, o_ref):
    o_ref[...] = _rms(x_ref[...], g_ref[...]).T.astype(BF16)


def norm_t(x, g, *, tb):
    t, d = x.shape
    return pl.pallas_call(
        _norm_t_kernel,
        grid=(t // tb,),
        in_specs=[pl.BlockSpec((tb, d), lambda i: (i, 0)), pl.BlockSpec((1, d), lambda i: (0, 0))],
        out_specs=pl.BlockSpec((d, tb), lambda i: (0, i)),
        out_shape=jax.ShapeDtypeStruct((d, t), BF16),
        compiler_params=_params("parallel"),
        name="norm_t",
    )(x, g.reshape(1, d))


_CAND = [(i, j) for i in range(PEER_TOPK + 1) for j in range(PEER_TOPK + 1)
         if (i + 1) * (j + 1) <= PEER_TOPK + 1]


def _peer_pre_kernel(ht_ref, wq_ref, keys_ref, s2_o, e2_o, th_o, e1_o, q_scr, s_scr, v_scr, *, tb):
    nk = GROUP
    q_scr[...] = _dot(wq_ref[...], ht_ref[...]).astype(BF16)
    for hp in range(2 * PEER_HEADS):
        s_scr[hp] = _dot(keys_ref[hp], q_scr[hp * nk:(hp + 1) * nk, :])

    idx = lax.broadcasted_iota(jnp.int32, (nk, LANES), 0)

    def chunk(ci, carry):
        lanes = pl.ds(pl.multiple_of(ci * LANES, LANES), LANES)
        for hp in range(2 * PEER_HEADS):
            h, p = divmod(hp, 2)
            cur = s_scr[hp, :, lanes]
            for r in range(PEER_TOPK + 1):
                m = jnp.max(cur, axis=0, keepdims=True)
                v_scr[p, r, h:h + 1, lanes] = m
                if r < PEER_TOPK:
                    first = jnp.min(jnp.where(cur == m, idx, nk), axis=0, keepdims=True)
                    cur = jnp.where(idx == first, NEG_INF, cur)
        v1 = [v_scr[0, r, :, lanes] for r in range(PEER_TOPK + 1)]
        v2 = [v_scr[1, r, :, lanes] for r in range(PEER_TOPK + 1)]
        cands = [v1[i] + v2[j] for i, j in _CAND]
        kth = jnp.full_like(cands[0], NEG_INF)
        nxt = jnp.full_like(cands[0], NEG_INF)
        for ck in cands:
            cnt = jnp.zeros_like(ck)
            for cl in cands:
                cnt = cnt + jnp.where(cl >= ck, 1.0, 0.0)
            kth = jnp.maximum(kth, jnp.where(cnt >= PEER_TOPK, ck, NEG_INF))
            nxt = jnp.maximum(nxt, jnp.where(cnt >= PEER_TOPK + 1, ck, NEG_INF))
        tau = 0.5 * (kth + nxt)
        top = cands[0]
        zsum = jnp.zeros_like(top)
        for ck in cands:
            zsum = zsum + jnp.where(ck >= tau, jnp.exp(ck - top), 0.0)
        for h in range(PEER_HEADS):
            s1 = s_scr[2 * h, :, lanes]
            s2 = s_scr[2 * h + 1, :, lanes]
            th_o[h, :, lanes] = tau[h:h + 1, :] - s1
            e1_o[h, :, lanes] = jnp.exp(s1 - v1[0][h:h + 1, :])
            s2_o[h * nk:(h + 1) * nk, lanes] = s2
            e2_o[h * nk:(h + 1) * nk, lanes] = jnp.exp(s2 - v2[0][h:h + 1, :]) / zsum[h:h + 1, :]
        return carry

    lax.fori_loop(0, tb // LANES, chunk, 0)


def peer_pre(ht, wq_t, keys, *, tb):
    d, t = ht.shape
    nq = wq_t.shape[0]
    hk = PEER_HEADS * GROUP
    flat = pl.BlockSpec((hk, tb), lambda i: (0, i))
    cube = pl.BlockSpec((PEER_HEADS, GROUP, tb), lambda i: (0, 0, i))
    return pl.pallas_call(
        functools.partial(_peer_pre_kernel, tb=tb),
        grid=(t // tb,),
        in_specs=[pl.BlockSpec((d, tb), lambda i: (0, i)),
                  pl.BlockSpec((nq, d), lambda i: (0, 0)),
                  pl.BlockSpec((2 * PEER_HEADS, GROUP, GROUP), lambda i: (0, 0, 0))],
        out_specs=[flat, flat, cube, cube],
        out_shape=[jax.ShapeDtypeStruct((hk, t), F32), jax.ShapeDtypeStruct((hk, t), F32),
                   jax.ShapeDtypeStruct((PEER_HEADS, GROUP, t), F32),
                   jax.ShapeDtypeStruct((PEER_HEADS, GROUP, t), F32)],
        scratch_shapes=[pltpu.VMEM((nq, tb), BF16),
                        pltpu.VMEM((2 * PEER_HEADS, GROUP, tb), F32),
                        pltpu.VMEM((2, PEER_TOPK + 1, PEER_HEADS, tb), F32)],
        compiler_params=_params("parallel"),
        name="peer_pre",
    )(ht, wq_t, keys)


def _peer_dense_kernel(ht_ref, u_ref, vt_ref, s2_ref, e2_ref, th_ref, e1_ref, x_ref, o_ref,
                       acc_ref, at_ref, gw_ref, *, tb, te):
    e = pl.program_id(1)

    @pl.when(e == 0)
    def _():
        acc_ref[...] = jnp.zeros_like(acc_ref)

    at_ref[...] = _dot(u_ref[...], ht_ref[...])

    def tile(ci, carry, a):
        rows = slice(a * GROUP, (a + 1) * GROUP)
        lanes = pl.ds(pl.multiple_of(ci * LANES, LANES), LANES)
        pre = at_ref[rows, lanes]
        act = 0.5 * pre * (1.0 + lax.erf(pre * math.sqrt(0.5)))
        w = jnp.zeros((GROUP, LANES), F32)
        for h in range(PEER_HEADS):
            hs = slice(h * GROUP, (h + 1) * GROUP)
            keep = s2_ref[hs, lanes] >= th_ref[h, a:a + 1, lanes]
            w = w + jnp.where(keep, e2_ref[hs, lanes], 0.0) * e1_ref[h, a:a + 1, lanes]
        gw_ref[rows, lanes] = (w * act).astype(BF16)
        return carry

    for a in range(te // GROUP):
        lax.fori_loop(0, tb // LANES, functools.partial(tile, a=a), 0)
    acc_ref[...] += _dot(vt_ref[...], gw_ref[...])

    @pl.when(e == pl.num_programs(1) - 1)
    def _():
        o_ref[...] = x_ref[...] + acc_ref[...].T


def peer_dense(ht, u, vt, s2, e2, theta, e1, x, *, tb, te):
    d, t = ht.shape
    n_exp = u.shape[0]
    a_blk = te // GROUP
    once = pl.Buffered(1)
    side = pl.BlockSpec((PEER_HEADS * GROUP, tb), lambda i, e: (0, i), pipeline_mode=once)
    sel = pl.BlockSpec((PEER_HEADS, a_blk, tb), lambda i, e: (0, e, i))
    return pl.pallas_call(
        functools.partial(_peer_dense_kernel, tb=tb, te=te),
        grid=(t // tb, n_exp // te),
        in_specs=[pl.BlockSpec((d, tb), lambda i, e: (0, i), pipeline_mode=once),
                  pl.BlockSpec((te, d), lambda i, e: (e, 0)),
                  pl.BlockSpec((d, te), lambda i, e: (0, e)),
                  side, side, sel, sel,
                  pl.BlockSpec((tb, d), lambda i, e: (i, 0), pipeline_mode=once)],
        out_specs=pl.BlockSpec((tb, d), lambda i, e: (i, 0)),
        out_shape=jax.ShapeDtypeStruct((t, d), F32),
        scratch_shapes=[pltpu.VMEM((d, tb), F32), pltpu.VMEM((te, tb), F32),
                        pltpu.VMEM((te, tb), BF16)],
        compiler_params=_params("parallel", "arbitrary"),
        name="peer_dense",
    )(ht, u, vt, s2, e2, theta, e1, x)


def _final_norm_kernel(x_ref, g_ref, o_ref):
    o_ref[...] = _rms(x_ref[...], g_ref[...])


def final_norm(x, g, *, tm):
    t, d = x.shape
    return pl.pallas_call(
        _final_norm_kernel,
        grid=(t // tm,),
        in_specs=[pl.BlockSpec((tm, d), lambda i: (i, 0)), pl.BlockSpec((1, d), lambda i: (0, 0))],
        out_specs=pl.BlockSpec((tm, d), lambda i: (i, 0)),
        out_shape=jax.ShapeDtypeStruct((t, d), F32),
        compiler_params=_params("parallel"),
        name="final_norm",
    )(x, g.reshape(1, d))


def _to_time_major(s):
    return jnp.swapaxes(s, 0, 1)


def kernel(x_prompt, x_sample, mem_prompt, cache_mem_k, cache_mem_v, state_pool, state_sconv, state_cconv, g_mix, w_in, b_gate, pool_w, pool_scale, w_pool_out, sc_w, w_sc_out, cm_w, cm_b, cm_ln_g, cm_ln_b, w_cm_out, sg_ln_g, sg_ln_b, sg_w, sg_b, w_sg_out, w_o, g_x, g_mem, w_xq, w_xk, w_xv, w_xo, g_peer, w_pq, peer_keys, peer_u, peer_v, g_final):
    nb, seq, d = x_prompt.shape
    ns, td, _ = x_sample.shape
    depth = w_in.shape[0]
    mem_len = mem_prompt.shape[1]
    past_len = 16384
    t_prompt = nb * seq
    t_all = t_prompt + ns * td
    dx = X_HEADS * GROUP

    tm_big = _tile(t_all, 1088, 16)
    tm_mid = _tile(t_all, 544, 16)
    tn_in = _tile(w_in.shape[2], 1024, 128)
    tn_d = _tile(d, 512, 128)
    ts = _tile(seq, 256, CHUNK)
    tq = _tile(seq, 512, 8)
    tb = _tile(t_all, 512, LANES)
    tb_pre = _tile(t_all, 256, LANES)
    te = _tile(peer_u.shape[1], 1024, 8 * GROUP)

    x = jnp.concatenate([x_prompt.reshape(t_prompt, d),
                         _to_time_major(x_sample).reshape(ns * td, d)], axis=0)
    mem2d = mem_prompt.reshape(nb * mem_len, d)
    col1 = lambda v: v.reshape(1, -1)

    outs = {k: [] for k in ("mk", "mv", "pool_p", "sc_p", "cm_p", "pool_s", "sc_s", "cm_s", "cv_s")}
    for l in range(depth):
        lp = dict(
            pool_w=pool_w[l].astype(BF16), pool_scale=col1(pool_scale[l]), sc_w=sc_w[l],
            cm_w=cm_w[l], cm_b=col1(cm_b[l]), cm_ln_g=col1(cm_ln_g[l]), cm_ln_b=col1(cm_ln_b[l]),
            sg_ln_g=col1(sg_ln_g[l]), sg_ln_b=col1(sg_ln_b[l]), sg_w=sg_w[l],
            sg_bias=jnp.repeat(sg_b[l].T, GROUP, axis=1),
            sg_wts=jnp.repeat(jnp.transpose(sg_w[l][:, :td, :td], (1, 2, 0)), GROUP, axis=2))
        w_branch = jnp.stack([w_pool_out[l], w_sc_out[l], w_cm_out[l], w_sg_out[l]]).astype(BF16)

        z = norm_mm(x, g_mix[l], w_in[l].astype(BF16), tm=tm_big, tn=tn_in)
        p_all, pool_p, sc_p, cm_p = mixer_prompt(z, lp, n_seq=nb, seq=seq, t_all=t_all, ts=ts)
        p_all, pool_s, sc_s, cm_s, vn_s = mixer_sample(
            z, p_all, _to_time_major(state_pool[l]), _to_time_major(state_sconv[l]),
            _to_time_major(state_cconv[l]), lp, td=td, ns=ns, t_prompt=t_prompt, start_pos=past_len)
        merged = gate_merge(z, p_all, w_branch, b_gate[l], d=d, tm=tm_mid, tn=tn_d)
        x = mm_res(merged, w_o[l].astype(BF16), x, tm=tm_big, tn=tn_d)

        q = norm_mm(x, g_x[l], w_xq[l].astype(BF16), tm=tm_big, tn=dx)
        tm_mem = _tile(nb * mem_len, 1024, 16)
        k_p = norm_mm(mem2d, g_mem[l], w_xk[l].astype(BF16), tm=tm_mem, tn=dx)
        v_p = norm_mm(mem2d, g_mem[l], w_xv[l].astype(BF16), tm=tm_mem, tn=dx)
        o = attn_prompt(q, k_p.reshape(nb, mem_len, dx), v_p.reshape(nb, mem_len, dx),
                        n_seq=nb, seq=seq, tq=tq)
        q_s = jnp.swapaxes(q[t_prompt:].reshape(td, ns, dx), 0, 1)
        q_s = jnp.pad(q_s, ((0, 0), (0, 8 - td), (0, 0)))
        o_s = attn_sample(q_s, cache_mem_k[l].reshape(ns, mem_len, dx),
                          cache_mem_v[l].reshape(ns, mem_len, dx), sb=_tile(ns, 16, 1))
        o = o.at[t_prompt:].set(jnp.swapaxes(o_s[:, :td], 0, 1).reshape(ns * td, dx))
        x = mm_res(o, w_xo[l].astype(BF16), x, tm=tm_big, tn=tn_d)

        ht = norm_t(x, g_peer[l], tb=tb)
        keys = peer_keys[l].reshape(2 * PEER_HEADS, GROUP, GROUP).astype(BF16)
        s2, e2, theta, e1 = peer_pre(ht, w_pq[l].T.astype(BF16), keys, tb=tb_pre)
        x = peer_dense(ht, peer_u[l].astype(BF16), peer_v[l].T.astype(BF16), s2, e2, theta, e1, x,
                       tb=tb, te=te)

        outs["mk"].append(k_p.reshape(nb, mem_len, X_HEADS, GROUP))
        outs["mv"].append(v_p.reshape(nb, mem_len, X_HEADS, GROUP))
        outs["pool_p"].append(pool_p)
        outs["sc_p"].append(sc_p)
        outs["cm_p"].append(cm_p)
        outs["pool_s"].append(_to_time_major(pool_s))
        outs["sc_s"].append(_to_time_major(sc_s))
        outs["cm_s"].append(_to_time_major(cm_s))
        outs["cv_s"].append(_to_time_major(vn_s))

    y = final_norm(x, g_final, tm=tm_big)
    y_prompt = y[:t_prompt].reshape(nb, seq, d)
    y_sample = _to_time_major(y[t_prompt:].reshape(td, ns, d))
    st = lambda k: jnp.stack(outs[k])
    return (y_prompt, y_sample, st("mk"), st("mv"), st("pool_p"), st("sc_p"), st("cm_p"),
            st("pool_s"), st("sc_s"), st("cm_s"), st("cv_s"))
```

```python
import functools
import math

import jax
import jax.numpy as jnp
from jax import lax
from jax.experimental import pallas as pl
from jax.experimental.pallas import tpu as pltpu

EPS = 1e-6
GROUP = 128
POOL_WINDOWS = (2, 4, 8, 16)
POOL_STATE = max(POOL_WINDOWS) - 1
SC_WIDTH = 3
CM_WIDTH = 31
D_BR = 4 * GROUP
N_BRANCH = 4
CHUNK = 128
X_HEADS = 4
PEER_HEADS = 8
PEER_TOPK = 16
HALO = 32
LANES = 128
ROWS = 32
VMEM_LIMIT = 56 * 2 ** 20

BF16 = jnp.bfloat16
F32 = jnp.float32
NEG_INF = float("-inf")
LOG2E = math.log2(math.e)


def _tile(n, pref, mult=8):
    best = None
    for t in range(mult, min(n, pref) + 1, mult):
        if n % t == 0:
            best = t
    assert best is not None, (n, pref, mult)
    return best


def _params(*sem):
    return pltpu.CompilerParams(dimension_semantics=sem, vmem_limit_bytes=VMEM_LIMIT)


def _rms(x, g):
    ms = jnp.mean(x * x, axis=-1, keepdims=True)
    return x * lax.rsqrt(ms + EPS) * g


def _ln(x, g, b):
    mu = jnp.mean(x, axis=-1, keepdims=True)
    xc = x - mu
    var = jnp.mean(xc * xc, axis=-1, keepdims=True)
    return xc * lax.rsqrt(var + EPS) * g + b


def _sigmoid(x):
    return 1.0 / (1.0 + jnp.exp(-x))


def _dot(a, b):
    return jnp.dot(a, b, preferred_element_type=F32)


def _norm_mm_kernel(x_ref, g_ref, w_ref, o_ref, hn_ref):
    @pl.when(pl.program_id(1) == 0)
    def _():
        hn_ref[...] = _rms(x_ref[...], g_ref[...]).astype(BF16)

    o_ref[...] = _dot(hn_ref[...], w_ref[...])


def norm_mm(x, g, w, *, tm, tn):
    t, d = x.shape
    n = w.shape[1]
    return pl.pallas_call(
        _norm_mm_kernel,
        grid=(t // tm, n // tn),
        in_specs=[pl.BlockSpec((tm, d), lambda i, j: (i, 0)),
                  pl.BlockSpec((1, d), lambda i, j: (0, 0)),
                  pl.BlockSpec((d, tn), lambda i, j: (0, j))],
        out_specs=pl.BlockSpec((tm, tn), lambda i, j: (i, j)),
        out_shape=jax.ShapeDtypeStruct((t, n), F32),
        scratch_shapes=[pltpu.VMEM((tm, d), BF16)],
        compiler_params=_params("parallel", "arbitrary"),
        name="norm_mm",
    )(x, g.reshape(1, d), w)


def _mm_res_kernel(a_ref, w_ref, r_ref, o_ref):
    o_ref[...] = r_ref[...] + _dot(a_ref[...].astype(BF16), w_ref[...])


def mm_res(a, w, res, *, tm, tn):
    t, k = a.shape
    n = w.shape[1]
    return pl.pallas_call(
        _mm_res_kernel,
        grid=(t // tm, n // tn),
        in_specs=[pl.BlockSpec((tm, k), lambda i, j: (i, 0)),
                  pl.BlockSpec((k, tn), lambda i, j: (0, j)),
                  pl.BlockSpec((tm, tn), lambda i, j: (i, j))],
        out_specs=pl.BlockSpec((tm, tn), lambda i, j: (i, j)),
        out_shape=jax.ShapeDtypeStruct((t, n), F32),
        compiler_params=_params("parallel", "parallel"),
        name="mm_res",
    )(a, w, res)


def _pool_branch(window_sums, a, cnts, poolw_ref, pscale):
    outs = []
    for gi in range(len(POOL_WINDOWS)):
        sl = slice(gi * GROUP, (gi + 1) * GROUP)
        pg = window_sums[gi] / cnts[gi] - a[:, sl]
        outs.append(_dot(pg.astype(BF16), poolw_ref[gi]))
    return jnp.concatenate(outs, axis=1) * pscale


def _mixer_prompt_kernel(z_ref, poolw_ref, pscale_ref, scw_ref, cmw_ref, cmb_ref, cmg_ref,
                         cmbeta_ref, sgg_ref, sgbeta_ref, sgw_ref, sgbias_ref,
                         p_ref, pool_o, sc_o, cm_o, ext_ref, *, ts):
    tb = pl.program_id(1)
    h0 = HALO

    @pl.when(tb == 0)
    def _():
        ext_ref[:, 0:h0, :] = jnp.zeros((3, h0, D_BR), F32)

    a = z_ref[:, 0:D_BR]
    bg = z_ref[:, D_BR:2 * D_BR]
    gated = z_ref[:, 2 * D_BR:3 * D_BR] * z_ref[:, 3 * D_BR:4 * D_BR]
    glu = z_ref[:, 4 * D_BR:5 * D_BR] * _sigmoid(z_ref[:, 5 * D_BR:6 * D_BR])
    ext_ref[0, h0:h0 + ts, :] = a
    ext_ref[1, h0:h0 + ts, :] = gated
    ext_ref[2, h0:h0 + ts, :] = glu

    pos = tb * ts + lax.broadcasted_iota(jnp.int32, (ts, 1), 0)
    sums, cnts = [], []
    for gi, w in enumerate(POOL_WINDOWS):
        sl = slice(gi * GROUP, (gi + 1) * GROUP)
        s = a[:, sl]
        for j in range(1, w):
            s = s + ext_ref[0, h0 - j:h0 - j + ts, sl]
        sums.append(s)
        cnts.append(jnp.minimum(w, pos + 1).astype(F32))
    p_ref[:, 0:D_BR] = _pool_branch(sums, a, cnts, poolw_ref, pscale_ref[...])

    conv = scw_ref[SC_WIDTH - 1:SC_WIDTH, :] * gated
    for k in range(SC_WIDTH - 1):
        off = h0 - (SC_WIDTH - 1) + k
        conv = conv + scw_ref[k:k + 1, :] * ext_ref[1, off:off + ts, :]
    p_ref[:, D_BR:2 * D_BR] = bg * conv

    c = cmw_ref[CM_WIDTH - 1:CM_WIDTH, :] * glu
    for k in range(CM_WIDTH - 1):
        off = h0 - (CM_WIDTH - 1) + k
        c = c + cmw_ref[k:k + 1, :] * ext_ref[2, off:off + ts, :]
    c = _ln(c + cmb_ref[...], cmg_ref[...], cmbeta_ref[...])
    p_ref[:, 2 * D_BR:3 * D_BR] = c * _sigmoid(c)

    vn = _ln(z_ref[:, 7 * D_BR:8 * D_BR], sgg_ref[...], sgbeta_ref[...])
    row = lax.broadcasted_iota(jnp.int32, (CHUNK, CHUNK), 0)
    col = lax.broadcasted_iota(jnp.int32, (CHUNK, CHUNK), 1)
    tril = row >= col
    for gi in range(D_BR // GROUP):
        sl = slice(gi * GROUP, (gi + 1) * GROUP)
        wg = jnp.where(tril, sgw_ref[gi], 0.0).astype(BF16)
        for ci in range(ts // CHUNK):
            rows = slice(ci * CHUNK, (ci + 1) * CHUNK)
            mixed = _dot(wg, vn[rows, sl].astype(BF16)) + sgbias_ref[:, sl]
            p_ref[rows, 3 * D_BR + gi * GROUP:3 * D_BR + (gi + 1) * GROUP] = (
                z_ref[rows, 6 * D_BR + gi * GROUP:6 * D_BR + (gi + 1) * GROUP] * mixed)

    end = h0 + ts
    pool_o[0] = ext_ref[0, end - POOL_STATE:end, :]
    sc_o[0] = ext_ref[1, end - (SC_WIDTH - 1):end, :]
    cm_o[0] = ext_ref[2, end - (CM_WIDTH - 1):end, :]
    ext_ref[:, 0:h0, :] = ext_ref[:, ts:ts + h0, :]


def mixer_prompt(z, lp, *, n_seq, seq, t_all, ts):
    nb = seq // ts
    full = lambda *shape: pl.BlockSpec(shape, lambda n, t: (0,) * len(shape))
    state = lambda rows: pl.BlockSpec((1, rows, D_BR), lambda n, t: (n, 0, 0))
    return pl.pallas_call(
        functools.partial(_mixer_prompt_kernel, ts=ts),
        grid=(n_seq, nb),
        in_specs=[pl.BlockSpec((ts, 8 * D_BR), lambda n, t: (n * nb + t, 0)),
                  full(4, GROUP, GROUP), full(1, D_BR), full(SC_WIDTH, D_BR),
                  full(CM_WIDTH, D_BR), full(1, D_BR), full(1, D_BR), full(1, D_BR),
                  full(1, D_BR), full(1, D_BR), full(4, CHUNK, CHUNK), full(CHUNK, D_BR)],
        out_specs=[pl.BlockSpec((ts, N_BRANCH * D_BR), lambda n, t: (n * nb + t, 0)),
                   state(POOL_STATE), state(SC_WIDTH - 1), state(CM_WIDTH - 1)],
        out_shape=[jax.ShapeDtypeStruct((t_all, N_BRANCH * D_BR), F32),
                   jax.ShapeDtypeStruct((n_seq, POOL_STATE, D_BR), F32),
                   jax.ShapeDtypeStruct((n_seq, SC_WIDTH - 1, D_BR), F32),
                   jax.ShapeDtypeStruct((n_seq, CM_WIDTH - 1, D_BR), F32)],
        scratch_shapes=[pltpu.VMEM((3, HALO + ts, D_BR), F32)],
        compiler_params=_params("parallel", "arbitrary"),
        name="mixer_prompt",
    )(z, lp["pool_w"], lp["pool_scale"], lp["sc_w"], lp["cm_w"], lp["cm_b"], lp["cm_ln_g"],
      lp["cm_ln_b"], lp["sg_ln_g"], lp["sg_ln_b"], lp["sg_w"], lp["sg_bias"])


def _mixer_sample_kernel(z_ref, pool_ref, sc_ref, cm_ref, poolw_ref, pscale_ref, scw_ref, cmw_ref,
                         cmb_ref, cmg_ref, cmbeta_ref, sgg_ref, sgbeta_ref, sgwts_ref, sgbias_ref,
                         p_hbm, p_ref, pool_o, sc_o, cm_o, vn_o, *, td, ns, start_pos):
    del p_hbm
    rows = lambda t: slice(t * ns, (t + 1) * ns)
    col = lambda k: slice(k * D_BR, (k + 1) * D_BR)

    ext_a = [pool_ref[j] for j in range(POOL_STATE)] + [z_ref[rows(t), col(0)] for t in range(td)]
    a_all = jnp.concatenate(ext_a[POOL_STATE:], axis=0)
    sums, cnts = [], []
    for gi, w in enumerate(POOL_WINDOWS):
        sl = slice(gi * GROUP, (gi + 1) * GROUP)
        per_t = []
        for t in range(td):
            s = ext_a[POOL_STATE + t][:, sl]
            for j in range(1, w):
                s = s + ext_a[POOL_STATE + t - j][:, sl]
            per_t.append(s)
        sums.append(jnp.concatenate(per_t, axis=0))
        cnt = jnp.concatenate(
            [jnp.full((ns, 1), float(min(w, start_pos + t + 1)), F32) for t in range(td)], axis=0)
        cnts.append(cnt)
    p_ref[:, col(0)] = _pool_branch(sums, a_all, cnts, poolw_ref, pscale_ref[...])
    for j in range(POOL_STATE):
        pool_o[j] = ext_a[td + j]

    ext_b = [sc_ref[j] for j in range(SC_WIDTH - 1)]
    ext_b += [z_ref[rows(t), col(2)] * z_ref[rows(t), col(3)] for t in range(td)]
    for t in range(td):
        conv = scw_ref[0:1, :] * ext_b[t]
        for k in range(1, SC_WIDTH):
            conv = conv + scw_ref[k:k + 1, :] * ext_b[t + k]
        p_ref[rows(t), col(1)] = z_ref[rows(t), col(1)] * conv
    for j in range(SC_WIDTH - 1):
        sc_o[j] = ext_b[td + j]

    ext_c = [cm_ref[j] for j in range(CM_WIDTH - 1)]
    ext_c += [z_ref[rows(t), col(4)] * _sigmoid(z_ref[rows(t), col(5)]) for t in range(td)]
    for t in range(td):
        c = cmw_ref[0:1, :] * ext_c[t]
        for k in range(1, CM_WIDTH):
            c = c + cmw_ref[k:k + 1, :] * ext_c[t + k]
        c = _ln(c + cmb_ref[...], cmg_ref[...], cmbeta_ref[...])
        p_ref[rows(t), col(2)] = c * _sigmoid(c)
    for j in range(CM_WIDTH - 1):
        cm_o[j] = ext_c[td + j]

    vn = [_ln(z_ref[rows(t), col(7)], sgg_ref[...], sgbeta_ref[...]) for t in range(td)]
    for t in range(td):
        vn_o[t] = vn[t]
        mixed = sgbias_ref[t:t + 1, :]
        for s in range(t + 1):
            mixed = mixed + sgwts_ref[t, s:s + 1, :] * vn[s]
        p_ref[rows(t), col(3)] = z_ref[rows(t), col(6)] * mixed


def mixer_sample(z, p_all, pool_tm, sc_tm, cm_tm, lp, *, td, ns, t_prompt, start_pos):
    blk = t_prompt // (td * ns)
    full = lambda *shape: pl.BlockSpec(shape, lambda i: (0,) * len(shape))
    tm_shape = lambda rows: jax.ShapeDtypeStruct((rows, ns, D_BR), F32)
    return pl.pallas_call(
        functools.partial(_mixer_sample_kernel, td=td, ns=ns, start_pos=start_pos),
        grid=(1,),
        in_specs=[pl.BlockSpec((td * ns, 8 * D_BR), lambda i: (blk, 0)),
                  full(POOL_STATE, ns, D_BR), full(SC_WIDTH - 1, ns, D_BR),
                  full(CM_WIDTH - 1, ns, D_BR),
                  full(4, GROUP, GROUP), full(1, D_BR), full(SC_WIDTH, D_BR),
                  full(CM_WIDTH, D_BR), full(1, D_BR), full(1, D_BR), full(1, D_BR),
                  full(1, D_BR), full(1, D_BR), full(td, td, D_BR), full(CHUNK, D_BR),
                  pl.BlockSpec(memory_space=pl.ANY)],
        out_specs=[pl.BlockSpec((td * ns, N_BRANCH * D_BR), lambda i: (blk, 0)),
                   full(POOL_STATE, ns, D_BR), full(SC_WIDTH - 1, ns, D_BR),
                   full(CM_WIDTH - 1, ns, D_BR), full(td, ns, D_BR)],
        out_shape=[jax.ShapeDtypeStruct(p_all.shape, F32), tm_shape(POOL_STATE),
                   tm_shape(SC_WIDTH - 1), tm_shape(CM_WIDTH - 1), tm_shape(td)],
        input_output_aliases={15: 0},
        compiler_params=_params("arbitrary"),
        name="mixer_sample",
    )(z, pool_tm, sc_tm, cm_tm, lp["pool_w"], lp["pool_scale"], lp["sc_w"], lp["cm_w"],
      lp["cm_b"], lp["cm_ln_g"], lp["cm_ln_b"], lp["sg_ln_g"], lp["sg_ln_b"], lp["sg_wts"],
      lp["sg_bias"], p_all)


def _gate_merge_kernel(g0_ref, g1_ref, g2_ref, g3_ref, p_ref, w_ref, b_ref, o_ref):
    acc = None
    for i, g_ref in enumerate((g0_ref, g1_ref, g2_ref, g3_ref)):
        y = _dot(p_ref[:, i * D_BR:(i + 1) * D_BR].astype(BF16), w_ref[i])
        term = _sigmoid(g_ref[...] + b_ref[i:i + 1, :]) * y
        acc = term if acc is None else acc + term
    o_ref[...] = acc.astype(BF16)


def gate_merge(z, p_all, w_out, b_gate, *, d, tm, tn):
    t = z.shape[0]
    first = N_BRANCH * 2 * D_BR // tn
    gate_spec = lambda i: pl.BlockSpec((tm, tn), lambda r, c: (r, first + i * (d // tn) + c))
    return pl.pallas_call(
        _gate_merge_kernel,
        grid=(t // tm, d // tn),
        in_specs=[gate_spec(0), gate_spec(1), gate_spec(2), gate_spec(3),
                  pl.BlockSpec((tm, N_BRANCH * D_BR), lambda r, c: (r, 0)),
                  pl.BlockSpec((N_BRANCH, D_BR, tn), lambda r, c: (0, 0, c)),
                  pl.BlockSpec((N_BRANCH, tn), lambda r, c: (0, c))],
        out_specs=pl.BlockSpec((tm, tn), lambda r, c: (r, c)),
        out_shape=jax.ShapeDtypeStruct((t, d), BF16),
        compiler_params=_params("parallel", "parallel"),
        name="gate_merge",
    )(z, z, z, z, p_all, w_out, b_gate)


def _softmax_rows(s):
    m = jnp.max(s, axis=-1, keepdims=True)
    p = jnp.exp(s - m)
    return p / jnp.sum(p, axis=-1, keepdims=True)


def _attn_prompt_kernel(q_ref, k_ref, v_ref, o_ref, *, scale):
    for h in range(X_HEADS):
        sl = slice(h * GROUP, (h + 1) * GROUP)
        s = lax.dot_general(q_ref[:, sl].astype(BF16), k_ref[0, :, sl].astype(BF16),
                            (((1,), (1,)), ((), ())), preferred_element_type=F32) * scale
        o_ref[:, sl] = _dot(_softmax_rows(s).astype(BF16), v_ref[0, :, sl].astype(BF16))


def attn_prompt(q, k, v, *, n_seq, seq, tq):
    t_all, dx = q.shape
    mem = k.shape[1]
    nb = seq // tq
    return pl.pallas_call(
        functools.partial(_attn_prompt_kernel, scale=GROUP ** -0.5),
        grid=(n_seq, nb),
        in_specs=[pl.BlockSpec((tq, dx), lambda n, t: (n * nb + t, 0)),
                  pl.BlockSpec((1, mem, dx), lambda n, t: (n, 0, 0)),
                  pl.BlockSpec((1, mem, dx), lambda n, t: (n, 0, 0))],
        out_specs=pl.BlockSpec((tq, dx), lambda n, t: (n * nb + t, 0)),
        out_shape=jax.ShapeDtypeStruct((t_all, dx), F32),
        compiler_params=_params("parallel", "parallel"),
        name="attn_prompt",
    )(q, k, v)


def _attn_sample_kernel(q_ref, k_ref, v_ref, o_ref, *, scale):
    for h in range(X_HEADS):
        sl = slice(h * GROUP, (h + 1) * GROUP)
        s = jnp.einsum("bqd,bkd->bqk", q_ref[:, :, sl].astype(BF16), k_ref[:, :, sl].astype(BF16),
                       preferred_element_type=F32) * scale
        o_ref[:, :, sl] = jnp.einsum("bqk,bkd->bqd", _softmax_rows(s).astype(BF16),
                                     v_ref[:, :, sl].astype(BF16), preferred_element_type=F32)


def attn_sample(q, k, v, *, sb):
    n_seq, rows, dx = q.shape
    mem = k.shape[1]
    return pl.pallas_call(
        functools.partial(_attn_sample_kernel, scale=GROUP ** -0.5),
        grid=(n_seq // sb,),
        in_specs=[pl.BlockSpec((sb, rows, dx), lambda i: (i, 0, 0)),
                  pl.BlockSpec((sb, mem, dx), lambda i: (i, 0, 0)),
                  pl.BlockSpec((sb, mem, dx), lambda i: (i, 0, 0))],
        out_specs=pl.BlockSpec((sb, rows, dx), lambda i: (i, 0, 0)),
        out_shape=jax.ShapeDtypeStruct((n_seq, rows, dx), F32),
        compiler_params=_params("parallel"),
        name="attn_sample",
    )(q, k, v)


def _norm_t_kernel(x_ref, g_ref, o_ref):
    o_ref[...] = _rms(x_ref[...], g_ref[...]).T.astype(BF16)


def norm_t(x, g, *, tb):
    t, d = x.shape
    return pl.pallas_call(
        _norm_t_kernel,
        grid=(t // tb,),
        in_specs=[pl.BlockSpec((tb, d), lambda i: (i, 0)), pl.BlockSpec((1, d), lambda i: (0, 0))],
        out_specs=pl.BlockSpec((d, tb), lambda i: (0, i)),
        out_shape=jax.ShapeDtypeStruct((d, t), BF16),
        compiler_params=_params("parallel"),
        name="norm_t",
    )(x, g.reshape(1, d))


_CAND = [(i, j) for i in range(PEER_TOPK + 1) for j in range(PEER_TOPK + 1)
         if (i + 1) * (j + 1) <= PEER_TOPK + 1]


def _peer_pre_kernel(ht_ref, wq_ref, keys_ref, s2_o, th_o, lam_o, q_scr, s_scr, v_scr, *, tb):
    nk = GROUP
    q_scr[...] = _dot(wq_ref[...], ht_ref[...]).astype(BF16)
    for hp in range(2 * PEER_HEADS):
        s_scr[hp] = _dot(keys_ref[hp], q_scr[hp * nk:(hp + 1) * nk, :])

    idx = lax.broadcasted_iota(jnp.int32, (nk, LANES), 0)

    def chunk(ci, carry):
        lanes = pl.ds(pl.multiple_of(ci * LANES, LANES), LANES)
        for hp in range(2 * PEER_HEADS):
            h, p = divmod(hp, 2)
            cur = s_scr[hp, :, lanes]
            for r in range(PEER_TOPK + 1):
                m = jnp.max(cur, axis=0, keepdims=True)
                v_scr[p, r, h:h + 1, lanes] = m
                if r < PEER_TOPK:
                    first = jnp.min(jnp.where(cur == m, idx, nk), axis=0, keepdims=True)
                    cur = jnp.where(idx == first, NEG_INF, cur)
        v1 = [v_scr[0, r, :, lanes] for r in range(PEER_TOPK + 1)]
        v2 = [v_scr[1, r, :, lanes] for r in range(PEER_TOPK + 1)]
        cands = [v1[i] + v2[j] for i, j in _CAND]
        kth = jnp.full_like(cands[0], NEG_INF)
        nxt = jnp.full_like(cands[0], NEG_INF)
        for ck in cands:
            cnt = jnp.zeros_like(ck)
            for cl in cands:
                cnt = cnt + jnp.where(cl >= ck, 1.0, 0.0)
            kth = jnp.maximum(kth, jnp.where(cnt >= PEER_TOPK, ck, NEG_INF))
            nxt = jnp.maximum(nxt, jnp.where(cnt >= PEER_TOPK + 1, ck, NEG_INF))
        tau = 0.5 * (kth + nxt)
        top = cands[0]
        zsum = jnp.zeros_like(top)
        for ck in cands:
            zsum = zsum + jnp.where(ck >= tau, jnp.exp(ck - top), 0.0)
        shift = top + jnp.log(zsum)
        for h in range(PEER_HEADS):
            s1 = s_scr[2 * h, :, lanes]
            th_o[h, :, lanes] = (tau[h:h + 1, :] - s1) * LOG2E
            lam_o[h, :, lanes] = (s1 - shift[h:h + 1, :]) * LOG2E
            s2_o[h * nk:(h + 1) * nk, lanes] = s_scr[2 * h + 1, :, lanes] * LOG2E
        return carry

    lax.fori_loop(0, tb // LANES, chunk, 0)


def peer_pre(ht, wq_t, keys, *, tb):
    d, t = ht.shape
    nq = wq_t.shape[0]
    hk = PEER_HEADS * GROUP
    flat = pl.BlockSpec((hk, tb), lambda i: (0, i))
    cube = pl.BlockSpec((PEER_HEADS, GROUP, tb), lambda i: (0, 0, i))
    return pl.pallas_call(
        functools.partial(_peer_pre_kernel, tb=tb),
        grid=(t // tb,),
        in_specs=[pl.BlockSpec((d, tb), lambda i: (0, i)),
                  pl.BlockSpec((nq, d), lambda i: (0, 0)),
                  pl.BlockSpec((2 * PEER_HEADS, GROUP, GROUP), lambda i: (0, 0, 0))],
        out_specs=[flat, cube, cube],
        out_shape=[jax.ShapeDtypeStruct((hk, t), F32),
                   jax.ShapeDtypeStruct((PEER_HEADS, GROUP, t), F32),
                   jax.ShapeDtypeStruct((PEER_HEADS, GROUP, t), F32)],
        scratch_shapes=[pltpu.VMEM((nq, tb), BF16),
                        pltpu.VMEM((2 * PEER_HEADS, GROUP, tb), F32),
                        pltpu.VMEM((2, PEER_TOPK + 1, PEER_HEADS, tb), F32)],
        compiler_params=_params("parallel"),
        name="peer_pre",
    )(ht, wq_t, keys)


def _peer_dense_kernel(ht_ref, htn_ref, ua_ref, ub_ref, vt_ref, s2_ref, th_ref, lam_ref, x_ref,
                       o_ref, acc_ref, at_ref, gw_ref, *, tb, te, ne):
    s = pl.program_id(0)
    e_cur = jnp.maximum(s - 1, 0) % ne
    half = te // 2
    n_a = half // GROUP

    @pl.when(s == 0)
    def _():
        at_ref[0] = jnp.zeros((half, tb), F32)

    @pl.when(e_cur == 0)
    def _():
        acc_ref[...] = jnp.zeros_like(acc_ref)

    def weighted_activations(k):
        for c in range(tb // LANES):
            lanes = slice(c * LANES, (c + 1) * LANES)
            for r in range(GROUP // ROWS):
                w = [jnp.zeros((ROWS, LANES), F32) for _ in range(n_a)]
                for h in range(PEER_HEADS):
                    s2 = s2_ref[h * GROUP + r * ROWS:h * GROUP + (r + 1) * ROWS, lanes]
                    for a in range(n_a):
                        ag = k * n_a + a
                        keep = s2 >= th_ref[h, ag:ag + 1, lanes]
                        w[a] = w[a] + jnp.where(keep, jnp.exp2(s2 + lam_ref[h, ag:ag + 1, lanes]), 0.0)
                for a in range(n_a):
                    rows = slice(a * GROUP + r * ROWS, a * GROUP + (r + 1) * ROWS)
                    pre = at_ref[k, rows, lanes]
                    act = 0.5 * pre * (1.0 + lax.erf(pre * math.sqrt(0.5)))
                    gw_ref[k, rows, lanes] = (w[a] * act).astype(BF16)

    def accumulate(k):
        acc_ref[...] += _dot(vt_ref[:, k * half:(k + 1) * half], gw_ref[k])

    weighted_activations(0)
    at_ref[1] = _dot(ub_ref[...], ht_ref[...])
    weighted_activations(1)
    accumulate(0)
    at_ref[0] = _dot(ua_ref[...], htn_ref[...])
    accumulate(1)

    @pl.when(jnp.logical_and(e_cur == ne - 1, s > 0))
    def _():
        o_ref[...] = x_ref[...] + acc_ref[...].T


def peer_dense(ht, u, vt, s2, theta, lam, x, *, tb, te):
    d, t = ht.shape
    ne = u.shape[0] // te
    n_tiles = (t // tb) * ne
    half = te // 2
    a_blk = te // GROUP
    once = pl.Buffered(1)
    cur = lambda s: jnp.clip(s - 1, 0, n_tiles - 1)
    nxt = lambda s: jnp.minimum(s, n_tiles - 1)
    return pl.pallas_call(
        functools.partial(_peer_dense_kernel, tb=tb, te=te, ne=ne),
        grid=(n_tiles + 1,),
        in_specs=[pl.BlockSpec((d, tb), lambda s: (0, cur(s) // ne), pipeline_mode=once),
                  pl.BlockSpec((d, tb), lambda s: (0, nxt(s) // ne), pipeline_mode=once),
                  pl.BlockSpec((half, d), lambda s: (2 * (nxt(s) % ne), 0)),
                  pl.BlockSpec((half, d), lambda s: (2 * (cur(s) % ne) + 1, 0)),
                  pl.BlockSpec((d, te), lambda s: (0, cur(s) % ne)),
                  pl.BlockSpec((PEER_HEADS * GROUP, tb), lambda s: (0, cur(s) // ne),
                               pipeline_mode=once),
                  pl.BlockSpec((PEER_HEADS, a_blk, tb), lambda s: (0, cur(s) % ne, cur(s) // ne)),
                  pl.BlockSpec((PEER_HEADS, a_blk, tb), lambda s: (0, cur(s) % ne, cur(s) // ne)),
                  pl.BlockSpec((tb, d), lambda s: (cur(s) // ne, 0), pipeline_mode=once)],
        out_specs=pl.BlockSpec((tb, d), lambda s: (cur(s) // ne, 0)),
        out_shape=jax.ShapeDtypeStruct((t, d), F32),
        scratch_shapes=[pltpu.VMEM((d, tb), F32), pltpu.VMEM((2, half, tb), F32),
                        pltpu.VMEM((2, half, tb), BF16)],
        compiler_params=_params("arbitrary"),
        name="peer_dense",
    )(ht, ht, u, u, vt, s2, theta, lam, x)


def _final_norm_kernel(x_ref, g_ref, o_ref):
    o_ref[...] = _rms(x_ref[...], g_ref[...])


def final_norm(x, g, *, tm):
    t, d = x.shape
    return pl.pallas_call(
        _final_norm_kernel,
        grid=(t // tm,),
        in_specs=[pl.BlockSpec((tm, d), lambda i: (i, 0)), pl.BlockSpec((1, d), lambda i: (0, 0))],
        out_specs=pl.BlockSpec((tm, d), lambda i: (i, 0)),
        out_shape=jax.ShapeDtypeStruct((t, d), F32),
        compiler_params=_params("parallel"),
        name="final_norm",
    )(x, g.reshape(1, d))


def _to_time_major(s):
    return jnp.swapaxes(s, 0, 1)


def kernel(x_prompt, x_sample, mem_prompt, cache_mem_k, cache_mem_v, state_pool, state_sconv, state_cconv, g_mix, w_in, b_gate, pool_w, pool_scale, w_pool_out, sc_w, w_sc_out, cm_w, cm_b, cm_ln_g, cm_ln_b, w_cm_out, sg_ln_g, sg_ln_b, sg_w, sg_b, w_sg_out, w_o, g_x, g_mem, w_xq, w_xk, w_xv, w_xo, g_peer, w_pq, peer_keys, peer_u, peer_v, g_final):
    nb, seq, d = x_prompt.shape
    ns, td, _ = x_sample.shape
    depth = w_in.shape[0]
    mem_len = mem_prompt.shape[1]
    past_len = 16384
    t_prompt = nb * seq
    t_all = t_prompt + ns * td
    dx = X_HEADS * GROUP

    tm_big = _tile(t_all, 1088, 16)
    tm_mid = _tile(t_all, 544, 16)
    tn_in = _tile(w_in.shape[2], 1024, 128)
    tn_d = _tile(d, 512, 128)
    ts = _tile(seq, 256, CHUNK)
    tq = _tile(seq, 512, 8)
    tb = _tile(t_all, 512, LANES)
    tb_pre = _tile(t_all, 256, LANES)
    te = _tile(peer_u.shape[1], 1024, 8 * GROUP)

    x = jnp.concatenate([x_prompt.reshape(t_prompt, d),
                         _to_time_major(x_sample).reshape(ns * td, d)], axis=0)
    mem2d = mem_prompt.reshape(nb * mem_len, d)
    col1 = lambda v: v.reshape(1, -1)

    outs = {k: [] for k in ("mk", "mv", "pool_p", "sc_p", "cm_p", "pool_s", "sc_s", "cm_s", "cv_s")}
    for l in range(depth):
        lp = dict(
            pool_w=pool_w[l].astype(BF16), pool_scale=col1(pool_scale[l]), sc_w=sc_w[l],
            cm_w=cm_w[l], cm_b=col1(cm_b[l]), cm_ln_g=col1(cm_ln_g[l]), cm_ln_b=col1(cm_ln_b[l]),
            sg_ln_g=col1(sg_ln_g[l]), sg_ln_b=col1(sg_ln_b[l]), sg_w=sg_w[l],
            sg_bias=jnp.repeat(sg_b[l].T, GROUP, axis=1),
            sg_wts=jnp.repeat(jnp.transpose(sg_w[l][:, :td, :td], (1, 2, 0)), GROUP, axis=2))
        w_branch = jnp.stack([w_pool_out[l], w_sc_out[l], w_cm_out[l], w_sg_out[l]]).astype(BF16)

        z = norm_mm(x, g_mix[l], w_in[l].astype(BF16), tm=tm_big, tn=tn_in)
        p_all, pool_p, sc_p, cm_p = mixer_prompt(z, lp, n_seq=nb, seq=seq, t_all=t_all, ts=ts)
        p_all, pool_s, sc_s, cm_s, vn_s = mixer_sample(
            z, p_all, _to_time_major(state_pool[l]), _to_time_major(state_sconv[l]),
            _to_time_major(state_cconv[l]), lp, td=td, ns=ns, t_prompt=t_prompt, start_pos=past_len)
        merged = gate_merge(z, p_all, w_branch, b_gate[l], d=d, tm=tm_mid, tn=tn_d)
        x = mm_res(merged, w_o[l].astype(BF16), x, tm=tm_big, tn=tn_d)

        q = norm_mm(x, g_x[l], w_xq[l].astype(BF16), tm=tm_big, tn=dx)
        tm_mem = _tile(nb * mem_len, 1024, 16)
        k_p = norm_mm(mem2d, g_mem[l], w_xk[l].astype(BF16), tm=tm_mem, tn=dx)
        v_p = norm_mm(mem2d, g_mem[l], w_xv[l].astype(BF16), tm=tm_mem, tn=dx)
        o = attn_prompt(q, k_p.reshape(nb, mem_len, dx), v_p.reshape(nb, mem_len, dx),
                        n_seq=nb, seq=seq, tq=tq)
        q_s = jnp.swapaxes(q[t_prompt:].reshape(td, ns, dx), 0, 1)
        q_s = jnp.pad(q_s, ((0, 0), (0, 8 - td), (0, 0)))
        o_s = attn_sample(q_s, cache_mem_k[l].reshape(ns, mem_len, dx),
                          cache_mem_v[l].reshape(ns, mem_len, dx), sb=_tile(ns, 16, 1))
        o = o.at[t_prompt:].set(jnp.swapaxes(o_s[:, :td], 0, 1).reshape(ns * td, dx))
        x = mm_res(o, w_xo[l].astype(BF16), x, tm=tm_big, tn=tn_d)

        ht = norm_t(x, g_peer[l], tb=tb)
        keys = peer_keys[l].reshape(2 * PEER_HEADS, GROUP, GROUP).astype(BF16)
        s2, theta, lam = peer_pre(ht, w_pq[l].T.astype(BF16), keys, tb=tb_pre)
        x = peer_dense(ht, peer_u[l].astype(BF16), peer_v[l].T.astype(BF16), s2, theta, lam, x,
                       tb=tb, te=te)

        outs["mk"].append(k_p.reshape(nb, mem_len, X_HEADS, GROUP))
        outs["mv"].append(v_p.reshape(nb, mem_len, X_HEADS, GROUP))
        outs["pool_p"].append(pool_p)
        outs["sc_p"].append(sc_p)
        outs["cm_p"].append(cm_p)
        outs["pool_s"].append(_to_time_major(pool_s))
        outs["sc_s"].append(_to_time_major(sc_s))
        outs["cm_s"].append(_to_time_major(cm_s))
        outs["cv_s"].append(_to_time_major(vn_s))

    y = final_norm(x, g_final, tm=tm_big)
    y_prompt = y[:t_prompt].reshape(nb, seq, d)
    y_sample = _to_time_major(y[t_prompt:].reshape(td, ns, d))
    st = lambda k: jnp.stack(outs[k])
    return (y_prompt, y_sample, st("mk"), st("mv"), st("pool_p"), st("sc_p"), st("cm_p"),
            st("pool_s"), st("sc_s"), st("cm_s"), st("cv_s"))
```

```python
import functools
import math

import jax
import jax.numpy as jnp
from jax import lax
from jax.experimental import pallas as pl
from jax.experimental.pallas import tpu as pltpu

EPS = 1e-6
GROUP = 128
POOL_WINDOWS = (2, 4, 8, 16)
POOL_STATE = max(POOL_WINDOWS) - 1
SC_WIDTH = 3
CM_WIDTH = 31
D_BR = 4 * GROUP
N_BRANCH = 4
CHUNK = 128
X_HEADS = 4
PEER_HEADS = 8
PEER_TOPK = 16
HALO = 32
LANES = 128
ROWS = 32
VMEM_LIMIT = 56 * 2 ** 20

BF16 = jnp.bfloat16
F32 = jnp.float32
NEG_INF = float("-inf")
LOG2E = math.log2(math.e)


def _tile(n, pref, mult=8):
    best = None
    for t in range(mult, min(n, pref) + 1, mult):
        if n % t == 0:
            best = t
    assert best is not None, (n, pref, mult)
    return best


def _params(*sem):
    return pltpu.CompilerParams(dimension_semantics=sem, vmem_limit_bytes=VMEM_LIMIT)


def _rms(x, g):
    ms = jnp.mean(x * x, axis=-1, keepdims=True)
    return x * lax.rsqrt(ms + EPS) * g


def _ln(x, g, b):
    mu = jnp.mean(x, axis=-1, keepdims=True)
    xc = x - mu
    var = jnp.mean(xc * xc, axis=-1, keepdims=True)
    return xc * lax.rsqrt(var + EPS) * g + b


def _sigmoid(x):
    return 1.0 / (1.0 + jnp.exp(-x))


def _dot(a, b):
    return jnp.dot(a, b, preferred_element_type=F32)


def _norm_mm_kernel(x_ref, g_ref, w_ref, o_ref, hn_ref):
    @pl.when(pl.program_id(1) == 0)
    def _():
        hn_ref[...] = _rms(x_ref[...], g_ref[...]).astype(BF16)

    o_ref[...] = _dot(hn_ref[...], w_ref[...])


def norm_mm(x, g, w, l, *, tm, tn):
    t, d = x.shape
    n = w.shape[2]
    return pl.pallas_call(
        _norm_mm_kernel,
        grid=(t // tm, n // tn),
        in_specs=[pl.BlockSpec((tm, d), lambda i, j: (i, 0)),
                  pl.BlockSpec((1, d), lambda i, j: (0, 0)),
                  pl.BlockSpec((None, d, tn), lambda i, j: (l, 0, j))],
        out_specs=pl.BlockSpec((tm, tn), lambda i, j: (i, j)),
        out_shape=jax.ShapeDtypeStruct((t, n), F32),
        scratch_shapes=[pltpu.VMEM((tm, d), BF16)],
        compiler_params=_params("parallel", "arbitrary"),
        name="norm_mm",
    )(x, g.reshape(1, d), w)


def _rows_of(i, n_head, head_ref, tail_ref):
    if tail_ref is None:
        return head_ref[...]
    return jnp.where(i < n_head, head_ref[...], tail_ref[...])


def _mm_res_kernel(a_ref, w_ref, r_ref, o_ref):
    o_ref[...] = r_ref[...] + _dot(a_ref[...].astype(BF16), w_ref[...])


def _mm_res_tail_kernel(a_ref, at_ref, w_ref, r_ref, o_ref, *, n_head):
    a = _rows_of(pl.program_id(0), n_head, a_ref, at_ref)
    o_ref[...] = r_ref[...] + _dot(a.astype(BF16), w_ref[...])


def mm_res(a, w, l, res, *, tm, tn, a_tail=None):
    t = res.shape[0]
    k, n = w.shape[1:]
    row = pl.BlockSpec((tm, tn), lambda i, j: (i, j))
    wspec = pl.BlockSpec((None, k, tn), lambda i, j: (l, 0, j))
    if a_tail is None:
        body, srcs = _mm_res_kernel, (a,)
        aspecs = [pl.BlockSpec((tm, k), lambda i, j: (i, 0))]
    else:
        n_head = a.shape[0] // tm
        assert a.shape[0] == n_head * tm and a_tail.shape[0] == tm and t == (n_head + 1) * tm
        body, srcs = functools.partial(_mm_res_tail_kernel, n_head=n_head), (a, a_tail)
        aspecs = [pl.BlockSpec((tm, k), lambda i, j: (jnp.minimum(i, n_head - 1), 0)),
                  pl.BlockSpec((tm, k), lambda i, j: (0, 0))]
    return pl.pallas_call(
        body,
        grid=(t // tm, n // tn),
        in_specs=aspecs + [wspec, row],
        out_specs=row,
        out_shape=jax.ShapeDtypeStruct((t, n), F32),
        compiler_params=_params("parallel", "parallel"),
        name="mm_res",
    )(*srcs, w, res)


def _pool_branch(window_sums, a, cnts, poolw_ref, pscale):
    outs = []
    for gi in range(len(POOL_WINDOWS)):
        sl = slice(gi * GROUP, (gi + 1) * GROUP)
        pg = window_sums[gi] / cnts[gi] - a[:, sl]
        outs.append(_dot(pg.astype(BF16), poolw_ref[gi]))
    return jnp.concatenate(outs, axis=1) * pscale


def _mixer_prompt_kernel(z_ref, poolw_ref, pscale_ref, scw_ref, cmw_ref, cmb_ref, cmg_ref,
                         cmbeta_ref, sgg_ref, sgbeta_ref, sgw_ref, sgbias_ref,
                         p_ref, pool_o, sc_o, cm_o, ext_ref, *, ts):
    tb = pl.program_id(1)
    h0 = HALO

    @pl.when(tb == 0)
    def _():
        ext_ref[:, 0:h0, :] = jnp.zeros((3, h0, D_BR), F32)

    a = z_ref[:, 0:D_BR]
    bg = z_ref[:, D_BR:2 * D_BR]
    gated = z_ref[:, 2 * D_BR:3 * D_BR] * z_ref[:, 3 * D_BR:4 * D_BR]
    glu = z_ref[:, 4 * D_BR:5 * D_BR] * _sigmoid(z_ref[:, 5 * D_BR:6 * D_BR])
    ext_ref[0, h0:h0 + ts, :] = a
    ext_ref[1, h0:h0 + ts, :] = gated
    ext_ref[2, h0:h0 + ts, :] = glu

    pos = tb * ts + lax.broadcasted_iota(jnp.int32, (ts, 1), 0)
    sums, cnts = [], []
    for gi, w in enumerate(POOL_WINDOWS):
        sl = slice(gi * GROUP, (gi + 1) * GROUP)
        s = a[:, sl]
        for j in range(1, w):
            s = s + ext_ref[0, h0 - j:h0 - j + ts, sl]
        sums.append(s)
        cnts.append(jnp.minimum(w, pos + 1).astype(F32))
    p_ref[:, 0:D_BR] = _pool_branch(sums, a, cnts, poolw_ref, pscale_ref[...])

    conv = scw_ref[SC_WIDTH - 1:SC_WIDTH, :] * gated
    for k in range(SC_WIDTH - 1):
        off = h0 - (SC_WIDTH - 1) + k
        conv = conv + scw_ref[k:k + 1, :] * ext_ref[1, off:off + ts, :]
    p_ref[:, D_BR:2 * D_BR] = bg * conv

    c = cmw_ref[CM_WIDTH - 1:CM_WIDTH, :] * glu
    for k in range(CM_WIDTH - 1):
        off = h0 - (CM_WIDTH - 1) + k
        c = c + cmw_ref[k:k + 1, :] * ext_ref[2, off:off + ts, :]
    c = _ln(c + cmb_ref[...], cmg_ref[...], cmbeta_ref[...])
    p_ref[:, 2 * D_BR:3 * D_BR] = c * _sigmoid(c)

    vn = _ln(z_ref[:, 7 * D_BR:8 * D_BR], sgg_ref[...], sgbeta_ref[...])
    row = lax.broadcasted_iota(jnp.int32, (CHUNK, CHUNK), 0)
    col = lax.broadcasted_iota(jnp.int32, (CHUNK, CHUNK), 1)
    tril = row >= col
    for gi in range(D_BR // GROUP):
        sl = slice(gi * GROUP, (gi + 1) * GROUP)
        wg = jnp.where(tril, sgw_ref[gi], 0.0).astype(BF16)
        for ci in range(ts // CHUNK):
            rows = slice(ci * CHUNK, (ci + 1) * CHUNK)
            mixed = _dot(wg, vn[rows, sl].astype(BF16)) + sgbias_ref[:, sl]
            p_ref[rows, 3 * D_BR + gi * GROUP:3 * D_BR + (gi + 1) * GROUP] = (
                z_ref[rows, 6 * D_BR + gi * GROUP:6 * D_BR + (gi + 1) * GROUP] * mixed)

    end = h0 + ts
    pool_o[0] = ext_ref[0, end - POOL_STATE:end, :]
    sc_o[0] = ext_ref[1, end - (SC_WIDTH - 1):end, :]
    cm_o[0] = ext_ref[2, end - (CM_WIDTH - 1):end, :]
    ext_ref[:, 0:h0, :] = ext_ref[:, ts:ts + h0, :]


def mixer_prompt(z, lp, *, n_seq, seq, ts):
    nb = seq // ts
    full = lambda *shape: pl.BlockSpec(shape, lambda n, t: (0,) * len(shape))
    state = lambda rows: pl.BlockSpec((1, rows, D_BR), lambda n, t: (n, 0, 0))
    return pl.pallas_call(
        functools.partial(_mixer_prompt_kernel, ts=ts),
        grid=(n_seq, nb),
        in_specs=[pl.BlockSpec((ts, 8 * D_BR), lambda n, t: (n * nb + t, 0)),
                  full(4, GROUP, GROUP), full(1, D_BR), full(SC_WIDTH, D_BR),
                  full(CM_WIDTH, D_BR), full(1, D_BR), full(1, D_BR), full(1, D_BR),
                  full(1, D_BR), full(1, D_BR), full(4, CHUNK, CHUNK), full(CHUNK, D_BR)],
        out_specs=[pl.BlockSpec((ts, N_BRANCH * D_BR), lambda n, t: (n * nb + t, 0)),
                   state(POOL_STATE), state(SC_WIDTH - 1), state(CM_WIDTH - 1)],
        out_shape=[jax.ShapeDtypeStruct((n_seq * seq, N_BRANCH * D_BR), F32),
                   jax.ShapeDtypeStruct((n_seq, POOL_STATE, D_BR), F32),
                   jax.ShapeDtypeStruct((n_seq, SC_WIDTH - 1, D_BR), F32),
                   jax.ShapeDtypeStruct((n_seq, CM_WIDTH - 1, D_BR), F32)],
        scratch_shapes=[pltpu.VMEM((3, HALO + ts, D_BR), F32)],
        compiler_params=_params("parallel", "arbitrary"),
        name="mixer_prompt",
    )(z, lp["pool_w"], lp["pool_scale"], lp["sc_w"], lp["cm_w"], lp["cm_b"], lp["cm_ln_g"],
      lp["cm_ln_b"], lp["sg_ln_g"], lp["sg_ln_b"], lp["sg_w"], lp["sg_bias"])


def _mixer_sample_kernel(z_ref, pool_ref, sc_ref, cm_ref, poolw_ref, pscale_ref, scw_ref, cmw_ref,
                         cmb_ref, cmg_ref, cmbeta_ref, sgg_ref, sgbeta_ref, sgwts_ref, sgbias_ref,
                         p_ref, pool_o, sc_o, cm_o, vn_o, *, td, ns, start_pos):
    rows = lambda t: slice(t * ns, (t + 1) * ns)
    col = lambda k: slice(k * D_BR, (k + 1) * D_BR)

    ext_a = [pool_ref[j] for j in range(POOL_STATE)] + [z_ref[rows(t), col(0)] for t in range(td)]
    a_all = jnp.concatenate(ext_a[POOL_STATE:], axis=0)
    sums, cnts = [], []
    for gi, w in enumerate(POOL_WINDOWS):
        sl = slice(gi * GROUP, (gi + 1) * GROUP)
        per_t = []
        for t in range(td):
            s = ext_a[POOL_STATE + t][:, sl]
            for j in range(1, w):
                s = s + ext_a[POOL_STATE + t - j][:, sl]
            per_t.append(s)
        sums.append(jnp.concatenate(per_t, axis=0))
        cnt = jnp.concatenate(
            [jnp.full((ns, 1), float(min(w, start_pos + t + 1)), F32) for t in range(td)], axis=0)
        cnts.append(cnt)
    p_ref[:, col(0)] = _pool_branch(sums, a_all, cnts, poolw_ref, pscale_ref[...])
    for j in range(POOL_STATE):
        pool_o[j] = ext_a[td + j]

    ext_b = [sc_ref[j] for j in range(SC_WIDTH - 1)]
    ext_b += [z_ref[rows(t), col(2)] * z_ref[rows(t), col(3)] for t in range(td)]
    for t in range(td):
        conv = scw_ref[0:1, :] * ext_b[t]
        for k in range(1, SC_WIDTH):
            conv = conv + scw_ref[k:k + 1, :] * ext_b[t + k]
        p_ref[rows(t), col(1)] = z_ref[rows(t), col(1)] * conv
    for j in range(SC_WIDTH - 1):
        sc_o[j] = ext_b[td + j]

    ext_c = [cm_ref[j] for j in range(CM_WIDTH - 1)]
    ext_c += [z_ref[rows(t), col(4)] * _sigmoid(z_ref[rows(t), col(5)]) for t in range(td)]
    for t in range(td):
        c = cmw_ref[0:1, :] * ext_c[t]
        for k in range(1, CM_WIDTH):
            c = c + cmw_ref[k:k + 1, :] * ext_c[t + k]
        c = _ln(c + cmb_ref[...], cmg_ref[...], cmbeta_ref[...])
        p_ref[rows(t), col(2)] = c * _sigmoid(c)
    for j in range(CM_WIDTH - 1):
        cm_o[j] = ext_c[td + j]

    vn = [_ln(z_ref[rows(t), col(7)], sgg_ref[...], sgbeta_ref[...]) for t in range(td)]
    for t in range(td):
        vn_o[t] = vn[t]
        mixed = sgbias_ref[t:t + 1, :]
        for s in range(t + 1):
            mixed = mixed + sgwts_ref[t, s:s + 1, :] * vn[s]
        p_ref[rows(t), col(3)] = z_ref[rows(t), col(6)] * mixed


def mixer_sample(z, pool_tm, sc_tm, cm_tm, lp, *, td, ns, t_prompt, start_pos):
    blk = t_prompt // (td * ns)
    full = lambda *shape: pl.BlockSpec(shape, lambda i: (0,) * len(shape))
    tm_shape = lambda rows: jax.ShapeDtypeStruct((rows, ns, D_BR), F32)
    return pl.pallas_call(
        functools.partial(_mixer_sample_kernel, td=td, ns=ns, start_pos=start_pos),
        grid=(1,),
        in_specs=[pl.BlockSpec((td * ns, 8 * D_BR), lambda i: (blk, 0)),
                  full(POOL_STATE, ns, D_BR), full(SC_WIDTH - 1, ns, D_BR),
                  full(CM_WIDTH - 1, ns, D_BR),
                  full(4, GROUP, GROUP), full(1, D_BR), full(SC_WIDTH, D_BR),
                  full(CM_WIDTH, D_BR), full(1, D_BR), full(1, D_BR), full(1, D_BR),
                  full(1, D_BR), full(1, D_BR), full(td, td, D_BR), full(CHUNK, D_BR)],
        out_specs=[full(td * ns, N_BRANCH * D_BR),
                   full(POOL_STATE, ns, D_BR), full(SC_WIDTH - 1, ns, D_BR),
                   full(CM_WIDTH - 1, ns, D_BR), full(td, ns, D_BR)],
        out_shape=[jax.ShapeDtypeStruct((td * ns, N_BRANCH * D_BR), F32), tm_shape(POOL_STATE),
                   tm_shape(SC_WIDTH - 1), tm_shape(CM_WIDTH - 1), tm_shape(td)],
        compiler_params=_params("arbitrary"),
        name="mixer_sample",
    )(z, pool_tm, sc_tm, cm_tm, lp["pool_w"], lp["pool_scale"], lp["sc_w"], lp["cm_w"],
      lp["cm_b"], lp["cm_ln_g"], lp["cm_ln_b"], lp["sg_ln_g"], lp["sg_ln_b"], lp["sg_wts"],
      lp["sg_bias"])


def _gate_merge_kernel(g0_ref, g1_ref, g2_ref, g3_ref, p_ref, pt_ref, w0_ref, w1_ref, w2_ref,
                       w3_ref, b_ref, o_ref, *, n_head):
    p = _rows_of(pl.program_id(0), n_head, p_ref, pt_ref)
    acc = None
    for i, (g_ref, w_ref) in enumerate(((g0_ref, w0_ref), (g1_ref, w1_ref), (g2_ref, w2_ref),
                                        (g3_ref, w3_ref))):
        y = _dot(p[:, i * D_BR:(i + 1) * D_BR].astype(BF16), w_ref[...])
        term = _sigmoid(g_ref[...] + b_ref[i:i + 1, :]) * y
        acc = term if acc is None else acc + term
    o_ref[...] = acc.astype(BF16)


def gate_merge(z, p_head, p_tail, w_outs, l, b_gate, *, d, tn):
    t = z.shape[0]
    tm = p_tail.shape[0]
    n_head = p_head.shape[0] // tm
    assert p_head.shape[0] == n_head * tm and t == (n_head + 1) * tm
    first = N_BRANCH * 2 * D_BR // tn
    gate_spec = lambda i: pl.BlockSpec((tm, tn), lambda r, c: (r, first + i * (d // tn) + c))
    w_spec = pl.BlockSpec((None, D_BR, tn), lambda r, c: (l, 0, c))
    return pl.pallas_call(
        functools.partial(_gate_merge_kernel, n_head=n_head),
        grid=(t // tm, d // tn),
        in_specs=[gate_spec(0), gate_spec(1), gate_spec(2), gate_spec(3),
                  pl.BlockSpec((tm, N_BRANCH * D_BR), lambda r, c: (jnp.minimum(r, n_head - 1), 0)),
                  pl.BlockSpec((tm, N_BRANCH * D_BR), lambda r, c: (0, 0)),
                  w_spec, w_spec, w_spec, w_spec,
                  pl.BlockSpec((N_BRANCH, tn), lambda r, c: (0, c))],
        out_specs=pl.BlockSpec((tm, tn), lambda r, c: (r, c)),
        out_shape=jax.ShapeDtypeStruct((t, d), BF16),
        compiler_params=_params("parallel", "parallel"),
        name="gate_merge",
    )(z, z, z, z, p_head, p_tail, *w_outs, b_gate)


def _softmax_rows(s):
    m = jnp.max(s, axis=-1, keepdims=True)
    p = jnp.exp(s - m)
    return p / jnp.sum(p, axis=-1, keepdims=True)


def _attn_prompt_kernel(q_ref, k_ref, v_ref, o_ref, *, scale):
    for h in range(X_HEADS):
        sl = slice(h * GROUP, (h + 1) * GROUP)
        s = lax.dot_general(q_ref[:, sl].astype(BF16), k_ref[0, :, sl].astype(BF16),
                            (((1,), (1,)), ((), ())), preferred_element_type=F32) * scale
        o_ref[:, sl] = _dot(_softmax_rows(s).astype(BF16), v_ref[0, :, sl].astype(BF16))


def attn_prompt(q, k, v, *, n_seq, seq, tq):
    dx = q.shape[1]
    mem = k.shape[1]
    nb = seq // tq
    return pl.pallas_call(
        functools.partial(_attn_prompt_kernel, scale=GROUP ** -0.5),
        grid=(n_seq, nb),
        in_specs=[pl.BlockSpec((tq, dx), lambda n, t: (n * nb + t, 0)),
                  pl.BlockSpec((1, mem, dx), lambda n, t: (n, 0, 0)),
                  pl.BlockSpec((1, mem, dx), lambda n, t: (n, 0, 0))],
        out_specs=pl.BlockSpec((tq, dx), lambda n, t: (n * nb + t, 0)),
        out_shape=jax.ShapeDtypeStruct((n_seq * seq, dx), F32),
        compiler_params=_params("parallel", "parallel"),
        name="attn_prompt",
    )(q, k, v)


def _attn_sample_kernel(q_ref, k_ref, v_ref, o_ref, *, scale):
    for h in range(X_HEADS):
        sl = slice(h * GROUP, (h + 1) * GROUP)
        s = jnp.einsum("bqd,bkd->bqk", q_ref[:, :, sl].astype(BF16), k_ref[:, :, sl].astype(BF16),
                       preferred_element_type=F32) * scale
        o_ref[:, :, sl] = jnp.einsum("bqk,bkd->bqd", _softmax_rows(s).astype(BF16),
                                     v_ref[:, :, sl].astype(BF16), preferred_element_type=F32)


def attn_sample(q, k, v, *, sb):
    n_seq, rows, dx = q.shape
    mem = k.shape[1]
    return pl.pallas_call(
        functools.partial(_attn_sample_kernel, scale=GROUP ** -0.5),
        grid=(n_seq // sb,),
        in_specs=[pl.BlockSpec((sb, rows, dx), lambda i: (i, 0, 0)),
                  pl.BlockSpec((sb, mem, dx), lambda i: (i, 0, 0)),
                  pl.BlockSpec((sb, mem, dx), lambda i: (i, 0, 0))],
        out_specs=pl.BlockSpec((sb, rows, dx), lambda i: (i, 0, 0)),
        out_shape=jax.ShapeDtypeStruct((n_seq, rows, dx), F32),
        compiler_params=_params("parallel"),
        name="attn_sample",
    )(q, k, v)


def _norm_t_kernel(x_ref, g_ref, o_ref):
    o_ref[...] = _rms(x_ref[...], g_ref[...]).T.astype(BF16)


def norm_t(x, g, *, tb):
    t, d = x.shape
    return pl.pallas_call(
        _norm_t_kernel,
        grid=(t // tb,),
        in_specs=[pl.BlockSpec((tb, d), lambda i: (i, 0)), pl.BlockSpec((1, d), lambda i: (0, 0))],
        out_specs=pl.BlockSpec((d, tb), lambda i: (0, i)),
        out_shape=jax.ShapeDtypeStruct((d, t), BF16),
        compiler_params=_params("parallel"),
        name="norm_t",
    )(x, g.reshape(1, d))


_CAND = [(i, j) for i in range(PEER_TOPK + 1) for j in range(PEER_TOPK + 1)
         if (i + 1) * (j + 1) <= PEER_TOPK + 1]


def _peer_pre_kernel(ht_ref, wq_ref, keys_ref, s2_o, th_o, lam_o, q_scr, s_scr, v_scr, *, tb):
    nk = GROUP
    q_scr[...] = _dot(wq_ref[...], ht_ref[...]).astype(BF16)
    for hp in range(2 * PEER_HEADS):
        s_scr[hp] = _dot(keys_ref[hp], q_scr[hp * nk:(hp + 1) * nk, :])

    idx = lax.broadcasted_iota(jnp.int32, (nk, LANES), 0)

    def chunk(ci, carry):
        lanes = pl.ds(pl.multiple_of(ci * LANES, LANES), LANES)
        for hp in range(2 * PEER_HEADS):
            h, p = divmod(hp, 2)
            cur = s_scr[hp, :, lanes]
            for r in range(PEER_TOPK + 1):
                m = jnp.max(cur, axis=0, keepdims=True)
                v_scr[p, r, h:h + 1, lanes] = m
                if r < PEER_TOPK:
                    first = jnp.min(jnp.where(cur == m, idx, nk), axis=0, keepdims=True)
                    cur = jnp.where(idx == first, NEG_INF, cur)
        v1 = [v_scr[0, r, :, lanes] for r in range(PEER_TOPK + 1)]
        v2 = [v_scr[1, r, :, lanes] for r in range(PEER_TOPK + 1)]
        cands = [v1[i] + v2[j] for i, j in _CAND]
        kth = jnp.full_like(cands[0], NEG_INF)
        nxt = jnp.full_like(cands[0], NEG_INF)
        for ck in cands:
            cnt = jnp.zeros_like(ck)
            for cl in cands:
                cnt = cnt + jnp.where(cl >= ck, 1.0, 0.0)
            kth = jnp.maximum(kth, jnp.where(cnt >= PEER_TOPK, ck, NEG_INF))
            nxt = jnp.maximum(nxt, jnp.where(cnt >= PEER_TOPK + 1, ck, NEG_INF))
        tau = 0.5 * (kth + nxt)
        top = cands[0]
        zsum = jnp.zeros_like(top)
        for ck in cands:
            zsum = zsum + jnp.where(ck >= tau, jnp.exp(ck - top), 0.0)
        shift = top + jnp.log(zsum)
        for h in range(PEER_HEADS):
            s1 = s_scr[2 * h, :, lanes]
            th_o[h, :, lanes] = (tau[h:h + 1, :] - s1) * LOG2E
            lam_o[h, :, lanes] = (s1 - shift[h:h + 1, :]) * LOG2E
            s2_o[h * nk:(h + 1) * nk, lanes] = s_scr[2 * h + 1, :, lanes] * LOG2E
        return carry

    lax.fori_loop(0, tb // LANES, chunk, 0)


def peer_pre(ht, wq_t, keys, l, *, tb):
    d, t = ht.shape
    nq = wq_t.shape[1]
    hk = PEER_HEADS * GROUP
    flat = pl.BlockSpec((hk, tb), lambda i: (0, i))
    cube = pl.BlockSpec((PEER_HEADS, GROUP, tb), lambda i: (0, 0, i))
    return pl.pallas_call(
        functools.partial(_peer_pre_kernel, tb=tb),
        grid=(t // tb,),
        in_specs=[pl.BlockSpec((d, tb), lambda i: (0, i)),
                  pl.BlockSpec((None, nq, d), lambda i: (l, 0, 0)),
                  pl.BlockSpec((None, 2 * PEER_HEADS, GROUP, GROUP), lambda i: (l, 0, 0, 0))],
        out_specs=[flat, cube, cube],
        out_shape=[jax.ShapeDtypeStruct((hk, t), F32),
                   jax.ShapeDtypeStruct((PEER_HEADS, GROUP, t), F32),
                   jax.ShapeDtypeStruct((PEER_HEADS, GROUP, t), F32)],
        scratch_shapes=[pltpu.VMEM((nq, tb), BF16),
                        pltpu.VMEM((2 * PEER_HEADS, GROUP, tb), F32),
                        pltpu.VMEM((2, PEER_TOPK + 1, PEER_HEADS, tb), F32)],
        compiler_params=_params("parallel"),
        name="peer_pre",
    )(ht, wq_t, keys)


def _peer_dense_kernel(ht_ref, htn_ref, ua_ref, ub_ref, vt_ref, s2_ref, th_ref, lam_ref, x_ref,
                       o_ref, acc_ref, at_ref, gw_ref, *, tb, te, ne):
    s = pl.program_id(0)
    e_cur = jnp.maximum(s - 1, 0) % ne
    half = te // 2
    n_a = half // GROUP

    @pl.when(s == 0)
    def _():
        at_ref[0] = jnp.zeros((half, tb), F32)

    @pl.when(e_cur == 0)
    def _():
        acc_ref[...] = jnp.zeros_like(acc_ref)

    def weighted_activations(k):
        for c in range(tb // LANES):
            lanes = slice(c * LANES, (c + 1) * LANES)
            for r in range(GROUP // ROWS):
                w = [jnp.zeros((ROWS, LANES), F32) for _ in range(n_a)]
                for h in range(PEER_HEADS):
                    s2 = s2_ref[h * GROUP + r * ROWS:h * GROUP + (r + 1) * ROWS, lanes]
                    for a in range(n_a):
                        ag = k * n_a + a
                        keep = s2 >= th_ref[h, ag:ag + 1, lanes]
                        w[a] = w[a] + jnp.where(keep, jnp.exp2(s2 + lam_ref[h, ag:ag + 1, lanes]), 0.0)
                for a in range(n_a):
                    rows = slice(a * GROUP + r * ROWS, a * GROUP + (r + 1) * ROWS)
                    pre = at_ref[k, rows, lanes]
                    act = 0.5 * pre * (1.0 + lax.erf(pre * math.sqrt(0.5)))
                    gw_ref[k, rows, lanes] = (w[a] * act).astype(BF16)

    def accumulate(k):
        acc_ref[...] += _dot(vt_ref[:, k * half:(k + 1) * half], gw_ref[k])

    weighted_activations(0)
    at_ref[1] = _dot(ub_ref[...], ht_ref[...])
    weighted_activations(1)
    accumulate(0)
    at_ref[0] = _dot(ua_ref[...], htn_ref[...])
    accumulate(1)

    @pl.when(jnp.logical_and(e_cur == ne - 1, s > 0))
    def _():
        o_ref[...] = x_ref[...] + acc_ref[...].T


def peer_dense(ht, u, vt, l, s2, theta, lam, x, *, tb, te):
    d, t = ht.shape
    ne = u.shape[1] // te
    n_tiles = (t // tb) * ne
    half = te // 2
    a_blk = te // GROUP
    once = pl.Buffered(1)
    cur = lambda s: jnp.clip(s - 1, 0, n_tiles - 1)
    nxt = lambda s: jnp.minimum(s, n_tiles - 1)
    return pl.pallas_call(
        functools.partial(_peer_dense_kernel, tb=tb, te=te, ne=ne),
        grid=(n_tiles + 1,),
        in_specs=[pl.BlockSpec((d, tb), lambda s: (0, cur(s) // ne), pipeline_mode=once),
                  pl.BlockSpec((d, tb), lambda s: (0, nxt(s) // ne), pipeline_mode=once),
                  pl.BlockSpec((None, half, d), lambda s: (l, 2 * (nxt(s) % ne), 0)),
                  pl.BlockSpec((None, half, d), lambda s: (l, 2 * (cur(s) % ne) + 1, 0)),
                  pl.BlockSpec((None, d, te), lambda s: (l, 0, cur(s) % ne)),
                  pl.BlockSpec((PEER_HEADS * GROUP, tb), lambda s: (0, cur(s) // ne),
                               pipeline_mode=once),
                  pl.BlockSpec((PEER_HEADS, a_blk, tb), lambda s: (0, cur(s) % ne, cur(s) // ne)),
                  pl.BlockSpec((PEER_HEADS, a_blk, tb), lambda s: (0, cur(s) % ne, cur(s) // ne)),
                  pl.BlockSpec((tb, d), lambda s: (cur(s) // ne, 0), pipeline_mode=once)],
        out_specs=pl.BlockSpec((tb, d), lambda s: (cur(s) // ne, 0)),
        out_shape=jax.ShapeDtypeStruct((t, d), F32),
        scratch_shapes=[pltpu.VMEM((d, tb), F32), pltpu.VMEM((2, half, tb), F32),
                        pltpu.VMEM((2, half, tb), BF16)],
        compiler_params=_params("arbitrary"),
        name="peer_dense",
    )(ht, ht, u, u, vt, s2, theta, lam, x)


def _final_norm_kernel(x_ref, g_ref, head_o, tail_o, *, n_head):
    i = pl.program_id(0)
    y = _rms(x_ref[...], g_ref[...])

    @pl.when(i < n_head)
    def _():
        head_o[...] = y

    @pl.when(i == n_head)
    def _():
        tail_o[...] = y


def final_norm(x, g, *, tm):
    t, d = x.shape
    n_head = t // tm - 1
    assert t == (n_head + 1) * tm
    return pl.pallas_call(
        functools.partial(_final_norm_kernel, n_head=n_head),
        grid=(n_head + 1,),
        in_specs=[pl.BlockSpec((tm, d), lambda i: (i, 0)), pl.BlockSpec((1, d), lambda i: (0, 0))],
        out_specs=[pl.BlockSpec((tm, d), lambda i: (jnp.minimum(i, n_head - 1), 0)),
                   pl.BlockSpec((tm, d), lambda i: (0, 0))],
        out_shape=[jax.ShapeDtypeStruct((n_head * tm, d), F32),
                   jax.ShapeDtypeStruct((tm, d), F32)],
        compiler_params=_params("arbitrary"),
        name="final_norm",
    )(x, g.reshape(1, d))


def _to_time_major(s):
    return jnp.swapaxes(s, 0, 1)


def kernel(x_prompt, x_sample, mem_prompt, cache_mem_k, cache_mem_v, state_pool, state_sconv, state_cconv, g_mix, w_in, b_gate, pool_w, pool_scale, w_pool_out, sc_w, w_sc_out, cm_w, cm_b, cm_ln_g, cm_ln_b, w_cm_out, sg_ln_g, sg_ln_b, sg_w, sg_b, w_sg_out, w_o, g_x, g_mem, w_xq, w_xk, w_xv, w_xo, g_peer, w_pq, peer_keys, peer_u, peer_v, g_final):
    nb, seq, d = x_prompt.shape
    ns, td, _ = x_sample.shape
    depth = w_in.shape[0]
    mem_len = mem_prompt.shape[1]
    past_len = 16384
    t_prompt = nb * seq
    t_all = t_prompt + ns * td
    dx = X_HEADS * GROUP

    tm_big = _tile(t_all, 1088, 16)
    t_s = ns * td
    assert t_prompt % t_s == 0
    tn_in = _tile(w_in.shape[2], 1024, 128)
    tn_d = _tile(d, 512, 128)
    ts = _tile(seq, 256, CHUNK)
    tq = _tile(seq, 512, 8)
    tb = _tile(t_all, 512, LANES)
    tb_pre = _tile(t_all, 256, LANES)
    te = _tile(peer_u.shape[1], 1024, 8 * GROUP)

    x = jnp.concatenate([x_prompt.reshape(t_prompt, d),
                         _to_time_major(x_sample).reshape(ns * td, d)], axis=0)
    mem2d = mem_prompt.reshape(nb * mem_len, d)
    col1 = lambda v: v.reshape(1, -1)

    bf = lambda w: w.astype(BF16)
    w_in_b, w_o_b, w_xq_b, w_xk_b, w_xv_b, w_xo_b = map(bf, (w_in, w_o, w_xq, w_xk, w_xv, w_xo))
    w_branch_b = tuple(map(bf, (w_pool_out, w_sc_out, w_cm_out, w_sg_out)))
    w_pq_t = bf(jnp.swapaxes(w_pq, 1, 2))
    keys_b = bf(peer_keys.reshape(depth, 2 * PEER_HEADS, GROUP, GROUP))
    peer_u_b = bf(peer_u)
    peer_vt = bf(jnp.swapaxes(peer_v, 1, 2))

    outs = {k: [] for k in ("mk", "mv", "pool_p", "sc_p", "cm_p", "pool_s", "sc_s", "cm_s", "cv_s")}
    for l in range(depth):
        lp = dict(
            pool_w=pool_w[l].astype(BF16), pool_scale=col1(pool_scale[l]), sc_w=sc_w[l],
            cm_w=cm_w[l], cm_b=col1(cm_b[l]), cm_ln_g=col1(cm_ln_g[l]), cm_ln_b=col1(cm_ln_b[l]),
            sg_ln_g=col1(sg_ln_g[l]), sg_ln_b=col1(sg_ln_b[l]), sg_w=sg_w[l],
            sg_bias=jnp.repeat(sg_b[l].T, GROUP, axis=1),
            sg_wts=jnp.repeat(jnp.transpose(sg_w[l][:, :td, :td], (1, 2, 0)), GROUP, axis=2))

        z = norm_mm(x, g_mix[l], w_in_b, l, tm=tm_big, tn=tn_in)
        p_p, pool_p, sc_p, cm_p = mixer_prompt(z, lp, n_seq=nb, seq=seq, ts=ts)
        p_s, pool_s, sc_s, cm_s, vn_s = mixer_sample(
            z, _to_time_major(state_pool[l]), _to_time_major(state_sconv[l]),
            _to_time_major(state_cconv[l]), lp, td=td, ns=ns, t_prompt=t_prompt, start_pos=past_len)
        merged = gate_merge(z, p_p, p_s, w_branch_b, l, b_gate[l], d=d, tn=tn_d)
        x = mm_res(merged, w_o_b, l, x, tm=tm_big, tn=tn_d)

        q = norm_mm(x, g_x[l], w_xq_b, l, tm=tm_big, tn=dx)
        tm_mem = _tile(nb * mem_len, 1024, 16)
        k_p = norm_mm(mem2d, g_mem[l], w_xk_b, l, tm=tm_mem, tn=dx)
        v_p = norm_mm(mem2d, g_mem[l], w_xv_b, l, tm=tm_mem, tn=dx)
        o_p = attn_prompt(q, k_p.reshape(nb, mem_len, dx), v_p.reshape(nb, mem_len, dx),
                          n_seq=nb, seq=seq, tq=tq)
        q_s = jnp.swapaxes(q[t_prompt:].reshape(td, ns, dx), 0, 1)
        q_s = jnp.pad(q_s, ((0, 0), (0, 8 - td), (0, 0)))
        o_s = attn_sample(q_s, cache_mem_k[l].reshape(ns, mem_len, dx),
                          cache_mem_v[l].reshape(ns, mem_len, dx), sb=_tile(ns, 8, 1))
        o_s = jnp.swapaxes(o_s[:, :td], 0, 1).reshape(t_s, dx)
        x = mm_res(o_p, w_xo_b, l, x, tm=t_s, tn=tn_d, a_tail=o_s)

        ht = norm_t(x, g_peer[l], tb=tb)
        s2, theta, lam = peer_pre(ht, w_pq_t, keys_b, l, tb=tb_pre)
        x = peer_dense(ht, peer_u_b, peer_vt, l, s2, theta, lam, x, tb=tb, te=te)

        outs["mk"].append(k_p.reshape(nb, mem_len, X_HEADS, GROUP))
        outs["mv"].append(v_p.reshape(nb, mem_len, X_HEADS, GROUP))
        outs["pool_p"].append(pool_p)
        outs["sc_p"].append(sc_p)
        outs["cm_p"].append(cm_p)
        outs["pool_s"].append(_to_time_major(pool_s))
        outs["sc_s"].append(_to_time_major(sc_s))
        outs["cm_s"].append(_to_time_major(cm_s))
        outs["cv_s"].append(_to_time_major(vn_s))

    y_p, y_s = final_norm(x, g_final, tm=t_s)
    y_prompt = y_p.reshape(nb, seq, d)
    y_sample = _to_time_major(y_s.reshape(td, ns, d))
    st = lambda k: jnp.stack(outs[k])
    return (y_prompt, y_sample, st("mk"), st("mv"), st("pool_p"), st("sc_p"), st("cm_p"),
            st("pool_s"), st("sc_s"), st("cm_s"), st("cv_s"))
```

```python
import functools
import math

import jax
import jax.numpy as jnp
from jax import lax
from jax.experimental import pallas as pl
from jax.experimental.pallas import tpu as pltpu

EPS = 1e-6
GROUP = 128
POOL_WINDOWS = (2, 4, 8, 16)
POOL_STATE = max(POOL_WINDOWS) - 1
SC_WIDTH = 3
CM_WIDTH = 31
D_BR = 4 * GROUP
N_BRANCH = 4
CHUNK = 128
X_HEADS = 4
PEER_HEADS = 8
PEER_TOPK = 16
HALO = 32
LANES = 128
ROWS = 32
VMEM_LIMIT = 56 * 2 ** 20

BF16 = jnp.bfloat16
F32 = jnp.float32
NEG_INF = float("-inf")
LOG2E = math.log2(math.e)


def _tile(n, pref, mult=8):
    best = None
    for t in range(mult, min(n, pref) + 1, mult):
        if n % t == 0:
            best = t
    assert best is not None, (n, pref, mult)
    return best


def _params(*sem):
    return pltpu.CompilerParams(dimension_semantics=sem, vmem_limit_bytes=VMEM_LIMIT)


def _rms(x, g):
    ms = jnp.mean(x * x, axis=-1, keepdims=True)
    return x * lax.rsqrt(ms + EPS) * g


def _ln(x, g, b):
    mu = jnp.mean(x, axis=-1, keepdims=True)
    xc = x - mu
    var = jnp.mean(xc * xc, axis=-1, keepdims=True)
    return xc * lax.rsqrt(var + EPS) * g + b


def _sigmoid(x):
    return 1.0 / (1.0 + jnp.exp(-x))


def _dot(a, b):
    return jnp.dot(a, b, preferred_element_type=F32)


def _norm_mm_kernel(x_ref, g_ref, w_ref, o_ref, hn_ref):
    @pl.when(pl.program_id(1) == 0)
    def _():
        hn_ref[...] = _rms(x_ref[...], g_ref[...]).astype(BF16)

    o_ref[...] = _dot(hn_ref[...], w_ref[...])


def norm_mm(x, g, w, l, *, tm, tn):
    t, d = x.shape
    n = w.shape[2]
    return pl.pallas_call(
        _norm_mm_kernel,
        grid=(t // tm, n // tn),
        in_specs=[pl.BlockSpec((tm, d), lambda i, j: (i, 0)),
                  pl.BlockSpec((1, d), lambda i, j: (0, 0)),
                  pl.BlockSpec((None, d, tn), lambda i, j: (l, 0, j))],
        out_specs=pl.BlockSpec((tm, tn), lambda i, j: (i, j)),
        out_shape=jax.ShapeDtypeStruct((t, n), F32),
        scratch_shapes=[pltpu.VMEM((tm, d), BF16)],
        compiler_params=_params("parallel", "arbitrary"),
        name="norm_mm",
    )(x, g.reshape(1, d), w)


def _rows_of(i, n_head, head_ref, tail_ref):
    if tail_ref is None:
        return head_ref[...]
    return jnp.where(i < n_head, head_ref[...], tail_ref[...])


def _mm_res_kernel(a_ref, w_ref, r_ref, o_ref):
    o_ref[...] = r_ref[...] + _dot(a_ref[...].astype(BF16), w_ref[...])


def _mm_res_tail_kernel(a_ref, at_ref, w_ref, r_ref, o_ref, *, n_head):
    a = _rows_of(pl.program_id(0), n_head, a_ref, at_ref)
    o_ref[...] = r_ref[...] + _dot(a.astype(BF16), w_ref[...])


def mm_res(a, w, l, res, *, tm, tn, a_tail=None):
    t = res.shape[0]
    k, n = w.shape[1:]
    row = pl.BlockSpec((tm, tn), lambda i, j: (i, j))
    wspec = pl.BlockSpec((None, k, tn), lambda i, j: (l, 0, j))
    if a_tail is None:
        body, srcs = _mm_res_kernel, (a,)
        aspecs = [pl.BlockSpec((tm, k), lambda i, j: (i, 0))]
    else:
        n_head = a.shape[0] // tm
        assert a.shape[0] == n_head * tm and a_tail.shape[0] == tm and t == (n_head + 1) * tm
        body, srcs = functools.partial(_mm_res_tail_kernel, n_head=n_head), (a, a_tail)
        aspecs = [pl.BlockSpec((tm, k), lambda i, j: (jnp.minimum(i, n_head - 1), 0)),
                  pl.BlockSpec((tm, k), lambda i, j: (0, 0))]
    return pl.pallas_call(
        body,
        grid=(t // tm, n // tn),
        in_specs=aspecs + [wspec, row],
        out_specs=row,
        out_shape=jax.ShapeDtypeStruct((t, n), F32),
        compiler_params=_params("parallel", "parallel"),
        name="mm_res",
    )(*srcs, w, res)


def _pool_branch(window_sums, a, cnts, poolw_ref, pscale):
    outs = []
    for gi in range(len(POOL_WINDOWS)):
        sl = slice(gi * GROUP, (gi + 1) * GROUP)
        pg = window_sums[gi] / cnts[gi] - a[:, sl]
        outs.append(_dot(pg.astype(BF16), poolw_ref[gi]))
    return jnp.concatenate(outs, axis=1) * pscale


def _mixer_prompt_kernel(z_ref, poolw_ref, pscale_ref, scw_ref, cmw_ref, cmb_ref, cmg_ref,
                         cmbeta_ref, sgg_ref, sgbeta_ref, sgw_ref, sgbias_ref,
                         p_ref, pool_o, sc_o, cm_o, ext_ref, *, ts):
    tb = pl.program_id(1)
    h0 = HALO

    @pl.when(tb == 0)
    def _():
        ext_ref[:, 0:h0, :] = jnp.zeros((3, h0, D_BR), F32)

    a = z_ref[:, 0:D_BR]
    bg = z_ref[:, D_BR:2 * D_BR]
    gated = z_ref[:, 2 * D_BR:3 * D_BR] * z_ref[:, 3 * D_BR:4 * D_BR]
    glu = z_ref[:, 4 * D_BR:5 * D_BR] * _sigmoid(z_ref[:, 5 * D_BR:6 * D_BR])
    ext_ref[0, h0:h0 + ts, :] = a
    ext_ref[1, h0:h0 + ts, :] = gated
    ext_ref[2, h0:h0 + ts, :] = glu

    pos = tb * ts + lax.broadcasted_iota(jnp.int32, (ts, 1), 0)
    sums, cnts = [], []
    for gi, w in enumerate(POOL_WINDOWS):
        sl = slice(gi * GROUP, (gi + 1) * GROUP)
        s = a[:, sl]
        for j in range(1, w):
            s = s + ext_ref[0, h0 - j:h0 - j + ts, sl]
        sums.append(s)
        cnts.append(jnp.minimum(w, pos + 1).astype(F32))
    p_ref[:, 0:D_BR] = _pool_branch(sums, a, cnts, poolw_ref, pscale_ref[...])

    conv = scw_ref[SC_WIDTH - 1:SC_WIDTH, :] * gated
    for k in range(SC_WIDTH - 1):
        off = h0 - (SC_WIDTH - 1) + k
        conv = conv + scw_ref[k:k + 1, :] * ext_ref[1, off:off + ts, :]
    p_ref[:, D_BR:2 * D_BR] = bg * conv

    c = cmw_ref[CM_WIDTH - 1:CM_WIDTH, :] * glu
    for k in range(CM_WIDTH - 1):
        off = h0 - (CM_WIDTH - 1) + k
        c = c + cmw_ref[k:k + 1, :] * ext_ref[2, off:off + ts, :]
    c = _ln(c + cmb_ref[...], cmg_ref[...], cmbeta_ref[...])
    p_ref[:, 2 * D_BR:3 * D_BR] = c * _sigmoid(c)

    vn = _ln(z_ref[:, 7 * D_BR:8 * D_BR], sgg_ref[...], sgbeta_ref[...])
    row = lax.broadcasted_iota(jnp.int32, (CHUNK, CHUNK), 0)
    col = lax.broadcasted_iota(jnp.int32, (CHUNK, CHUNK), 1)
    tril = row >= col
    for gi in range(D_BR // GROUP):
        sl = slice(gi * GROUP, (gi + 1) * GROUP)
        wg = jnp.where(tril, sgw_ref[gi], 0.0).astype(BF16)
        for ci in range(ts // CHUNK):
            rows = slice(ci * CHUNK, (ci + 1) * CHUNK)
            mixed = _dot(wg, vn[rows, sl].astype(BF16)) + sgbias_ref[:, sl]
            p_ref[rows, 3 * D_BR + gi * GROUP:3 * D_BR + (gi + 1) * GROUP] = (
                z_ref[rows, 6 * D_BR + gi * GROUP:6 * D_BR + (gi + 1) * GROUP] * mixed)

    end = h0 + ts
    pool_o[0] = ext_ref[0, end - POOL_STATE:end, :]
    sc_o[0] = ext_ref[1, end - (SC_WIDTH - 1):end, :]
    cm_o[0] = ext_ref[2, end - (CM_WIDTH - 1):end, :]
    ext_ref[:, 0:h0, :] = ext_ref[:, ts:ts + h0, :]


def mixer_prompt(z, lp, *, n_seq, seq, ts):
    nb = seq // ts
    full = lambda *shape: pl.BlockSpec(shape, lambda n, t: (0,) * len(shape))
    state = lambda rows: pl.BlockSpec((1, rows, D_BR), lambda n, t: (n, 0, 0))
    return pl.pallas_call(
        functools.partial(_mixer_prompt_kernel, ts=ts),
        grid=(n_seq, nb),
        in_specs=[pl.BlockSpec((ts, 8 * D_BR), lambda n, t: (n * nb + t, 0)),
                  full(4, GROUP, GROUP), full(1, D_BR), full(SC_WIDTH, D_BR),
                  full(CM_WIDTH, D_BR), full(1, D_BR), full(1, D_BR), full(1, D_BR),
                  full(1, D_BR), full(1, D_BR), full(4, CHUNK, CHUNK), full(CHUNK, D_BR)],
        out_specs=[pl.BlockSpec((ts, N_BRANCH * D_BR), lambda n, t: (n * nb + t, 0)),
                   state(POOL_STATE), state(SC_WIDTH - 1), state(CM_WIDTH - 1)],
        out_shape=[jax.ShapeDtypeStruct((n_seq * seq, N_BRANCH * D_BR), F32),
                   jax.ShapeDtypeStruct((n_seq, POOL_STATE, D_BR), F32),
                   jax.ShapeDtypeStruct((n_seq, SC_WIDTH - 1, D_BR), F32),
                   jax.ShapeDtypeStruct((n_seq, CM_WIDTH - 1, D_BR), F32)],
        scratch_shapes=[pltpu.VMEM((3, HALO + ts, D_BR), F32)],
        compiler_params=_params("parallel", "arbitrary"),
        name="mixer_prompt",
    )(z, lp["pool_w"], lp["pool_scale"], lp["sc_w"], lp["cm_w"], lp["cm_b"], lp["cm_ln_g"],
      lp["cm_ln_b"], lp["sg_ln_g"], lp["sg_ln_b"], lp["sg_w"], lp["sg_bias"])


def _mixer_sample_kernel(z_ref, pool_ref, sc_ref, cm_ref, poolw_ref, pscale_ref, scw_ref, cmw_ref,
                         cmb_ref, cmg_ref, cmbeta_ref, sgg_ref, sgbeta_ref, sgwts_ref, sgbias_ref,
                         p_ref, pool_o, sc_o, cm_o, vn_o, *, td, ns, start_pos):
    rows = lambda t: slice(t * ns, (t + 1) * ns)
    col = lambda k: slice(k * D_BR, (k + 1) * D_BR)

    ext_a = [pool_ref[j] for j in range(POOL_STATE)] + [z_ref[rows(t), col(0)] for t in range(td)]
    a_all = jnp.concatenate(ext_a[POOL_STATE:], axis=0)
    sums, cnts = [], []
    for gi, w in enumerate(POOL_WINDOWS):
        sl = slice(gi * GROUP, (gi + 1) * GROUP)
        per_t = []
        for t in range(td):
            s = ext_a[POOL_STATE + t][:, sl]
            for j in range(1, w):
                s = s + ext_a[POOL_STATE + t - j][:, sl]
            per_t.append(s)
        sums.append(jnp.concatenate(per_t, axis=0))
        cnt = jnp.concatenate(
            [jnp.full((ns, 1), float(min(w, start_pos + t + 1)), F32) for t in range(td)], axis=0)
        cnts.append(cnt)
    p_ref[:, col(0)] = _pool_branch(sums, a_all, cnts, poolw_ref, pscale_ref[...])
    for j in range(POOL_STATE):
        pool_o[j] = ext_a[td + j]

    ext_b = [sc_ref[j] for j in range(SC_WIDTH - 1)]
    ext_b += [z_ref[rows(t), col(2)] * z_ref[rows(t), col(3)] for t in range(td)]
    for t in range(td):
        conv = scw_ref[0:1, :] * ext_b[t]
        for k in range(1, SC_WIDTH):
            conv = conv + scw_ref[k:k + 1, :] * ext_b[t + k]
        p_ref[rows(t), col(1)] = z_ref[rows(t), col(1)] * conv
    for j in range(SC_WIDTH - 1):
        sc_o[j] = ext_b[td + j]

    ext_c = [cm_ref[j] for j in range(CM_WIDTH - 1)]
    ext_c += [z_ref[rows(t), col(4)] * _sigmoid(z_ref[rows(t), col(5)]) for t in range(td)]
    for t in range(td):
        c = cmw_ref[0:1, :] * ext_c[t]
        for k in range(1, CM_WIDTH):
            c = c + cmw_ref[k:k + 1, :] * ext_c[t + k]
        c = _ln(c + cmb_ref[...], cmg_ref[...], cmbeta_ref[...])
        p_ref[rows(t), col(2)] = c * _sigmoid(c)
    for j in range(CM_WIDTH - 1):
        cm_o[j] = ext_c[td + j]

    vn = [_ln(z_ref[rows(t), col(7)], sgg_ref[...], sgbeta_ref[...]) for t in range(td)]
    for t in range(td):
        vn_o[t] = vn[t]
        mixed = sgbias_ref[t:t + 1, :]
        for s in range(t + 1):
            mixed = mixed + sgwts_ref[t, s:s + 1, :] * vn[s]
        p_ref[rows(t), col(3)] = z_ref[rows(t), col(6)] * mixed


def mixer_sample(z, pool_tm, sc_tm, cm_tm, lp, *, td, ns, t_prompt, start_pos):
    blk = t_prompt // (td * ns)
    full = lambda *shape: pl.BlockSpec(shape, lambda i: (0,) * len(shape))
    tm_shape = lambda rows: jax.ShapeDtypeStruct((rows, ns, D_BR), F32)
    return pl.pallas_call(
        functools.partial(_mixer_sample_kernel, td=td, ns=ns, start_pos=start_pos),
        grid=(1,),
        in_specs=[pl.BlockSpec((td * ns, 8 * D_BR), lambda i: (blk, 0)),
                  full(POOL_STATE, ns, D_BR), full(SC_WIDTH - 1, ns, D_BR),
                  full(CM_WIDTH - 1, ns, D_BR),
                  full(4, GROUP, GROUP), full(1, D_BR), full(SC_WIDTH, D_BR),
                  full(CM_WIDTH, D_BR), full(1, D_BR), full(1, D_BR), full(1, D_BR),
                  full(1, D_BR), full(1, D_BR), full(td, td, D_BR), full(CHUNK, D_BR)],
        out_specs=[full(td * ns, N_BRANCH * D_BR),
                   full(POOL_STATE, ns, D_BR), full(SC_WIDTH - 1, ns, D_BR),
                   full(CM_WIDTH - 1, ns, D_BR), full(td, ns, D_BR)],
        out_shape=[jax.ShapeDtypeStruct((td * ns, N_BRANCH * D_BR), F32), tm_shape(POOL_STATE),
                   tm_shape(SC_WIDTH - 1), tm_shape(CM_WIDTH - 1), tm_shape(td)],
        compiler_params=_params("arbitrary"),
        name="mixer_sample",
    )(z, pool_tm, sc_tm, cm_tm, lp["pool_w"], lp["pool_scale"], lp["sc_w"], lp["cm_w"],
      lp["cm_b"], lp["cm_ln_g"], lp["cm_ln_b"], lp["sg_ln_g"], lp["sg_ln_b"], lp["sg_wts"],
      lp["sg_bias"])


def _gate_merge_kernel(g0_ref, g1_ref, g2_ref, g3_ref, p_ref, pt_ref, w0_ref, w1_ref, w2_ref,
                       w3_ref, b_ref, o_ref, *, n_head):
    p = _rows_of(pl.program_id(0), n_head, p_ref, pt_ref)
    acc = None
    for i, (g_ref, w_ref) in enumerate(((g0_ref, w0_ref), (g1_ref, w1_ref), (g2_ref, w2_ref),
                                        (g3_ref, w3_ref))):
        y = _dot(p[:, i * D_BR:(i + 1) * D_BR].astype(BF16), w_ref[...])
        term = _sigmoid(g_ref[...] + b_ref[i:i + 1, :]) * y
        acc = term if acc is None else acc + term
    o_ref[...] = acc.astype(BF16)


def gate_merge(z, p_head, p_tail, w_outs, l, b_gate, *, d, tn):
    t = z.shape[0]
    tm = p_tail.shape[0]
    n_head = p_head.shape[0] // tm
    assert p_head.shape[0] == n_head * tm and t == (n_head + 1) * tm
    first = N_BRANCH * 2 * D_BR // tn
    gate_spec = lambda i: pl.BlockSpec((tm, tn), lambda r, c: (r, first + i * (d // tn) + c))
    w_spec = pl.BlockSpec((None, D_BR, tn), lambda r, c: (l, 0, c))
    return pl.pallas_call(
        functools.partial(_gate_merge_kernel, n_head=n_head),
        grid=(t // tm, d // tn),
        in_specs=[gate_spec(0), gate_spec(1), gate_spec(2), gate_spec(3),
                  pl.BlockSpec((tm, N_BRANCH * D_BR), lambda r, c: (jnp.minimum(r, n_head - 1), 0)),
                  pl.BlockSpec((tm, N_BRANCH * D_BR), lambda r, c: (0, 0)),
                  w_spec, w_spec, w_spec, w_spec,
                  pl.BlockSpec((N_BRANCH, tn), lambda r, c: (0, c))],
        out_specs=pl.BlockSpec((tm, tn), lambda r, c: (r, c)),
        out_shape=jax.ShapeDtypeStruct((t, d), BF16),
        compiler_params=_params("parallel", "parallel"),
        name="gate_merge",
    )(z, z, z, z, p_head, p_tail, *w_outs, b_gate)


def _softmax_rows(s):
    m = jnp.max(s, axis=-1, keepdims=True)
    p = jnp.exp(s - m)
    return p / jnp.sum(p, axis=-1, keepdims=True)


def _attn_prompt_kernel(q_ref, k_ref, v_ref, o_ref, *, scale):
    for h in range(X_HEADS):
        sl = slice(h * GROUP, (h + 1) * GROUP)
        s = lax.dot_general(q_ref[:, sl].astype(BF16), k_ref[0, :, sl].astype(BF16),
                            (((1,), (1,)), ((), ())), preferred_element_type=F32) * scale
        o_ref[:, sl] = _dot(_softmax_rows(s).astype(BF16), v_ref[0, :, sl].astype(BF16))


def attn_prompt(q, k, v, *, n_seq, seq, tq):
    dx = q.shape[1]
    mem = k.shape[1]
    nb = seq // tq
    return pl.pallas_call(
        functools.partial(_attn_prompt_kernel, scale=GROUP ** -0.5),
        grid=(n_seq, nb),
        in_specs=[pl.BlockSpec((tq, dx), lambda n, t: (n * nb + t, 0)),
                  pl.BlockSpec((1, mem, dx), lambda n, t: (n, 0, 0)),
                  pl.BlockSpec((1, mem, dx), lambda n, t: (n, 0, 0))],
        out_specs=pl.BlockSpec((tq, dx), lambda n, t: (n * nb + t, 0)),
        out_shape=jax.ShapeDtypeStruct((n_seq * seq, dx), F32),
        compiler_params=_params("parallel", "parallel"),
        name="attn_prompt",
    )(q, k, v)


def _attn_sample_kernel(q_ref, k_ref, v_ref, o_ref, *, scale):
    rows, cols = q_ref.shape[1], k_ref.shape[1]
    row_head = lax.broadcasted_iota(jnp.int32, (rows, cols), 0) % X_HEADS
    col_head = lax.broadcasted_iota(jnp.int32, (rows, cols), 1) % X_HEADS
    s = jnp.einsum("bqd,bkd->bqk", q_ref[...].astype(BF16), k_ref[...].astype(BF16),
                   preferred_element_type=F32) * scale
    s = jnp.where((row_head == col_head)[None], s, NEG_INF)
    o_ref[...] = jnp.einsum("bqk,bkd->bqd", _softmax_rows(s).astype(BF16),
                            v_ref[...].astype(BF16), preferred_element_type=F32)


def attn_sample(q, k, v, l, *, sb):
    n_seq, rows, hd = q.shape
    kv_spec = pl.BlockSpec((None, sb, k.shape[2], hd), lambda i: (l, i, 0, 0))
    return pl.pallas_call(
        functools.partial(_attn_sample_kernel, scale=GROUP ** -0.5),
        grid=(n_seq // sb,),
        in_specs=[pl.BlockSpec((sb, rows, hd), lambda i: (i, 0, 0)), kv_spec, kv_spec],
        out_specs=pl.BlockSpec((sb, rows, hd), lambda i: (i, 0, 0)),
        out_shape=jax.ShapeDtypeStruct((n_seq, rows, hd), F32),
        compiler_params=_params("parallel"),
        name="attn_sample",
    )(q, k, v)


def _norm_t_kernel(x_ref, g_ref, o_ref):
    o_ref[...] = _rms(x_ref[...], g_ref[...]).T.astype(BF16)


def norm_t(x, g, *, tb):
    t, d = x.shape
    return pl.pallas_call(
        _norm_t_kernel,
        grid=(t // tb,),
        in_specs=[pl.BlockSpec((tb, d), lambda i: (i, 0)), pl.BlockSpec((1, d), lambda i: (0, 0))],
        out_specs=pl.BlockSpec((d, tb), lambda i: (0, i)),
        out_shape=jax.ShapeDtypeStruct((d, t), BF16),
        compiler_params=_params("parallel"),
        name="norm_t",
    )(x, g.reshape(1, d))


_CAND = [(i, j) for i in range(PEER_TOPK + 1) for j in range(PEER_TOPK + 1)
         if (i + 1) * (j + 1) <= PEER_TOPK + 1]


def _peer_pre_kernel(ht_ref, wq_ref, keys_ref, s2_o, th_o, lam_o, q_scr, s_scr, v_scr, *, tb):
    nk = GROUP
    q_scr[...] = _dot(wq_ref[...], ht_ref[...]).astype(BF16)
    for hp in range(2 * PEER_HEADS):
        s_scr[hp] = _dot(keys_ref[hp], q_scr[hp * nk:(hp + 1) * nk, :])

    idx = lax.broadcasted_iota(jnp.int32, (nk, LANES), 0)

    def chunk(ci, carry):
        lanes = pl.ds(pl.multiple_of(ci * LANES, LANES), LANES)
        tied = jnp.zeros((1, LANES), F32)
        for hp in range(2 * PEER_HEADS):
            h, p = divmod(hp, 2)
            cur = s_scr[hp, :, lanes]
            m = jnp.max(cur, axis=0, keepdims=True)
            v_scr[p, 0, h:h + 1, lanes] = m
            for r in range(1, PEER_TOPK + 1):
                m = jnp.max(jnp.where(cur < m, cur, NEG_INF), axis=0, keepdims=True)
                v_scr[p, r, h:h + 1, lanes] = m
            n_ge = jnp.sum(jnp.where(cur >= m, 1.0, 0.0), axis=0, keepdims=True)
            tied = jnp.maximum(tied, jnp.where(n_ge != PEER_TOPK + 1.0, 1.0, 0.0))

        @pl.when(jnp.max(tied) > 0.0)
        def _():
            for hp in range(2 * PEER_HEADS):
                h, p = divmod(hp, 2)
                cur = s_scr[hp, :, lanes]
                for r in range(PEER_TOPK + 1):
                    m = jnp.max(cur, axis=0, keepdims=True)
                    v_scr[p, r, h:h + 1, lanes] = m
                    if r < PEER_TOPK:
                        first = jnp.min(jnp.where(cur == m, idx, nk), axis=0, keepdims=True)
                        cur = jnp.where(idx == first, NEG_INF, cur)

        v1 = [v_scr[0, r, :, lanes] for r in range(PEER_TOPK + 1)]
        v2 = [v_scr[1, r, :, lanes] for r in range(PEER_TOPK + 1)]
        cands = [v1[i] + v2[j] for i, j in _CAND]
        kth = jnp.full_like(cands[0], NEG_INF)
        nxt = jnp.full_like(cands[0], NEG_INF)
        for ck in cands:
            cnt = jnp.zeros_like(ck)
            for cl in cands:
                cnt = cnt + jnp.where(cl >= ck, 1.0, 0.0)
            kth = jnp.maximum(kth, jnp.where(cnt >= PEER_TOPK, ck, NEG_INF))
            nxt = jnp.maximum(nxt, jnp.where(cnt >= PEER_TOPK + 1, ck, NEG_INF))
        tau = 0.5 * (kth + nxt)
        top = cands[0]
        zsum = jnp.zeros_like(top)
        for ck in cands:
            zsum = zsum + jnp.where(ck >= tau, jnp.exp(ck - top), 0.0)
        shift = top + jnp.log(zsum)
        for h in range(PEER_HEADS):
            s1 = s_scr[2 * h, :, lanes]
            th_o[h, :, lanes] = (tau[h:h + 1, :] - s1) * LOG2E
            lam_o[h, :, lanes] = (s1 - shift[h:h + 1, :]) * LOG2E
            s2_o[h * nk:(h + 1) * nk, lanes] = s_scr[2 * h + 1, :, lanes] * LOG2E
        return carry

    lax.fori_loop(0, tb // LANES, chunk, 0)


def peer_pre(ht, wq_t, keys, l, *, tb):
    d, t = ht.shape
    nq = wq_t.shape[1]
    hk = PEER_HEADS * GROUP
    flat = pl.BlockSpec((hk, tb), lambda i: (0, i))
    cube = pl.BlockSpec((PEER_HEADS, GROUP, tb), lambda i: (0, 0, i))
    return pl.pallas_call(
        functools.partial(_peer_pre_kernel, tb=tb),
        grid=(t // tb,),
        in_specs=[pl.BlockSpec((d, tb), lambda i: (0, i)),
                  pl.BlockSpec((None, nq, d), lambda i: (l, 0, 0)),
                  pl.BlockSpec((None, 2 * PEER_HEADS, GROUP, GROUP), lambda i: (l, 0, 0, 0))],
        out_specs=[flat, cube, cube],
        out_shape=[jax.ShapeDtypeStruct((hk, t), F32),
                   jax.ShapeDtypeStruct((PEER_HEADS, GROUP, t), F32),
                   jax.ShapeDtypeStruct((PEER_HEADS, GROUP, t), F32)],
        scratch_shapes=[pltpu.VMEM((nq, tb), BF16),
                        pltpu.VMEM((2 * PEER_HEADS, GROUP, tb), F32),
                        pltpu.VMEM((2, PEER_TOPK + 1, PEER_HEADS, tb), F32)],
        compiler_params=_params("parallel"),
        name="peer_pre",
    )(ht, wq_t, keys)


def _peer_dense_kernel(ht_ref, htn_ref, ua_ref, ub_ref, vt_ref, s2_ref, th_ref, lam_ref, x_ref,
                       o_ref, acc_ref, at_ref, gw_ref, *, tb, te, ne):
    s = pl.program_id(0)
    e_cur = jnp.maximum(s - 1, 0) % ne
    half = te // 2
    n_a = half // GROUP

    @pl.when(s == 0)
    def _():
        at_ref[0] = jnp.zeros((half, tb), F32)

    @pl.when(e_cur == 0)
    def _():
        acc_ref[...] = jnp.zeros_like(acc_ref)

    def weighted_activations(k):
        for c in range(tb // LANES):
            lanes = slice(c * LANES, (c + 1) * LANES)
            for r in range(GROUP // ROWS):
                w = [jnp.zeros((ROWS, LANES), F32) for _ in range(n_a)]
                for h in range(PEER_HEADS):
                    s2 = s2_ref[h * GROUP + r * ROWS:h * GROUP + (r + 1) * ROWS, lanes]
                    for a in range(n_a):
                        ag = k * n_a + a
                        keep = s2 >= th_ref[h, ag:ag + 1, lanes]
                        w[a] = w[a] + jnp.where(keep, jnp.exp2(s2 + lam_ref[h, ag:ag + 1, lanes]), 0.0)
                for a in range(n_a):
                    rows = slice(a * GROUP + r * ROWS, a * GROUP + (r + 1) * ROWS)
                    pre = at_ref[k, rows, lanes]
                    act = 0.5 * pre * (1.0 + lax.erf(pre * math.sqrt(0.5)))
                    gw_ref[k, rows, lanes] = (w[a] * act).astype(BF16)

    def accumulate(k):
        acc_ref[...] += _dot(vt_ref[:, k * half:(k + 1) * half], gw_ref[k])

    weighted_activations(0)
    at_ref[1] = _dot(ub_ref[...], ht_ref[...])
    weighted_activations(1)
    accumulate(0)
    at_ref[0] = _dot(ua_ref[...], htn_ref[...])
    accumulate(1)

    @pl.when(jnp.logical_and(e_cur == ne - 1, s > 0))
    def _():
        o_ref[...] = x_ref[...] + acc_ref[...].T


def peer_dense(ht, u, vt, l, s2, theta, lam, x, *, tb, te):
    d, t = ht.shape
    ne = u.shape[1] // te
    n_tiles = (t // tb) * ne
    half = te // 2
    a_blk = te // GROUP
    once = pl.Buffered(1)
    cur = lambda s: jnp.clip(s - 1, 0, n_tiles - 1)
    nxt = lambda s: jnp.minimum(s, n_tiles - 1)
    return pl.pallas_call(
        functools.partial(_peer_dense_kernel, tb=tb, te=te, ne=ne),
        grid=(n_tiles + 1,),
        in_specs=[pl.BlockSpec((d, tb), lambda s: (0, cur(s) // ne), pipeline_mode=once),
                  pl.BlockSpec((d, tb), lambda s: (0, nxt(s) // ne), pipeline_mode=once),
                  pl.BlockSpec((None, half, d), lambda s: (l, 2 * (nxt(s) % ne), 0)),
                  pl.BlockSpec((None, half, d), lambda s: (l, 2 * (cur(s) % ne) + 1, 0)),
                  pl.BlockSpec((None, d, te), lambda s: (l, 0, cur(s) % ne)),
                  pl.BlockSpec((PEER_HEADS * GROUP, tb), lambda s: (0, cur(s) // ne),
                               pipeline_mode=once),
                  pl.BlockSpec((PEER_HEADS, a_blk, tb), lambda s: (0, cur(s) % ne, cur(s) // ne)),
                  pl.BlockSpec((PEER_HEADS, a_blk, tb), lambda s: (0, cur(s) % ne, cur(s) // ne)),
                  pl.BlockSpec((tb, d), lambda s: (cur(s) // ne, 0), pipeline_mode=once)],
        out_specs=pl.BlockSpec((tb, d), lambda s: (cur(s) // ne, 0)),
        out_shape=jax.ShapeDtypeStruct((t, d), F32),
        scratch_shapes=[pltpu.VMEM((d, tb), F32), pltpu.VMEM((2, half, tb), F32),
                        pltpu.VMEM((2, half, tb), BF16)],
        compiler_params=_params("arbitrary"),
        name="peer_dense",
    )(ht, ht, u, u, vt, s2, theta, lam, x)


def _final_norm_kernel(x_ref, g_ref, head_o, tail_o, *, n_head):
    i = pl.program_id(0)
    y = _rms(x_ref[...], g_ref[...])

    @pl.when(i < n_head)
    def _():
        head_o[...] = y

    @pl.when(i == n_head)
    def _():
        tail_o[...] = y


def final_norm(x, g, *, tm):
    t, d = x.shape
    n_head = t // tm - 1
    assert t == (n_head + 1) * tm
    return pl.pallas_call(
        functools.partial(_final_norm_kernel, n_head=n_head),
        grid=(n_head + 1,),
        in_specs=[pl.BlockSpec((tm, d), lambda i: (i, 0)), pl.BlockSpec((1, d), lambda i: (0, 0))],
        out_specs=[pl.BlockSpec((tm, d), lambda i: (jnp.minimum(i, n_head - 1), 0)),
                   pl.BlockSpec((tm, d), lambda i: (0, 0))],
        out_shape=[jax.ShapeDtypeStruct((n_head * tm, d), F32),
                   jax.ShapeDtypeStruct((tm, d), F32)],
        compiler_params=_params("arbitrary"),
        name="final_norm",
    )(x, g.reshape(1, d))


def _to_time_major(s):
    return jnp.swapaxes(s, 0, 1)


def kernel(x_prompt, x_sample, mem_prompt, cache_mem_k, cache_mem_v, state_pool, state_sconv, state_cconv, g_mix, w_in, b_gate, pool_w, pool_scale, w_pool_out, sc_w, w_sc_out, cm_w, cm_b, cm_ln_g, cm_ln_b, w_cm_out, sg_ln_g, sg_ln_b, sg_w, sg_b, w_sg_out, w_o, g_x, g_mem, w_xq, w_xk, w_xv, w_xo, g_peer, w_pq, peer_keys, peer_u, peer_v, g_final):
    nb, seq, d = x_prompt.shape
    ns, td, _ = x_sample.shape
    depth = w_in.shape[0]
    mem_len = mem_prompt.shape[1]
    past_len = 16384
    t_prompt = nb * seq
    t_all = t_prompt + ns * td
    dx = X_HEADS * GROUP

    tm_big = _tile(t_all, 1088, 16)
    t_s = ns * td
    assert t_prompt % t_s == 0
    tn_in = _tile(w_in.shape[2], 1024, 128)
    tn_d = _tile(d, 512, 128)
    ts = _tile(seq, 256, CHUNK)
    tq = _tile(seq, 512, 8)
    tb = _tile(t_all, 512, LANES)
    tb_pre = _tile(t_all, 256, LANES)
    te = _tile(peer_u.shape[1], 1024, 8 * GROUP)

    x = jnp.concatenate([x_prompt.reshape(t_prompt, d),
                         _to_time_major(x_sample).reshape(ns * td, d)], axis=0)
    mem2d = mem_prompt.reshape(nb * mem_len, d)
    col1 = lambda v: v.reshape(1, -1)
    cache_k = cache_mem_k.reshape(depth, ns, mem_len * X_HEADS, GROUP)
    cache_v = cache_mem_v.reshape(depth, ns, mem_len * X_HEADS, GROUP)

    bf = lambda w: w.astype(BF16)
    w_in_b, w_o_b, w_xq_b, w_xk_b, w_xv_b, w_xo_b = map(bf, (w_in, w_o, w_xq, w_xk, w_xv, w_xo))
    w_branch_b = tuple(map(bf, (w_pool_out, w_sc_out, w_cm_out, w_sg_out)))
    w_pq_t = bf(jnp.swapaxes(w_pq, 1, 2))
    keys_b = bf(peer_keys.reshape(depth, 2 * PEER_HEADS, GROUP, GROUP))
    peer_u_b = bf(peer_u)
    peer_vt = bf(jnp.swapaxes(peer_v, 1, 2))

    outs = {k: [] for k in ("mk", "mv", "pool_p", "sc_p", "cm_p", "pool_s", "sc_s", "cm_s", "cv_s")}
    for l in range(depth):
        lp = dict(
            pool_w=pool_w[l].astype(BF16), pool_scale=col1(pool_scale[l]), sc_w=sc_w[l],
            cm_w=cm_w[l], cm_b=col1(cm_b[l]), cm_ln_g=col1(cm_ln_g[l]), cm_ln_b=col1(cm_ln_b[l]),
            sg_ln_g=col1(sg_ln_g[l]), sg_ln_b=col1(sg_ln_b[l]), sg_w=sg_w[l],
            sg_bias=jnp.repeat(sg_b[l].T, GROUP, axis=1),
            sg_wts=jnp.repeat(jnp.transpose(sg_w[l][:, :td, :td], (1, 2, 0)), GROUP, axis=2))

        z = norm_mm(x, g_mix[l], w_in_b, l, tm=tm_big, tn=tn_in)
        p_p, pool_p, sc_p, cm_p = mixer_prompt(z, lp, n_seq=nb, seq=seq, ts=ts)
        p_s, pool_s, sc_s, cm_s, vn_s = mixer_sample(
            z, _to_time_major(state_pool[l]), _to_time_major(state_sconv[l]),
            _to_time_major(state_cconv[l]), lp, td=td, ns=ns, t_prompt=t_prompt, start_pos=past_len)
        merged = gate_merge(z, p_p, p_s, w_branch_b, l, b_gate[l], d=d, tn=tn_d)
        x = mm_res(merged, w_o_b, l, x, tm=tm_big, tn=tn_d)

        q = norm_mm(x, g_x[l], w_xq_b, l, tm=tm_big, tn=dx)
        tm_mem = _tile(nb * mem_len, 1024, 16)
        k_p = norm_mm(mem2d, g_mem[l], w_xk_b, l, tm=tm_mem, tn=dx)
        v_p = norm_mm(mem2d, g_mem[l], w_xv_b, l, tm=tm_mem, tn=dx)
        o_p = attn_prompt(q, k_p.reshape(nb, mem_len, dx), v_p.reshape(nb, mem_len, dx),
                          n_seq=nb, seq=seq, tq=tq)
        q_s = jnp.swapaxes(q[t_prompt:].reshape(td, ns, dx), 0, 1).reshape(ns, td * X_HEADS, GROUP)
        o_s = attn_sample(q_s, cache_k, cache_v, l, sb=_tile(ns, 8, 1))
        o_s = jnp.swapaxes(o_s.reshape(ns, td, dx), 0, 1).reshape(t_s, dx)
        x = mm_res(o_p, w_xo_b, l, x, tm=t_s, tn=tn_d, a_tail=o_s)

        ht = norm_t(x, g_peer[l], tb=tb)
        s2, theta, lam = peer_pre(ht, w_pq_t, keys_b, l, tb=tb_pre)
        x = peer_dense(ht, peer_u_b, peer_vt, l, s2, theta, lam, x, tb=tb, te=te)

        outs["mk"].append(k_p.reshape(nb, mem_len, X_HEADS, GROUP))
        outs["mv"].append(v_p.reshape(nb, mem_len, X_HEADS, GROUP))
        outs["pool_p"].append(pool_p)
        outs["sc_p"].append(sc_p)
        outs["cm_p"].append(cm_p)
        outs["pool_s"].append(_to_time_major(pool_s))
        outs["sc_s"].append(_to_time_major(sc_s))
        outs["cm_s"].append(_to_time_major(cm_s))
        outs["cv_s"].append(_to_time_major(vn_s))

    y_p, y_s = final_norm(x, g_final, tm=t_s)
    y_prompt = y_p.reshape(nb, seq, d)
    y_sample = _to_time_major(y_s.reshape(td, ns, d))
    st = lambda k: jnp.stack(outs[k])
    return (y_prompt, y_sample, st("mk"), st("mv"), st("pool_p"), st("sc_p"), st("cm_p"),
            st("pool_s"), st("sc_s"), st("cm_s"), st("cv_s"))
```

```python
import functools
import math

import jax
import jax.numpy as jnp
from jax import lax
from jax.experimental import pallas as pl
from jax.experimental.pallas import tpu as pltpu

EPS = 1e-6
GROUP = 128
POOL_WINDOWS = (2, 4, 8, 16)
POOL_STATE = max(POOL_WINDOWS) - 1
SC_WIDTH = 3
CM_WIDTH = 31
D_BR = 4 * GROUP
N_BRANCH = 4
CHUNK = 128
X_HEADS = 4
PEER_HEADS = 8
PEER_TOPK = 16
HALO = 32
LANES = 128
ROWS = 32
VMEM_LIMIT = 56 * 2 ** 20

BF16 = jnp.bfloat16
F32 = jnp.float32
NEG_INF = float("-inf")
LOG2E = math.log2(math.e)


def _tile(n, pref, mult=8):
    best = None
    for t in range(mult, min(n, pref) + 1, mult):
        if n % t == 0:
            best = t
    assert best is not None, (n, pref, mult)
    return best


def _params(*sem):
    return pltpu.CompilerParams(dimension_semantics=sem, vmem_limit_bytes=VMEM_LIMIT)


def _rms(x, g):
    ms = jnp.mean(x * x, axis=-1, keepdims=True)
    return x * lax.rsqrt(ms + EPS) * g


def _ln(x, g, b):
    mu = jnp.mean(x, axis=-1, keepdims=True)
    xc = x - mu
    var = jnp.mean(xc * xc, axis=-1, keepdims=True)
    return xc * lax.rsqrt(var + EPS) * g + b


def _sigmoid(x):
    return 1.0 / (1.0 + jnp.exp(-x))


def _dot(a, b):
    return jnp.dot(a, b, preferred_element_type=F32)


def _norm_mm_kernel(x_ref, g_ref, w_ref, o_ref, hn_ref):
    @pl.when(pl.program_id(1) == 0)
    def _():
        hn_ref[...] = _rms(x_ref[...], g_ref[...]).astype(BF16)

    o_ref[...] = _dot(hn_ref[...], w_ref[...])


def norm_mm(x, g, w, l, *, tm, tn):
    t, d = x.shape
    n = w.shape[2]
    return pl.pallas_call(
        _norm_mm_kernel,
        grid=(t // tm, n // tn),
        in_specs=[pl.BlockSpec((tm, d), lambda i, j: (i, 0)),
                  pl.BlockSpec((1, d), lambda i, j: (0, 0)),
                  pl.BlockSpec((None, d, tn), lambda i, j: (l, 0, j))],
        out_specs=pl.BlockSpec((tm, tn), lambda i, j: (i, j)),
        out_shape=jax.ShapeDtypeStruct((t, n), F32),
        scratch_shapes=[pltpu.VMEM((tm, d), BF16)],
        compiler_params=_params("parallel", "arbitrary"),
        name="norm_mm",
    )(x, g.reshape(1, d), w)


def _rows_of(i, n_head, head_ref, tail_ref):
    if tail_ref is None:
        return head_ref[...]
    return jnp.where(i < n_head, head_ref[...], tail_ref[...])


def _mm_res_kernel(a_ref, w_ref, r_ref, o_ref):
    o_ref[...] = r_ref[...] + _dot(a_ref[...].astype(BF16), w_ref[...])


def _mm_res_tail_kernel(a_ref, at_ref, w_ref, r_ref, o_ref, *, n_head):
    a = _rows_of(pl.program_id(0), n_head, a_ref, at_ref)
    o_ref[...] = r_ref[...] + _dot(a.astype(BF16), w_ref[...])


def mm_res(a, w, l, res, *, tm, tn, a_tail=None):
    t = res.shape[0]
    k, n = w.shape[1:]
    row = pl.BlockSpec((tm, tn), lambda i, j: (i, j))
    wspec = pl.BlockSpec((None, k, tn), lambda i, j: (l, 0, j))
    if a_tail is None:
        body, srcs = _mm_res_kernel, (a,)
        aspecs = [pl.BlockSpec((tm, k), lambda i, j: (i, 0))]
    else:
        n_head = a.shape[0] // tm
        assert a.shape[0] == n_head * tm and a_tail.shape[0] == tm and t == (n_head + 1) * tm
        body, srcs = functools.partial(_mm_res_tail_kernel, n_head=n_head), (a, a_tail)
        aspecs = [pl.BlockSpec((tm, k), lambda i, j: (jnp.minimum(i, n_head - 1), 0)),
                  pl.BlockSpec((tm, k), lambda i, j: (0, 0))]
    return pl.pallas_call(
        body,
        grid=(t // tm, n // tn),
        in_specs=aspecs + [wspec, row],
        out_specs=row,
        out_shape=jax.ShapeDtypeStruct((t, n), F32),
        compiler_params=_params("parallel", "parallel"),
        name="mm_res",
    )(*srcs, w, res)


def _pool_branch(window_sums, a, cnts, poolw_ref, pscale):
    outs = []
    for gi in range(len(POOL_WINDOWS)):
        sl = slice(gi * GROUP, (gi + 1) * GROUP)
        pg = window_sums[gi] / cnts[gi] - a[:, sl]
        outs.append(_dot(pg.astype(BF16), poolw_ref[gi]))
    return jnp.concatenate(outs, axis=1) * pscale


def _mixer_prompt_kernel(z_ref, poolw_ref, pscale_ref, scw_ref, cmw_ref, cmb_ref, cmg_ref,
                         cmbeta_ref, sgg_ref, sgbeta_ref, sgw_ref, sgbias_ref,
                         p_ref, pool_o, sc_o, cm_o, ext_ref, *, ts):
    tb = pl.program_id(1)
    h0 = HALO

    @pl.when(tb == 0)
    def _():
        ext_ref[:, 0:h0, :] = jnp.zeros((3, h0, D_BR), F32)

    a = z_ref[:, 0:D_BR]
    bg = z_ref[:, D_BR:2 * D_BR]
    gated = z_ref[:, 2 * D_BR:3 * D_BR] * z_ref[:, 3 * D_BR:4 * D_BR]
    glu = z_ref[:, 4 * D_BR:5 * D_BR] * _sigmoid(z_ref[:, 5 * D_BR:6 * D_BR])
    ext_ref[0, h0:h0 + ts, :] = a
    ext_ref[1, h0:h0 + ts, :] = gated
    ext_ref[2, h0:h0 + ts, :] = glu

    pos = tb * ts + lax.broadcasted_iota(jnp.int32, (ts, 1), 0)
    sums, cnts = [], []
    for gi, w in enumerate(POOL_WINDOWS):
        sl = slice(gi * GROUP, (gi + 1) * GROUP)
        s = a[:, sl]
        for j in range(1, w):
            s = s + ext_ref[0, h0 - j:h0 - j + ts, sl]
        sums.append(s)
        cnts.append(jnp.minimum(w, pos + 1).astype(F32))
    p_ref[:, 0:D_BR] = _pool_branch(sums, a, cnts, poolw_ref, pscale_ref[...])

    conv = scw_ref[SC_WIDTH - 1:SC_WIDTH, :] * gated
    for k in range(SC_WIDTH - 1):
        off = h0 - (SC_WIDTH - 1) + k
        conv = conv + scw_ref[k:k + 1, :] * ext_ref[1, off:off + ts, :]
    p_ref[:, D_BR:2 * D_BR] = bg * conv

    c = cmw_ref[CM_WIDTH - 1:CM_WIDTH, :] * glu
    for k in range(CM_WIDTH - 1):
        off = h0 - (CM_WIDTH - 1) + k
        c = c + cmw_ref[k:k + 1, :] * ext_ref[2, off:off + ts, :]
    c = _ln(c + cmb_ref[...], cmg_ref[...], cmbeta_ref[...])
    p_ref[:, 2 * D_BR:3 * D_BR] = c * _sigmoid(c)

    vn = _ln(z_ref[:, 7 * D_BR:8 * D_BR], sgg_ref[...], sgbeta_ref[...])
    row = lax.broadcasted_iota(jnp.int32, (CHUNK, CHUNK), 0)
    col = lax.broadcasted_iota(jnp.int32, (CHUNK, CHUNK), 1)
    tril = row >= col
    for gi in range(D_BR // GROUP):
        sl = slice(gi * GROUP, (gi + 1) * GROUP)
        wg = jnp.where(tril, sgw_ref[gi], 0.0).astype(BF16)
        for ci in range(ts // CHUNK):
            rows = slice(ci * CHUNK, (ci + 1) * CHUNK)
            mixed = _dot(wg, vn[rows, sl].astype(BF16)) + sgbias_ref[:, sl]
            p_ref[rows, 3 * D_BR + gi * GROUP:3 * D_BR + (gi + 1) * GROUP] = (
                z_ref[rows, 6 * D_BR + gi * GROUP:6 * D_BR + (gi + 1) * GROUP] * mixed)

    end = h0 + ts
    pool_o[0] = ext_ref[0, end - POOL_STATE:end, :]
    sc_o[0] = ext_ref[1, end - (SC_WIDTH - 1):end, :]
    cm_o[0] = ext_ref[2, end - (CM_WIDTH - 1):end, :]
    ext_ref[:, 0:h0, :] = ext_ref[:, ts:ts + h0, :]


def mixer_prompt(z, lp, *, n_seq, seq, ts):
    nb = seq // ts
    full = lambda *shape: pl.BlockSpec(shape, lambda n, t: (0,) * len(shape))
    state = lambda rows: pl.BlockSpec((1, rows, D_BR), lambda n, t: (n, 0, 0))
    return pl.pallas_call(
        functools.partial(_mixer_prompt_kernel, ts=ts),
        grid=(n_seq, nb),
        in_specs=[pl.BlockSpec((ts, 8 * D_BR), lambda n, t: (n * nb + t, 0)),
                  full(4, GROUP, GROUP), full(1, D_BR), full(SC_WIDTH, D_BR),
                  full(CM_WIDTH, D_BR), full(1, D_BR), full(1, D_BR), full(1, D_BR),
                  full(1, D_BR), full(1, D_BR), full(4, CHUNK, CHUNK), full(CHUNK, D_BR)],
        out_specs=[pl.BlockSpec((ts, N_BRANCH * D_BR), lambda n, t: (n * nb + t, 0)),
                   state(POOL_STATE), state(SC_WIDTH - 1), state(CM_WIDTH - 1)],
        out_shape=[jax.ShapeDtypeStruct((n_seq * seq, N_BRANCH * D_BR), F32),
                   jax.ShapeDtypeStruct((n_seq, POOL_STATE, D_BR), F32),
                   jax.ShapeDtypeStruct((n_seq, SC_WIDTH - 1, D_BR), F32),
                   jax.ShapeDtypeStruct((n_seq, CM_WIDTH - 1, D_BR), F32)],
        scratch_shapes=[pltpu.VMEM((3, HALO + ts, D_BR), F32)],
        compiler_params=_params("parallel", "arbitrary"),
        name="mixer_prompt",
    )(z, lp["pool_w"], lp["pool_scale"], lp["sc_w"], lp["cm_w"], lp["cm_b"], lp["cm_ln_g"],
      lp["cm_ln_b"], lp["sg_ln_g"], lp["sg_ln_b"], lp["sg_w"], lp["sg_bias"])


def _mixer_sample_kernel(z_ref, pool_ref, sc_ref, cm_ref, poolw_ref, pscale_ref, scw_ref, cmw_ref,
                         cmb_ref, cmg_ref, cmbeta_ref, sgg_ref, sgbeta_ref, sgwts_ref, sgbias_ref,
                         p_ref, pool_o, sc_o, cm_o, vn_o, *, td, ns, start_pos):
    rows = lambda t: slice(t * ns, (t + 1) * ns)
    col = lambda k: slice(k * D_BR, (k + 1) * D_BR)

    ext_a = [pool_ref[j] for j in range(POOL_STATE)] + [z_ref[rows(t), col(0)] for t in range(td)]
    a_all = jnp.concatenate(ext_a[POOL_STATE:], axis=0)
    sums, cnts = [], []
    for gi, w in enumerate(POOL_WINDOWS):
        sl = slice(gi * GROUP, (gi + 1) * GROUP)
        per_t = []
        for t in range(td):
            s = ext_a[POOL_STATE + t][:, sl]
            for j in range(1, w):
                s = s + ext_a[POOL_STATE + t - j][:, sl]
            per_t.append(s)
        sums.append(jnp.concatenate(per_t, axis=0))
        cnt = jnp.concatenate(
            [jnp.full((ns, 1), float(min(w, start_pos + t + 1)), F32) for t in range(td)], axis=0)
        cnts.append(cnt)
    p_ref[:, col(0)] = _pool_branch(sums, a_all, cnts, poolw_ref, pscale_ref[...])
    for j in range(POOL_STATE):
        pool_o[j] = ext_a[td + j]

    ext_b = [sc_ref[j] for j in range(SC_WIDTH - 1)]
    ext_b += [z_ref[rows(t), col(2)] * z_ref[rows(t), col(3)] for t in range(td)]
    for t in range(td):
        conv = scw_ref[0:1, :] * ext_b[t]
        for k in range(1, SC_WIDTH):
            conv = conv + scw_ref[k:k + 1, :] * ext_b[t + k]
        p_ref[rows(t), col(1)] = z_ref[rows(t), col(1)] * conv
    for j in range(SC_WIDTH - 1):
        sc_o[j] = ext_b[td + j]

    ext_c = [cm_ref[j] for j in range(CM_WIDTH - 1)]
    ext_c += [z_ref[rows(t), col(4)] * _sigmoid(z_ref[rows(t), col(5)]) for t in range(td)]
    for t in range(td):
        c = cmw_ref[0:1, :] * ext_c[t]
        for k in range(1, CM_WIDTH):
            c = c + cmw_ref[k:k + 1, :] * ext_c[t + k]
        c = _ln(c + cmb_ref[...], cmg_ref[...], cmbeta_ref[...])
        p_ref[rows(t), col(2)] = c * _sigmoid(c)
    for j in range(CM_WIDTH - 1):
        cm_o[j] = ext_c[td + j]

    vn = [_ln(z_ref[rows(t), col(7)], sgg_ref[...], sgbeta_ref[...]) for t in range(td)]
    for t in range(td):
        vn_o[t] = vn[t]
        mixed = sgbias_ref[t:t + 1, :]
        for s in range(t + 1):
            mixed = mixed + sgwts_ref[t, s:s + 1, :] * vn[s]
        p_ref[rows(t), col(3)] = z_ref[rows(t), col(6)] * mixed


def mixer_sample(z, pool_tm, sc_tm, cm_tm, lp, *, td, ns, t_prompt, start_pos):
    blk = t_prompt // (td * ns)
    full = lambda *shape: pl.BlockSpec(shape, lambda i: (0,) * len(shape))
    tm_shape = lambda rows: jax.ShapeDtypeStruct((rows, ns, D_BR), F32)
    return pl.pallas_call(
        functools.partial(_mixer_sample_kernel, td=td, ns=ns, start_pos=start_pos),
        grid=(1,),
        in_specs=[pl.BlockSpec((td * ns, 8 * D_BR), lambda i: (blk, 0)),
                  full(POOL_STATE, ns, D_BR), full(SC_WIDTH - 1, ns, D_BR),
                  full(CM_WIDTH - 1, ns, D_BR),
                  full(4, GROUP, GROUP), full(1, D_BR), full(SC_WIDTH, D_BR),
                  full(CM_WIDTH, D_BR), full(1, D_BR), full(1, D_BR), full(1, D_BR),
                  full(1, D_BR), full(1, D_BR), full(td, td, D_BR), full(CHUNK, D_BR)],
        out_specs=[full(td * ns, N_BRANCH * D_BR),
                   full(POOL_STATE, ns, D_BR), full(SC_WIDTH - 1, ns, D_BR),
                   full(CM_WIDTH - 1, ns, D_BR), full(td, ns, D_BR)],
        out_shape=[jax.ShapeDtypeStruct((td * ns, N_BRANCH * D_BR), F32), tm_shape(POOL_STATE),
                   tm_shape(SC_WIDTH - 1), tm_shape(CM_WIDTH - 1), tm_shape(td)],
        compiler_params=_params("arbitrary"),
        name="mixer_sample",
    )(z, pool_tm, sc_tm, cm_tm, lp["pool_w"], lp["pool_scale"], lp["sc_w"], lp["cm_w"],
      lp["cm_b"], lp["cm_ln_g"], lp["cm_ln_b"], lp["sg_ln_g"], lp["sg_ln_b"], lp["sg_wts"],
      lp["sg_bias"])


def _gate_merge_kernel(g0_ref, g1_ref, g2_ref, g3_ref, p_ref, pt_ref, w0_ref, w1_ref, w2_ref,
                       w3_ref, b_ref, o_ref, *, n_head):
    p = _rows_of(pl.program_id(0), n_head, p_ref, pt_ref)
    acc = None
    for i, (g_ref, w_ref) in enumerate(((g0_ref, w0_ref), (g1_ref, w1_ref), (g2_ref, w2_ref),
                                        (g3_ref, w3_ref))):
        y = _dot(p[:, i * D_BR:(i + 1) * D_BR].astype(BF16), w_ref[...])
        term = _sigmoid(g_ref[...] + b_ref[i:i + 1, :]) * y
        acc = term if acc is None else acc + term
    o_ref[...] = acc.astype(BF16)


def gate_merge(z, p_head, p_tail, w_outs, l, b_gate, *, d, tn):
    t = z.shape[0]
    tm = p_tail.shape[0]
    n_head = p_head.shape[0] // tm
    assert p_head.shape[0] == n_head * tm and t == (n_head + 1) * tm
    first = N_BRANCH * 2 * D_BR // tn
    gate_spec = lambda i: pl.BlockSpec((tm, tn), lambda r, c: (r, first + i * (d // tn) + c))
    w_spec = pl.BlockSpec((None, D_BR, tn), lambda r, c: (l, 0, c))
    return pl.pallas_call(
        functools.partial(_gate_merge_kernel, n_head=n_head),
        grid=(t // tm, d // tn),
        in_specs=[gate_spec(0), gate_spec(1), gate_spec(2), gate_spec(3),
                  pl.BlockSpec((tm, N_BRANCH * D_BR), lambda r, c: (jnp.minimum(r, n_head - 1), 0)),
                  pl.BlockSpec((tm, N_BRANCH * D_BR), lambda r, c: (0, 0)),
                  w_spec, w_spec, w_spec, w_spec,
                  pl.BlockSpec((N_BRANCH, tn), lambda r, c: (0, c))],
        out_specs=pl.BlockSpec((tm, tn), lambda r, c: (r, c)),
        out_shape=jax.ShapeDtypeStruct((t, d), BF16),
        compiler_params=_params("parallel", "parallel"),
        name="gate_merge",
    )(z, z, z, z, p_head, p_tail, *w_outs, b_gate)


def _softmax_rows(s):
    m = jnp.max(s, axis=-1, keepdims=True)
    p = jnp.exp(s - m)
    return p / jnp.sum(p, axis=-1, keepdims=True)


def _attn_prompt_kernel(q_ref, k_ref, v_ref, o_ref, *, scale):
    for h in range(X_HEADS):
        sl = slice(h * GROUP, (h + 1) * GROUP)
        s = lax.dot_general(q_ref[:, sl].astype(BF16), k_ref[0, :, sl].astype(BF16),
                            (((1,), (1,)), ((), ())), preferred_element_type=F32) * scale
        o_ref[:, sl] = _dot(_softmax_rows(s).astype(BF16), v_ref[0, :, sl].astype(BF16))


def attn_prompt(q, k, v, *, n_seq, seq, tq):
    dx = q.shape[1]
    mem = k.shape[1]
    nb = seq // tq
    return pl.pallas_call(
        functools.partial(_attn_prompt_kernel, scale=GROUP ** -0.5),
        grid=(n_seq, nb),
        in_specs=[pl.BlockSpec((tq, dx), lambda n, t: (n * nb + t, 0)),
                  pl.BlockSpec((1, mem, dx), lambda n, t: (n, 0, 0)),
                  pl.BlockSpec((1, mem, dx), lambda n, t: (n, 0, 0))],
        out_specs=pl.BlockSpec((tq, dx), lambda n, t: (n * nb + t, 0)),
        out_shape=jax.ShapeDtypeStruct((n_seq * seq, dx), F32),
        compiler_params=_params("parallel", "parallel"),
        name="attn_prompt",
    )(q, k, v)


def _attn_sample_kernel(q_ref, k_ref, v_ref, o_ref, *, scale):
    rows, cols = q_ref.shape[1], k_ref.shape[1]
    row_head = lax.broadcasted_iota(jnp.int32, (rows, cols), 0) % X_HEADS
    col_head = lax.broadcasted_iota(jnp.int32, (rows, cols), 1) % X_HEADS
    s = jnp.einsum("bqd,bkd->bqk", q_ref[...].astype(BF16), k_ref[...].astype(BF16),
                   preferred_element_type=F32) * scale
    s = jnp.where((row_head == col_head)[None], s, NEG_INF)
    o_ref[...] = jnp.einsum("bqk,bkd->bqd", _softmax_rows(s).astype(BF16),
                            v_ref[...].astype(BF16), preferred_element_type=F32)


def attn_sample(q, k, v, l, *, sb):
    n_seq, rows, hd = q.shape
    kv_spec = pl.BlockSpec((None, sb, k.shape[2], hd), lambda i: (l, i, 0, 0))
    return pl.pallas_call(
        functools.partial(_attn_sample_kernel, scale=GROUP ** -0.5),
        grid=(n_seq // sb,),
        in_specs=[pl.BlockSpec((sb, rows, hd), lambda i: (i, 0, 0)), kv_spec, kv_spec],
        out_specs=pl.BlockSpec((sb, rows, hd), lambda i: (i, 0, 0)),
        out_shape=jax.ShapeDtypeStruct((n_seq, rows, hd), F32),
        compiler_params=_params("parallel"),
        name="attn_sample",
    )(q, k, v)


def _norm_t_kernel(x_ref, g_ref, o_ref):
    o_ref[...] = _rms(x_ref[...], g_ref[...]).T.astype(BF16)


def norm_t(x, g, *, tb):
    t, d = x.shape
    return pl.pallas_call(
        _norm_t_kernel,
        grid=(t // tb,),
        in_specs=[pl.BlockSpec((tb, d), lambda i: (i, 0)), pl.BlockSpec((1, d), lambda i: (0, 0))],
        out_specs=pl.BlockSpec((d, tb), lambda i: (0, i)),
        out_shape=jax.ShapeDtypeStruct((d, t), BF16),
        compiler_params=_params("parallel"),
        name="norm_t",
    )(x, g.reshape(1, d))


_CAND = [(i, j) for i in range(PEER_TOPK + 1) for j in range(PEER_TOPK + 1)
         if (i + 1) * (j + 1) <= PEER_TOPK + 1]


def _peer_pre_kernel(ht_ref, wq_ref, keys_ref, s2_o, th_o, lam_o, q_scr, s_scr, v_scr, *, tb):
    nk = GROUP
    q_scr[...] = _dot(wq_ref[...], ht_ref[...]).astype(BF16)
    for hp in range(2 * PEER_HEADS):
        s_scr[hp] = _dot(keys_ref[hp], q_scr[hp * nk:(hp + 1) * nk, :])

    idx = lax.broadcasted_iota(jnp.int32, (nk, LANES), 0)

    def chunk(ci, carry):
        lanes = pl.ds(pl.multiple_of(ci * LANES, LANES), LANES)
        tied = jnp.zeros((1, LANES), F32)
        for hp in range(2 * PEER_HEADS):
            h, p = divmod(hp, 2)
            cur = s_scr[hp, :, lanes]
            m = jnp.max(cur, axis=0, keepdims=True)
            v_scr[p, 0, h:h + 1, lanes] = m
            for r in range(1, PEER_TOPK + 1):
                m = jnp.max(jnp.where(cur < m, cur, NEG_INF), axis=0, keepdims=True)
                v_scr[p, r, h:h + 1, lanes] = m
            n_ge = jnp.sum(jnp.where(cur >= m, 1.0, 0.0), axis=0, keepdims=True)
            tied = jnp.maximum(tied, jnp.where(n_ge != PEER_TOPK + 1.0, 1.0, 0.0))

        @pl.when(jnp.max(tied) > 0.0)
        def _():
            for hp in range(2 * PEER_HEADS):
                h, p = divmod(hp, 2)
                cur = s_scr[hp, :, lanes]
                for r in range(PEER_TOPK + 1):
                    m = jnp.max(cur, axis=0, keepdims=True)
                    v_scr[p, r, h:h + 1, lanes] = m
                    if r < PEER_TOPK:
                        first = jnp.min(jnp.where(cur == m, idx, nk), axis=0, keepdims=True)
                        cur = jnp.where(idx == first, NEG_INF, cur)

        v1 = [v_scr[0, r, :, lanes] for r in range(PEER_TOPK + 1)]
        v2 = [v_scr[1, r, :, lanes] for r in range(PEER_TOPK + 1)]
        cands = [v1[i] + v2[j] for i, j in _CAND]
        kth = jnp.full_like(cands[0], NEG_INF)
        nxt = jnp.full_like(cands[0], NEG_INF)
        for ck in cands:
            cnt = jnp.zeros_like(ck)
            for cl in cands:
                cnt = cnt + jnp.where(cl >= ck, 1.0, 0.0)
            kth = jnp.maximum(kth, jnp.where(cnt >= PEER_TOPK, ck, NEG_INF))
            nxt = jnp.maximum(nxt, jnp.where(cnt >= PEER_TOPK + 1, ck, NEG_INF))
        tau = 0.5 * (kth + nxt)
        top = cands[0]
        zsum = jnp.zeros_like(top)
        for ck in cands:
            zsum = zsum + jnp.where(ck >= tau, jnp.exp(ck - top), 0.0)
        shift = top + jnp.log(zsum)
        for h in range(PEER_HEADS):
            s1 = s_scr[2 * h, :, lanes]
            th_o[h, :, lanes] = (tau[h:h + 1, :] - s1) * LOG2E
            lam_o[h, :, lanes] = (s1 - shift[h:h + 1, :]) * LOG2E
            s2_o[h * nk:(h + 1) * nk, lanes] = s_scr[2 * h + 1, :, lanes] * LOG2E
        return carry

    lax.fori_loop(0, tb // LANES, chunk, 0)


def peer_pre(ht, wq_t, keys, l, *, tb):
    d, t = ht.shape
    nq = wq_t.shape[1]
    hk = PEER_HEADS * GROUP
    flat = pl.BlockSpec((hk, tb), lambda i: (0, i))
    cube = pl.BlockSpec((PEER_HEADS, GROUP, tb), lambda i: (0, 0, i))
    return pl.pallas_call(
        functools.partial(_peer_pre_kernel, tb=tb),
        grid=(t // tb,),
        in_specs=[pl.BlockSpec((d, tb), lambda i: (0, i)),
                  pl.BlockSpec((None, nq, d), lambda i: (l, 0, 0)),
                  pl.BlockSpec((None, 2 * PEER_HEADS, GROUP, GROUP), lambda i: (l, 0, 0, 0))],
        out_specs=[flat, cube, cube],
        out_shape=[jax.ShapeDtypeStruct((hk, t), F32),
                   jax.ShapeDtypeStruct((PEER_HEADS, GROUP, t), F32),
                   jax.ShapeDtypeStruct((PEER_HEADS, GROUP, t), F32)],
        scratch_shapes=[pltpu.VMEM((nq, tb), BF16),
                        pltpu.VMEM((2 * PEER_HEADS, GROUP, tb), F32),
                        pltpu.VMEM((2, PEER_TOPK + 1, PEER_HEADS, tb), F32)],
        compiler_params=_params("parallel"),
        name="peer_pre",
    )(ht, wq_t, keys)


def _peer_dense_kernel(ht_ref, htn_ref, ua_ref, ub_ref, vt_ref, s2_ref, th_ref, lam_ref, x_ref,
                       o_ref, acc_ref, at_ref, gw_ref, *, tb, te, ne):
    s = pl.program_id(0)
    e_cur = jnp.maximum(s - 1, 0) % ne
    half = te // 2
    tbh = tb // 2
    n_a = half // GROUP
    piece = lambda k, c: 2 * k + c
    cols = lambda c: slice(c * tbh, (c + 1) * tbh)

    @pl.when(s == 0)
    def _():
        at_ref[0:2] = jnp.zeros((2, half, tbh), F32)

    @pl.when(e_cur == 0)
    def _():
        acc_ref[...] = jnp.zeros_like(acc_ref)

    def pre_activations(k, c, u_ref, h_ref):
        at_ref[piece(k, c)] = _dot(u_ref[...], h_ref[:, cols(c)])

    def weighted_activations(k, c):
        p = piece(k, c)
        for cl in range(tbh // LANES):
            lanes = slice(cl * LANES, (cl + 1) * LANES)
            glanes = slice(c * tbh + cl * LANES, c * tbh + (cl + 1) * LANES)
            for r in range(GROUP // ROWS):
                w = [jnp.zeros((ROWS, LANES), F32) for _ in range(n_a)]
                for h in range(PEER_HEADS):
                    s2 = s2_ref[h * GROUP + r * ROWS:h * GROUP + (r + 1) * ROWS, glanes]
                    for a in range(n_a):
                        ag = k * n_a + a
                        keep = s2 >= th_ref[h, ag:ag + 1, glanes]
                        w[a] = w[a] + jnp.where(keep, jnp.exp2(s2 + lam_ref[h, ag:ag + 1, glanes]), 0.0)
                for a in range(n_a):
                    rows = slice(a * GROUP + r * ROWS, a * GROUP + (r + 1) * ROWS)
                    pre = at_ref[p, rows, lanes]
                    act = 0.5 * pre * (1.0 + lax.erf(pre * math.sqrt(0.5)))
                    gw_ref[p, rows, lanes] = (w[a] * act).astype(BF16)

    def accumulate(k, c):
        acc_ref[:, cols(c)] += _dot(vt_ref[:, k * half:(k + 1) * half], gw_ref[piece(k, c)])

    pre_activations(1, 0, ub_ref, ht_ref)
    pre_activations(1, 1, ub_ref, ht_ref)
    for c in range(2):
        weighted_activations(0, c)
        accumulate(0, c)
        pre_activations(0, c, ua_ref, htn_ref)
    for c in range(2):
        weighted_activations(1, c)
        accumulate(1, c)

    @pl.when(jnp.logical_and(e_cur == ne - 1, s > 0))
    def _():
        o_ref[...] = x_ref[...] + acc_ref[...].T


def peer_dense(ht, u, vt, l, s2, theta, lam, x, *, tb, te):
    d, t = ht.shape
    ne = u.shape[1] // te
    n_tiles = (t // tb) * ne
    half = te // 2
    a_blk = te // GROUP
    once = pl.Buffered(1)
    cur = lambda s: jnp.clip(s - 1, 0, n_tiles - 1)
    nxt = lambda s: jnp.minimum(s, n_tiles - 1)
    return pl.pallas_call(
        functools.partial(_peer_dense_kernel, tb=tb, te=te, ne=ne),
        grid=(n_tiles + 1,),
        in_specs=[pl.BlockSpec((d, tb), lambda s: (0, cur(s) // ne), pipeline_mode=once),
                  pl.BlockSpec((d, tb), lambda s: (0, nxt(s) // ne), pipeline_mode=once),
                  pl.BlockSpec((None, half, d), lambda s: (l, 2 * (nxt(s) % ne), 0)),
                  pl.BlockSpec((None, half, d), lambda s: (l, 2 * (cur(s) % ne) + 1, 0)),
                  pl.BlockSpec((None, d, te), lambda s: (l, 0, cur(s) % ne)),
                  pl.BlockSpec((PEER_HEADS * GROUP, tb), lambda s: (0, cur(s) // ne),
                               pipeline_mode=once),
                  pl.BlockSpec((PEER_HEADS, a_blk, tb), lambda s: (0, cur(s) % ne, cur(s) // ne)),
                  pl.BlockSpec((PEER_HEADS, a_blk, tb), lambda s: (0, cur(s) % ne, cur(s) // ne)),
                  pl.BlockSpec((tb, d), lambda s: (cur(s) // ne, 0), pipeline_mode=once)],
        out_specs=pl.BlockSpec((tb, d), lambda s: (cur(s) // ne, 0)),
        out_shape=jax.ShapeDtypeStruct((t, d), F32),
        scratch_shapes=[pltpu.VMEM((d, tb), F32), pltpu.VMEM((4, half, tb // 2), F32),
                        pltpu.VMEM((4, half, tb // 2), BF16)],
        compiler_params=_params("arbitrary"),
        name="peer_dense",
    )(ht, ht, u, u, vt, s2, theta, lam, x)


def _final_norm_kernel(x_ref, g_ref, head_o, tail_o, *, n_head):
    i = pl.program_id(0)
    y = _rms(x_ref[...], g_ref[...])

    @pl.when(i < n_head)
    def _():
        head_o[...] = y

    @pl.when(i == n_head)
    def _():
        tail_o[...] = y


def final_norm(x, g, *, tm):
    t, d = x.shape
    n_head = t // tm - 1
    assert t == (n_head + 1) * tm
    return pl.pallas_call(
        functools.partial(_final_norm_kernel, n_head=n_head),
        grid=(n_head + 1,),
        in_specs=[pl.BlockSpec((tm, d), lambda i: (i, 0)), pl.BlockSpec((1, d), lambda i: (0, 0))],
        out_specs=[pl.BlockSpec((tm, d), lambda i: (jnp.minimum(i, n_head - 1), 0)),
                   pl.BlockSpec((tm, d), lambda i: (0, 0))],
        out_shape=[jax.ShapeDtypeStruct((n_head * tm, d), F32),
                   jax.ShapeDtypeStruct((tm, d), F32)],
        compiler_params=_params("arbitrary"),
        name="final_norm",
    )(x, g.reshape(1, d))


def _to_time_major(s):
    return jnp.swapaxes(s, 0, 1)


def kernel(x_prompt, x_sample, mem_prompt, cache_mem_k, cache_mem_v, state_pool, state_sconv, state_cconv, g_mix, w_in, b_gate, pool_w, pool_scale, w_pool_out, sc_w, w_sc_out, cm_w, cm_b, cm_ln_g, cm_ln_b, w_cm_out, sg_ln_g, sg_ln_b, sg_w, sg_b, w_sg_out, w_o, g_x, g_mem, w_xq, w_xk, w_xv, w_xo, g_peer, w_pq, peer_keys, peer_u, peer_v, g_final):
    nb, seq, d = x_prompt.shape
    ns, td, _ = x_sample.shape
    depth = w_in.shape[0]
    mem_len = mem_prompt.shape[1]
    past_len = 16384
    t_prompt = nb * seq
    t_all = t_prompt + ns * td
    dx = X_HEADS * GROUP

    tm_big = _tile(t_all, 1088, 16)
    t_s = ns * td
    assert t_prompt % t_s == 0
    tn_in = _tile(w_in.shape[2], 1024, 128)
    tn_d = _tile(d, 512, 128)
    ts = _tile(seq, 256, CHUNK)
    tq = _tile(seq, 512, 8)
    tb = _tile(t_all, 512, LANES)
    tb_pre = _tile(t_all, 256, LANES)
    te = _tile(peer_u.shape[1], 1024, 8 * GROUP)

    x = jnp.concatenate([x_prompt.reshape(t_prompt, d),
                         _to_time_major(x_sample).reshape(ns * td, d)], axis=0)
    mem2d = mem_prompt.reshape(nb * mem_len, d)
    col1 = lambda v: v.reshape(1, -1)
    cache_k = cache_mem_k.reshape(depth, ns, mem_len * X_HEADS, GROUP)
    cache_v = cache_mem_v.reshape(depth, ns, mem_len * X_HEADS, GROUP)

    bf = lambda w: w.astype(BF16)
    w_in_b, w_o_b, w_xq_b, w_xk_b, w_xv_b, w_xo_b = map(bf, (w_in, w_o, w_xq, w_xk, w_xv, w_xo))
    w_branch_b = tuple(map(bf, (w_pool_out, w_sc_out, w_cm_out, w_sg_out)))
    w_pq_t = bf(jnp.swapaxes(w_pq, 1, 2))
    keys_b = bf(peer_keys.reshape(depth, 2 * PEER_HEADS, GROUP, GROUP))
    peer_u_b = bf(peer_u)
    peer_vt = bf(jnp.swapaxes(peer_v, 1, 2))

    outs = {k: [] for k in ("mk", "mv", "pool_p", "sc_p", "cm_p", "pool_s", "sc_s", "cm_s", "cv_s")}
    for l in range(depth):
        lp = dict(
            pool_w=pool_w[l].astype(BF16), pool_scale=col1(pool_scale[l]), sc_w=sc_w[l],
            cm_w=cm_w[l], cm_b=col1(cm_b[l]), cm_ln_g=col1(cm_ln_g[l]), cm_ln_b=col1(cm_ln_b[l]),
            sg_ln_g=col1(sg_ln_g[l]), sg_ln_b=col1(sg_ln_b[l]), sg_w=sg_w[l],
            sg_bias=jnp.repeat(sg_b[l].T, GROUP, axis=1),
            sg_wts=jnp.repeat(jnp.transpose(sg_w[l][:, :td, :td], (1, 2, 0)), GROUP, axis=2))

        z = norm_mm(x, g_mix[l], w_in_b, l, tm=tm_big, tn=tn_in)
        p_p, pool_p, sc_p, cm_p = mixer_prompt(z, lp, n_seq=nb, seq=seq, ts=ts)
        p_s, pool_s, sc_s, cm_s, vn_s = mixer_sample(
            z, _to_time_major(state_pool[l]), _to_time_major(state_sconv[l]),
            _to_time_major(state_cconv[l]), lp, td=td, ns=ns, t_prompt=t_prompt, start_pos=past_len)
        merged = gate_merge(z, p_p, p_s, w_branch_b, l, b_gate[l], d=d, tn=tn_d)
        x = mm_res(merged, w_o_b, l, x, tm=tm_big, tn=tn_d)

        q = norm_mm(x, g_x[l], w_xq_b, l, tm=tm_big, tn=dx)
        tm_mem = _tile(nb * mem_len, 1024, 16)
        k_p = norm_mm(mem2d, g_mem[l], w_xk_b, l, tm=tm_mem, tn=dx)
        v_p = norm_mm(mem2d, g_mem[l], w_xv_b, l, tm=tm_mem, tn=dx)
        o_p = attn_prompt(q, k_p.reshape(nb, mem_len, dx), v_p.reshape(nb, mem_len, dx),
                          n_seq=nb, seq=seq, tq=tq)
        q_s = jnp.swapaxes(q[t_prompt:].reshape(td, ns, dx), 0, 1).reshape(ns, td * X_HEADS, GROUP)
        o_s = attn_sample(q_s, cache_k, cache_v, l, sb=_tile(ns, 8, 1))
        o_s = jnp.swapaxes(o_s.reshape(ns, td, dx), 0, 1).reshape(t_s, dx)
        x = mm_res(o_p, w_xo_b, l, x, tm=t_s, tn=tn_d, a_tail=o_s)

        ht = norm_t(x, g_peer[l], tb=tb)
        s2, theta, lam = peer_pre(ht, w_pq_t, keys_b, l, tb=tb_pre)
        x = peer_dense(ht, peer_u_b, peer_vt, l, s2, theta, lam, x, tb=tb, te=te)

        outs["mk"].append(k_p.reshape(nb, mem_len, X_HEADS, GROUP))
        outs["mv"].append(v_p.reshape(nb, mem_len, X_HEADS, GROUP))
        outs["pool_p"].append(pool_p)
        outs["sc_p"].append(sc_p)
        outs["cm_p"].append(cm_p)
        outs["pool_s"].append(_to_time_major(pool_s))
        outs["sc_s"].append(_to_time_major(sc_s))
        outs["cm_s"].append(_to_time_major(cm_s))
        outs["cv_s"].append(_to_time_major(vn_s))

    y_p, y_s = final_norm(x, g_final, tm=t_s)
    y_prompt = y_p.reshape(nb, seq, d)
    y_sample = _to_time_major(y_s.reshape(td, ns, d))
    st = lambda k: jnp.stack(outs[k])
    return (y_prompt, y_sample, st("mk"), st("mv"), st("pool_p"), st("sc_p"), st("cm_p"),
            st("pool_s"), st("sc_s"), st("cm_s"), st("cv_s"))
```

```python
import functools
import math

import jax
import jax.numpy as jnp
from jax import lax
from jax.experimental import pallas as pl
from jax.experimental.pallas import tpu as pltpu

EPS = 1e-6
GROUP = 128
POOL_WINDOWS = (2, 4, 8, 16)
POOL_STATE = max(POOL_WINDOWS) - 1
SC_WIDTH = 3
CM_WIDTH = 31
D_BR = 4 * GROUP
N_BRANCH = 4
CHUNK = 128
X_HEADS = 4
PEER_HEADS = 8
PEER_TOPK = 16
HALO = 32
LANES = 128
ROWS = 32
VMEM_LIMIT = 56 * 2 ** 20

BF16 = jnp.bfloat16
F32 = jnp.float32
NEG_INF = float("-inf")
LOG2E = math.log2(math.e)


def _tile(n, pref, mult=8):
    best = None
    for t in range(mult, min(n, pref) + 1, mult):
        if n % t == 0:
            best = t
    assert best is not None, (n, pref, mult)
    return best


def _params(*sem):
    return pltpu.CompilerParams(dimension_semantics=sem, vmem_limit_bytes=VMEM_LIMIT)


def _rms(x, g):
    ms = jnp.mean(x * x, axis=-1, keepdims=True)
    return x * lax.rsqrt(ms + EPS) * g


def _ln(x, g, b):
    mu = jnp.mean(x, axis=-1, keepdims=True)
    xc = x - mu
    var = jnp.mean(xc * xc, axis=-1, keepdims=True)
    return xc * lax.rsqrt(var + EPS) * g + b


def _sigmoid(x):
    return 1.0 / (1.0 + jnp.exp(-x))


def _dot(a, b):
    return jnp.dot(a, b, preferred_element_type=F32)


def _norm_mm_kernel(x_ref, g_ref, w_ref, o_ref, hn_ref):
    @pl.when(pl.program_id(1) == 0)
    def _():
        hn_ref[...] = _rms(x_ref[...], g_ref[...]).astype(BF16)

    o_ref[...] = _dot(hn_ref[...], w_ref[...])


def norm_mm(x, g, w, l, *, tm, tn):
    t, d = x.shape
    n = w.shape[2]
    return pl.pallas_call(
        _norm_mm_kernel,
        grid=(t // tm, n // tn),
        in_specs=[pl.BlockSpec((tm, d), lambda i, j: (i, 0)),
                  pl.BlockSpec((1, d), lambda i, j: (0, 0)),
                  pl.BlockSpec((None, d, tn), lambda i, j: (l, 0, j))],
        out_specs=pl.BlockSpec((tm, tn), lambda i, j: (i, j)),
        out_shape=jax.ShapeDtypeStruct((t, n), F32),
        scratch_shapes=[pltpu.VMEM((tm, d), BF16)],
        compiler_params=_params("parallel", "arbitrary"),
        name="norm_mm",
    )(x, g.reshape(1, d), w)


def _rows_of(i, n_head, head_ref, tail_ref):
    if tail_ref is None:
        return head_ref[...]
    return jnp.where(i < n_head, head_ref[...], tail_ref[...])


def _mm_res_kernel(a_ref, w_ref, r_ref, o_ref):
    o_ref[...] = r_ref[...] + _dot(a_ref[...].astype(BF16), w_ref[...])


def _mm_res_tail_kernel(a_ref, at_ref, w_ref, r_ref, o_ref, *, n_head):
    a = _rows_of(pl.program_id(0), n_head, a_ref, at_ref)
    o_ref[...] = r_ref[...] + _dot(a.astype(BF16), w_ref[...])


def mm_res(a, w, l, res, *, tm, tn, a_tail=None):
    t = res.shape[0]
    k, n = w.shape[1:]
    row = pl.BlockSpec((tm, tn), lambda i, j: (i, j))
    wspec = pl.BlockSpec((None, k, tn), lambda i, j: (l, 0, j))
    if a_tail is None:
        body, srcs = _mm_res_kernel, (a,)
        aspecs = [pl.BlockSpec((tm, k), lambda i, j: (i, 0))]
    else:
        n_head = a.shape[0] // tm
        assert a.shape[0] == n_head * tm and a_tail.shape[0] == tm and t == (n_head + 1) * tm
        body, srcs = functools.partial(_mm_res_tail_kernel, n_head=n_head), (a, a_tail)
        aspecs = [pl.BlockSpec((tm, k), lambda i, j: (jnp.minimum(i, n_head - 1), 0)),
                  pl.BlockSpec((tm, k), lambda i, j: (0, 0))]
    return pl.pallas_call(
        body,
        grid=(t // tm, n // tn),
        in_specs=aspecs + [wspec, row],
        out_specs=row,
        out_shape=jax.ShapeDtypeStruct((t, n), F32),
        compiler_params=_params("parallel", "parallel"),
        name="mm_res",
    )(*srcs, w, res)


def _pool_branch(window_sums, a, cnts, poolw_ref, pscale):
    outs = []
    for gi in range(len(POOL_WINDOWS)):
        sl = slice(gi * GROUP, (gi + 1) * GROUP)
        pg = window_sums[gi] / cnts[gi] - a[:, sl]
        outs.append(_dot(pg.astype(BF16), poolw_ref[gi]))
    return jnp.concatenate(outs, axis=1) * pscale


def _mixer_prompt_kernel(z_ref, poolw_ref, pscale_ref, scw_ref, cmw_ref, cmb_ref, cmg_ref,
                         cmbeta_ref, sgg_ref, sgbeta_ref, sgw_ref, sgbias_ref,
                         p_ref, pool_o, sc_o, cm_o, ext_ref, *, ts):
    tb = pl.program_id(1)
    h0 = HALO

    @pl.when(tb == 0)
    def _():
        ext_ref[:, 0:h0, :] = jnp.zeros((3, h0, D_BR), F32)

    a = z_ref[:, 0:D_BR]
    bg = z_ref[:, D_BR:2 * D_BR]
    gated = z_ref[:, 2 * D_BR:3 * D_BR] * z_ref[:, 3 * D_BR:4 * D_BR]
    glu = z_ref[:, 4 * D_BR:5 * D_BR] * _sigmoid(z_ref[:, 5 * D_BR:6 * D_BR])
    ext_ref[0, h0:h0 + ts, :] = a
    ext_ref[1, h0:h0 + ts, :] = gated
    ext_ref[2, h0:h0 + ts, :] = glu

    pos = tb * ts + lax.broadcasted_iota(jnp.int32, (ts, 1), 0)
    sums, cnts = [], []
    for gi, w in enumerate(POOL_WINDOWS):
        sl = slice(gi * GROUP, (gi + 1) * GROUP)
        s = a[:, sl]
        for j in range(1, w):
            s = s + ext_ref[0, h0 - j:h0 - j + ts, sl]
        sums.append(s)
        cnts.append(jnp.minimum(w, pos + 1).astype(F32))
    p_ref[:, 0:D_BR] = _pool_branch(sums, a, cnts, poolw_ref, pscale_ref[...]).astype(BF16)

    conv = scw_ref[SC_WIDTH - 1:SC_WIDTH, :] * gated
    for k in range(SC_WIDTH - 1):
        off = h0 - (SC_WIDTH - 1) + k
        conv = conv + scw_ref[k:k + 1, :] * ext_ref[1, off:off + ts, :]
    p_ref[:, D_BR:2 * D_BR] = (bg * conv).astype(BF16)

    c = cmw_ref[CM_WIDTH - 1:CM_WIDTH, :] * glu
    for k in range(CM_WIDTH - 1):
        off = h0 - (CM_WIDTH - 1) + k
        c = c + cmw_ref[k:k + 1, :] * ext_ref[2, off:off + ts, :]
    c = _ln(c + cmb_ref[...], cmg_ref[...], cmbeta_ref[...])
    p_ref[:, 2 * D_BR:3 * D_BR] = (c * _sigmoid(c)).astype(BF16)

    vn = _ln(z_ref[:, 7 * D_BR:8 * D_BR], sgg_ref[...], sgbeta_ref[...])
    row = lax.broadcasted_iota(jnp.int32, (CHUNK, CHUNK), 0)
    col = lax.broadcasted_iota(jnp.int32, (CHUNK, CHUNK), 1)
    tril = row >= col
    for gi in range(D_BR // GROUP):
        sl = slice(gi * GROUP, (gi + 1) * GROUP)
        wg = jnp.where(tril, sgw_ref[gi], 0.0).astype(BF16)
        for ci in range(ts // CHUNK):
            rows = slice(ci * CHUNK, (ci + 1) * CHUNK)
            mixed = _dot(wg, vn[rows, sl].astype(BF16)) + sgbias_ref[:, sl]
            p_ref[rows, 3 * D_BR + gi * GROUP:3 * D_BR + (gi + 1) * GROUP] = (
                z_ref[rows, 6 * D_BR + gi * GROUP:6 * D_BR + (gi + 1) * GROUP] * mixed).astype(BF16)

    end = h0 + ts
    pool_o[0] = ext_ref[0, end - POOL_STATE:end, :]
    sc_o[0] = ext_ref[1, end - (SC_WIDTH - 1):end, :]
    cm_o[0] = ext_ref[2, end - (CM_WIDTH - 1):end, :]
    ext_ref[:, 0:h0, :] = ext_ref[:, ts:ts + h0, :]


def mixer_prompt(z, lp, *, n_seq, seq, ts):
    nb = seq // ts
    full = lambda *shape: pl.BlockSpec(shape, lambda n, t: (0,) * len(shape))
    state = lambda rows: pl.BlockSpec((1, rows, D_BR), lambda n, t: (n, 0, 0))
    return pl.pallas_call(
        functools.partial(_mixer_prompt_kernel, ts=ts),
        grid=(n_seq, nb),
        in_specs=[pl.BlockSpec((ts, 8 * D_BR), lambda n, t: (n * nb + t, 0)),
                  full(4, GROUP, GROUP), full(1, D_BR), full(SC_WIDTH, D_BR),
                  full(CM_WIDTH, D_BR), full(1, D_BR), full(1, D_BR), full(1, D_BR),
                  full(1, D_BR), full(1, D_BR), full(4, CHUNK, CHUNK), full(CHUNK, D_BR)],
        out_specs=[pl.BlockSpec((ts, N_BRANCH * D_BR), lambda n, t: (n * nb + t, 0)),
                   state(POOL_STATE), state(SC_WIDTH - 1), state(CM_WIDTH - 1)],
        out_shape=[jax.ShapeDtypeStruct((n_seq * seq, N_BRANCH * D_BR), BF16),
                   jax.ShapeDtypeStruct((n_seq, POOL_STATE, D_BR), F32),
                   jax.ShapeDtypeStruct((n_seq, SC_WIDTH - 1, D_BR), F32),
                   jax.ShapeDtypeStruct((n_seq, CM_WIDTH - 1, D_BR), F32)],
        scratch_shapes=[pltpu.VMEM((3, HALO + ts, D_BR), F32)],
        compiler_params=_params("parallel", "arbitrary"),
        name="mixer_prompt",
    )(z, lp["pool_w"], lp["pool_scale"], lp["sc_w"], lp["cm_w"], lp["cm_b"], lp["cm_ln_g"],
      lp["cm_ln_b"], lp["sg_ln_g"], lp["sg_ln_b"], lp["sg_w"], lp["sg_bias"])


def _mixer_sample_kernel(z_ref, pool_ref, sc_ref, cm_ref, poolw_ref, pscale_ref, scw_ref, cmw_ref,
                         cmb_ref, cmg_ref, cmbeta_ref, sgg_ref, sgbeta_ref, sgwts_ref, sgbias_ref,
                         p_ref, pool_o, sc_o, cm_o, vn_o, *, td, ns, start_pos):
    rows = lambda t: slice(t * ns, (t + 1) * ns)
    col = lambda k: slice(k * D_BR, (k + 1) * D_BR)

    ext_a = [pool_ref[j] for j in range(POOL_STATE)] + [z_ref[rows(t), col(0)] for t in range(td)]
    a_all = jnp.concatenate(ext_a[POOL_STATE:], axis=0)
    sums, cnts = [], []
    for gi, w in enumerate(POOL_WINDOWS):
        sl = slice(gi * GROUP, (gi + 1) * GROUP)
        per_t = []
        for t in range(td):
            s = ext_a[POOL_STATE + t][:, sl]
            for j in range(1, w):
                s = s + ext_a[POOL_STATE + t - j][:, sl]
            per_t.append(s)
        sums.append(jnp.concatenate(per_t, axis=0))
        cnt = jnp.concatenate(
            [jnp.full((ns, 1), float(min(w, start_pos + t + 1)), F32) for t in range(td)], axis=0)
        cnts.append(cnt)
    p_ref[:, col(0)] = _pool_branch(sums, a_all, cnts, poolw_ref, pscale_ref[...]).astype(BF16)
    for j in range(POOL_STATE):
        pool_o[j] = ext_a[td + j]

    ext_b = [sc_ref[j] for j in range(SC_WIDTH - 1)]
    ext_b += [z_ref[rows(t), col(2)] * z_ref[rows(t), col(3)] for t in range(td)]
    for t in range(td):
        conv = scw_ref[0:1, :] * ext_b[t]
        for k in range(1, SC_WIDTH):
            conv = conv + scw_ref[k:k + 1, :] * ext_b[t + k]
        p_ref[rows(t), col(1)] = (z_ref[rows(t), col(1)] * conv).astype(BF16)
    for j in range(SC_WIDTH - 1):
        sc_o[j] = ext_b[td + j]

    ext_c = [cm_ref[j] for j in range(CM_WIDTH - 1)]
    ext_c += [z_ref[rows(t), col(4)] * _sigmoid(z_ref[rows(t), col(5)]) for t in range(td)]
    for t in range(td):
        c = cmw_ref[0:1, :] * ext_c[t]
        for k in range(1, CM_WIDTH):
            c = c + cmw_ref[k:k + 1, :] * ext_c[t + k]
        c = _ln(c + cmb_ref[...], cmg_ref[...], cmbeta_ref[...])
        p_ref[rows(t), col(2)] = (c * _sigmoid(c)).astype(BF16)
    for j in range(CM_WIDTH - 1):
        cm_o[j] = ext_c[td + j]

    vn = [_ln(z_ref[rows(t), col(7)], sgg_ref[...], sgbeta_ref[...]) for t in range(td)]
    for t in range(td):
        vn_o[t] = vn[t]
        mixed = sgbias_ref[t:t + 1, :]
        for s in range(t + 1):
            mixed = mixed + sgwts_ref[t, s:s + 1, :] * vn[s]
        p_ref[rows(t), col(3)] = (z_ref[rows(t), col(6)] * mixed).astype(BF16)


def mixer_sample(z, pool_tm, sc_tm, cm_tm, lp, *, td, ns, t_prompt, start_pos):
    blk = t_prompt // (td * ns)
    full = lambda *shape: pl.BlockSpec(shape, lambda i: (0,) * len(shape))
    tm_shape = lambda rows: jax.ShapeDtypeStruct((rows, ns, D_BR), F32)
    return pl.pallas_call(
        functools.partial(_mixer_sample_kernel, td=td, ns=ns, start_pos=start_pos),
        grid=(1,),
        in_specs=[pl.BlockSpec((td * ns, 8 * D_BR), lambda i: (blk, 0)),
                  full(POOL_STATE, ns, D_BR), full(SC_WIDTH - 1, ns, D_BR),
                  full(CM_WIDTH - 1, ns, D_BR),
                  full(4, GROUP, GROUP), full(1, D_BR), full(SC_WIDTH, D_BR),
                  full(CM_WIDTH, D_BR), full(1, D_BR), full(1, D_BR), full(1, D_BR),
                  full(1, D_BR), full(1, D_BR), full(td, td, D_BR), full(CHUNK, D_BR)],
        out_specs=[full(td * ns, N_BRANCH * D_BR),
                   full(POOL_STATE, ns, D_BR), full(SC_WIDTH - 1, ns, D_BR),
                   full(CM_WIDTH - 1, ns, D_BR), full(td, ns, D_BR)],
        out_shape=[jax.ShapeDtypeStruct((td * ns, N_BRANCH * D_BR), BF16), tm_shape(POOL_STATE),
                   tm_shape(SC_WIDTH - 1), tm_shape(CM_WIDTH - 1), tm_shape(td)],
        compiler_params=_params("arbitrary"),
        name="mixer_sample",
    )(z, pool_tm, sc_tm, cm_tm, lp["pool_w"], lp["pool_scale"], lp["sc_w"], lp["cm_w"],
      lp["cm_b"], lp["cm_ln_g"], lp["cm_ln_b"], lp["sg_ln_g"], lp["sg_ln_b"], lp["sg_wts"],
      lp["sg_bias"])


def _norm_in_kernel(x_ref, g_ref, w_ref, z_ref, hn_ref):
    @pl.when(pl.program_id(1) == 0)
    def _():
        hn_ref[...] = _rms(x_ref[...], g_ref[...]).astype(BF16)

    z_ref[...] = _dot(hn_ref[...], w_ref[...])


def norm_in(x, g, w, l, n_cols, *, tm, tn):
    t, d = x.shape
    return pl.pallas_call(
        _norm_in_kernel,
        grid=(t // tm, n_cols // tn),
        in_specs=[pl.BlockSpec((tm, d), lambda i, j: (i, 0)),
                  pl.BlockSpec((1, d), lambda i, j: (0, 0)),
                  pl.BlockSpec((None, d, tn), lambda i, j: (l, 0, j))],
        out_specs=[pl.BlockSpec((tm, tn), lambda i, j: (i, j)),
                   pl.BlockSpec((tm, d), lambda i, j: (i, 0))],
        out_shape=[jax.ShapeDtypeStruct((t, n_cols), F32), jax.ShapeDtypeStruct((t, d), BF16)],
        compiler_params=_params("parallel", "arbitrary"),
        name="norm_in",
    )(x, g.reshape(1, d), w)


def _gate_out_kernel(hn_ref, p_ref, pt_ref, g0_ref, g1_ref, g2_ref, g3_ref, w0_ref, w1_ref,
                     w2_ref, w3_ref, b_ref, wo_ref, x_ref, o_ref, merged_ref, *, n_head, tn):
    j = pl.program_id(1)
    p = _rows_of(pl.program_id(0), n_head, p_ref, pt_ref)
    hn = hn_ref[...]
    acc = None
    for i, (g_ref, w_ref) in enumerate(((g0_ref, w0_ref), (g1_ref, w1_ref), (g2_ref, w2_ref),
                                        (g3_ref, w3_ref))):
        gate = _sigmoid(_dot(hn, g_ref[...]) + b_ref[i:i + 1, :])
        term = gate * _dot(p[:, i * D_BR:(i + 1) * D_BR].astype(BF16), w_ref[...])
        acc = term if acc is None else acc + term
    merged_ref[:, pl.ds(pl.multiple_of(j * tn, tn), tn)] = acc.astype(BF16)

    @pl.when(j == pl.num_programs(1) - 1)
    def _():
        o_ref[...] = x_ref[...] + _dot(merged_ref[...], wo_ref[...])


def gate_out(hn, p_head, p_tail, w_in, w_outs, b_gate, w_o, l, x, *, tn):
    t, d = x.shape
    tm = p_tail.shape[0]
    n_head = p_head.shape[0] // tm
    assert p_head.shape[0] == n_head * tm and t == (n_head + 1) * tm
    first = N_BRANCH * 2 * D_BR // tn
    gate_spec = lambda i: pl.BlockSpec((None, d, tn), lambda r, c: (l, 0, first + i * (d // tn) + c))
    w_spec = pl.BlockSpec((None, D_BR, tn), lambda r, c: (l, 0, c))
    row = pl.BlockSpec((tm, d), lambda r, c: (r, 0))
    return pl.pallas_call(
        functools.partial(_gate_out_kernel, n_head=n_head, tn=tn),
        grid=(t // tm, d // tn),
        in_specs=[row,
                  pl.BlockSpec((tm, N_BRANCH * D_BR), lambda r, c: (jnp.minimum(r, n_head - 1), 0)),
                  pl.BlockSpec((tm, N_BRANCH * D_BR), lambda r, c: (0, 0)),
                  gate_spec(0), gate_spec(1), gate_spec(2), gate_spec(3),
                  w_spec, w_spec, w_spec, w_spec,
                  pl.BlockSpec((N_BRANCH, tn), lambda r, c: (0, c)),
                  pl.BlockSpec((None, d, d), lambda r, c: (l, 0, 0), pipeline_mode=pl.Buffered(1)),
                  row],
        out_specs=row,
        out_shape=jax.ShapeDtypeStruct((t, d), F32),
        scratch_shapes=[pltpu.VMEM((tm, d), BF16)],
        compiler_params=_params("parallel", "arbitrary"),
        name="gate_out",
    )(hn, p_head, p_tail, w_in, w_in, w_in, w_in, *w_outs, b_gate, w_o, x)


def _softmax_rows(s):
    m = jnp.max(s, axis=-1, keepdims=True)
    p = jnp.exp(s - m)
    return p / jnp.sum(p, axis=-1, keepdims=True)


def _attn_prompt_kernel(q_ref, k_ref, v_ref, o_ref, *, scale):
    for h in range(X_HEADS):
        sl = slice(h * GROUP, (h + 1) * GROUP)
        s = lax.dot_general(q_ref[:, sl].astype(BF16), k_ref[0, :, sl].astype(BF16),
                            (((1,), (1,)), ((), ())), preferred_element_type=F32) * scale
        o_ref[:, sl] = _dot(_softmax_rows(s).astype(BF16), v_ref[0, :, sl].astype(BF16))


def attn_prompt(q, k, v, *, n_seq, seq, tq):
    dx = q.shape[1]
    mem = k.shape[1]
    nb = seq // tq
    return pl.pallas_call(
        functools.partial(_attn_prompt_kernel, scale=GROUP ** -0.5),
        grid=(n_seq, nb),
        in_specs=[pl.BlockSpec((tq, dx), lambda n, t: (n * nb + t, 0)),
                  pl.BlockSpec((1, mem, dx), lambda n, t: (n, 0, 0)),
                  pl.BlockSpec((1, mem, dx), lambda n, t: (n, 0, 0))],
        out_specs=pl.BlockSpec((tq, dx), lambda n, t: (n * nb + t, 0)),
        out_shape=jax.ShapeDtypeStruct((n_seq * seq, dx), F32),
        compiler_params=_params("parallel", "parallel"),
        name="attn_prompt",
    )(q, k, v)


def _attn_sample_kernel(q_ref, k_ref, v_ref, o_ref, *, scale):
    rows, cols = q_ref.shape[1], k_ref.shape[1]
    row_head = lax.broadcasted_iota(jnp.int32, (rows, cols), 0) % X_HEADS
    col_head = lax.broadcasted_iota(jnp.int32, (rows, cols), 1) % X_HEADS
    s = jnp.einsum("bqd,bkd->bqk", q_ref[...].astype(BF16), k_ref[...].astype(BF16),
                   preferred_element_type=F32) * scale
    s = jnp.where((row_head == col_head)[None], s, NEG_INF)
    o_ref[...] = jnp.einsum("bqk,bkd->bqd", _softmax_rows(s).astype(BF16),
                            v_ref[...].astype(BF16), preferred_element_type=F32)


def attn_sample(q, k, v, l, *, sb):
    n_seq, rows, hd = q.shape
    kv_spec = pl.BlockSpec((None, sb, k.shape[2], hd), lambda i: (l, i, 0, 0))
    return pl.pallas_call(
        functools.partial(_attn_sample_kernel, scale=GROUP ** -0.5),
        grid=(n_seq // sb,),
        in_specs=[pl.BlockSpec((sb, rows, hd), lambda i: (i, 0, 0)), kv_spec, kv_spec],
        out_specs=pl.BlockSpec((sb, rows, hd), lambda i: (i, 0, 0)),
        out_shape=jax.ShapeDtypeStruct((n_seq, rows, hd), F32),
        compiler_params=_params("parallel"),
        name="attn_sample",
    )(q, k, v)


def _norm_t_kernel(x_ref, g_ref, o_ref):
    o_ref[...] = _rms(x_ref[...], g_ref[...]).T.astype(BF16)


def norm_t(x, g, *, tb):
    t, d = x.shape
    return pl.pallas_call(
        _norm_t_kernel,
        grid=(t // tb,),
        in_specs=[pl.BlockSpec((tb, d), lambda i: (i, 0)), pl.BlockSpec((1, d), lambda i: (0, 0))],
        out_specs=pl.BlockSpec((d, tb), lambda i: (0, i)),
        out_shape=jax.ShapeDtypeStruct((d, t), BF16),
        compiler_params=_params("parallel"),
        name="norm_t",
    )(x, g.reshape(1, d))


_CAND = [(i, j) for i in range(PEER_TOPK + 1) for j in range(PEER_TOPK + 1)
         if (i + 1) * (j + 1) <= PEER_TOPK + 1]


def _peer_pre_kernel(ht_ref, wq_ref, keys_ref, s2_o, th_o, lam_o, q_scr, s_scr, v_scr, *, tb):
    nk = GROUP
    q_scr[...] = _dot(wq_ref[...], ht_ref[...]).astype(BF16)
    for hp in range(2 * PEER_HEADS):
        s_scr[hp] = _dot(keys_ref[hp], q_scr[hp * nk:(hp + 1) * nk, :])

    idx = lax.broadcasted_iota(jnp.int32, (nk, LANES), 0)

    def chunk(ci, carry):
        lanes = pl.ds(pl.multiple_of(ci * LANES, LANES), LANES)
        tied = jnp.zeros((1, LANES), F32)
        for hp in range(2 * PEER_HEADS):
            h, p = divmod(hp, 2)
            cur = s_scr[hp, :, lanes]
            m = jnp.max(cur, axis=0, keepdims=True)
            v_scr[p, 0, h:h + 1, lanes] = m
            for r in range(1, PEER_TOPK + 1):
                m = jnp.max(jnp.where(cur < m, cur, NEG_INF), axis=0, keepdims=True)
                v_scr[p, r, h:h + 1, lanes] = m
            n_ge = jnp.sum(jnp.where(cur >= m, 1.0, 0.0), axis=0, keepdims=True)
            tied = jnp.maximum(tied, jnp.where(n_ge != PEER_TOPK + 1.0, 1.0, 0.0))

        @pl.when(jnp.max(tied) > 0.0)
        def _():
            for hp in range(2 * PEER_HEADS):
                h, p = divmod(hp, 2)
                cur = s_scr[hp, :, lanes]
                for r in range(PEER_TOPK + 1):
                    m = jnp.max(cur, axis=0, keepdims=True)
                    v_scr[p, r, h:h + 1, lanes] = m
                    if r < PEER_TOPK:
                        first = jnp.min(jnp.where(cur == m, idx, nk), axis=0, keepdims=True)
                        cur = jnp.where(idx == first, NEG_INF, cur)

        v1 = [v_scr[0, r, :, lanes] for r in range(PEER_TOPK + 1)]
        v2 = [v_scr[1, r, :, lanes] for r in range(PEER_TOPK + 1)]
        cands = [v1[i] + v2[j] for i, j in _CAND]
        kth = jnp.full_like(cands[0], NEG_INF)
        nxt = jnp.full_like(cands[0], NEG_INF)
        for ck in cands:
            cnt = jnp.zeros_like(ck)
            for cl in cands:
                cnt = cnt + jnp.where(cl >= ck, 1.0, 0.0)
            kth = jnp.maximum(kth, jnp.where(cnt >= PEER_TOPK, ck, NEG_INF))
            nxt = jnp.maximum(nxt, jnp.where(cnt >= PEER_TOPK + 1, ck, NEG_INF))
        tau = 0.5 * (kth + nxt)
        top = cands[0]
        zsum = jnp.zeros_like(top)
        for ck in cands:
            zsum = zsum + jnp.where(ck >= tau, jnp.exp(ck - top), 0.0)
        shift = top + jnp.log(zsum)
        for h in range(PEER_HEADS):
            s1 = s_scr[2 * h, :, lanes]
            th_o[h, :, lanes] = (tau[h:h + 1, :] - s1) * LOG2E
            lam_o[h, :, lanes] = (s1 - shift[h:h + 1, :]) * LOG2E
            s2_o[h * nk:(h + 1) * nk, lanes] = s_scr[2 * h + 1, :, lanes] * LOG2E
        return carry

    lax.fori_loop(0, tb // LANES, chunk, 0)


def peer_pre(ht, wq_t, keys, l, *, tb):
    d, t = ht.shape
    nq = wq_t.shape[1]
    hk = PEER_HEADS * GROUP
    flat = pl.BlockSpec((hk, tb), lambda i: (0, i))
    cube = pl.BlockSpec((PEER_HEADS, GROUP, tb), lambda i: (0, 0, i))
    return pl.pallas_call(
        functools.partial(_peer_pre_kernel, tb=tb),
        grid=(t // tb,),
        in_specs=[pl.BlockSpec((d, tb), lambda i: (0, i)),
                  pl.BlockSpec((None, nq, d), lambda i: (l, 0, 0)),
                  pl.BlockSpec((None, 2 * PEER_HEADS, GROUP, GROUP), lambda i: (l, 0, 0, 0))],
        out_specs=[flat, cube, cube],
        out_shape=[jax.ShapeDtypeStruct((hk, t), F32),
                   jax.ShapeDtypeStruct((PEER_HEADS, GROUP, t), F32),
                   jax.ShapeDtypeStruct((PEER_HEADS, GROUP, t), F32)],
        scratch_shapes=[pltpu.VMEM((nq, tb), BF16),
                        pltpu.VMEM((2 * PEER_HEADS, GROUP, tb), F32),
                        pltpu.VMEM((2, PEER_TOPK + 1, PEER_HEADS, tb), F32)],
        compiler_params=_params("parallel"),
        name="peer_pre",
    )(ht, wq_t, keys)


def _peer_dense_kernel(ht_ref, htn_ref, ua_ref, ub_ref, vt_ref, s2_ref, th_ref, lam_ref, x_ref,
                       o_ref, acc_ref, at_ref, gw_ref, *, tb, te, ne):
    s = pl.program_id(0)
    e_cur = jnp.maximum(s - 1, 0) % ne
    half = te // 2
    tbh = tb // 2
    n_a = half // GROUP
    piece = lambda k, c: 2 * k + c
    cols = lambda c: slice(c * tbh, (c + 1) * tbh)

    @pl.when(s == 0)
    def _():
        at_ref[0:2] = jnp.zeros((2, half, tbh), F32)

    @pl.when(e_cur == 0)
    def _():
        acc_ref[...] = jnp.zeros_like(acc_ref)

    def pre_activations(k, c, u_ref, h_ref):
        at_ref[piece(k, c)] = _dot(u_ref[...], h_ref[:, cols(c)])

    def weighted_activations(k, c):
        p = piece(k, c)
        for cl in range(tbh // LANES):
            lanes = slice(cl * LANES, (cl + 1) * LANES)
            glanes = slice(c * tbh + cl * LANES, c * tbh + (cl + 1) * LANES)
            for r in range(GROUP // ROWS):
                w = [jnp.zeros((ROWS, LANES), F32) for _ in range(n_a)]
                for h in range(PEER_HEADS):
                    s2 = s2_ref[h * GROUP + r * ROWS:h * GROUP + (r + 1) * ROWS, glanes]
                    for a in range(n_a):
                        ag = k * n_a + a
                        keep = s2 >= th_ref[h, ag:ag + 1, glanes]
                        w[a] = w[a] + jnp.where(keep, jnp.exp2(s2 + lam_ref[h, ag:ag + 1, glanes]), 0.0)
                for a in range(n_a):
                    rows = slice(a * GROUP + r * ROWS, a * GROUP + (r + 1) * ROWS)
                    pre = at_ref[p, rows, lanes]
                    act = 0.5 * pre * (1.0 + lax.erf(pre * math.sqrt(0.5)))
                    gw_ref[p, rows, lanes] = (w[a] * act).astype(BF16)

    def accumulate(k, c):
        acc_ref[:, cols(c)] += _dot(vt_ref[:, k * half:(k + 1) * half], gw_ref[piece(k, c)])

    pre_activations(1, 0, ub_ref, ht_ref)
    pre_activations(1, 1, ub_ref, ht_ref)
    for c in range(2):
        weighted_activations(0, c)
        accumulate(0, c)
        pre_activations(0, c, ua_ref, htn_ref)
    for c in range(2):
        weighted_activations(1, c)
        accumulate(1, c)

    @pl.when(jnp.logical_and(e_cur == ne - 1, s > 0))
    def _():
        o_ref[...] = x_ref[...] + acc_ref[...].T


def peer_dense(ht, u, vt, l, s2, theta, lam, x, *, tb, te):
    d, t = ht.shape
    ne = u.shape[1] // te
    n_tiles = (t // tb) * ne
    half = te // 2
    a_blk = te // GROUP
    once = pl.Buffered(1)
    cur = lambda s: jnp.clip(s - 1, 0, n_tiles - 1)
    nxt = lambda s: jnp.minimum(s, n_tiles - 1)
    return pl.pallas_call(
        functools.partial(_peer_dense_kernel, tb=tb, te=te, ne=ne),
        grid=(n_tiles + 1,),
        in_specs=[pl.BlockSpec((d, tb), lambda s: (0, cur(s) // ne), pipeline_mode=once),
                  pl.BlockSpec((d, tb), lambda s: (0, nxt(s) // ne), pipeline_mode=once),
                  pl.BlockSpec((None, half, d), lambda s: (l, 2 * (nxt(s) % ne), 0)),
                  pl.BlockSpec((None, half, d), lambda s: (l, 2 * (cur(s) % ne) + 1, 0)),
                  pl.BlockSpec((None, d, te), lambda s: (l, 0, cur(s) % ne)),
                  pl.BlockSpec((PEER_HEADS * GROUP, tb), lambda s: (0, cur(s) // ne),
                               pipeline_mode=once),
                  pl.BlockSpec((PEER_HEADS, a_blk, tb), lambda s: (0, cur(s) % ne, cur(s) // ne)),
                  pl.BlockSpec((PEER_HEADS, a_blk, tb), lambda s: (0, cur(s) % ne, cur(s) // ne)),
                  pl.BlockSpec((tb, d), lambda s: (cur(s) // ne, 0), pipeline_mode=once)],
        out_specs=pl.BlockSpec((tb, d), lambda s: (cur(s) // ne, 0)),
        out_shape=jax.ShapeDtypeStruct((t, d), F32),
        scratch_shapes=[pltpu.VMEM((d, tb), F32), pltpu.VMEM((4, half, tb // 2), F32),
                        pltpu.VMEM((4, half, tb // 2), BF16)],
        compiler_params=_params("arbitrary"),
        name="peer_dense",
    )(ht, ht, u, u, vt, s2, theta, lam, x)


def _final_norm_kernel(x_ref, g_ref, head_o, tail_o, *, n_head):
    i = pl.program_id(0)
    y = _rms(x_ref[...], g_ref[...])

    @pl.when(i < n_head)
    def _():
        head_o[...] = y

    @pl.when(i == n_head)
    def _():
        tail_o[...] = y


def final_norm(x, g, *, tm):
    t, d = x.shape
    n_head = t // tm - 1
    assert t == (n_head + 1) * tm
    return pl.pallas_call(
        functools.partial(_final_norm_kernel, n_head=n_head),
        grid=(n_head + 1,),
        in_specs=[pl.BlockSpec((tm, d), lambda i: (i, 0)), pl.BlockSpec((1, d), lambda i: (0, 0))],
        out_specs=[pl.BlockSpec((tm, d), lambda i: (jnp.minimum(i, n_head - 1), 0)),
                   pl.BlockSpec((tm, d), lambda i: (0, 0))],
        out_shape=[jax.ShapeDtypeStruct((n_head * tm, d), F32),
                   jax.ShapeDtypeStruct((tm, d), F32)],
        compiler_params=_params("arbitrary"),
        name="final_norm",
    )(x, g.reshape(1, d))


def _to_time_major(s):
    return jnp.swapaxes(s, 0, 1)


def kernel(x_prompt, x_sample, mem_prompt, cache_mem_k, cache_mem_v, state_pool, state_sconv, state_cconv, g_mix, w_in, b_gate, pool_w, pool_scale, w_pool_out, sc_w, w_sc_out, cm_w, cm_b, cm_ln_g, cm_ln_b, w_cm_out, sg_ln_g, sg_ln_b, sg_w, sg_b, w_sg_out, w_o, g_x, g_mem, w_xq, w_xk, w_xv, w_xo, g_peer, w_pq, peer_keys, peer_u, peer_v, g_final):
    nb, seq, d = x_prompt.shape
    ns, td, _ = x_sample.shape
    depth = w_in.shape[0]
    mem_len = mem_prompt.shape[1]
    past_len = 16384
    t_prompt = nb * seq
    t_all = t_prompt + ns * td
    dx = X_HEADS * GROUP

    tm_big = _tile(t_all, 1088, 16)
    t_s = ns * td
    assert t_prompt % t_s == 0
    tn_in = _tile(w_in.shape[2], 1024, 128)
    tn_d = _tile(d, 512, 128)
    tn_gate = _tile(d, 256, 128)
    ts = _tile(seq, 256, CHUNK)
    tq = _tile(seq, 512, 8)
    tb = _tile(t_all, 512, LANES)
    tb_pre = _tile(t_all, 256, LANES)
    te = _tile(peer_u.shape[1], 1024, 8 * GROUP)

    x = jnp.concatenate([x_prompt.reshape(t_prompt, d),
                         _to_time_major(x_sample).reshape(ns * td, d)], axis=0)
    mem2d = mem_prompt.reshape(nb * mem_len, d)
    col1 = lambda v: v.reshape(1, -1)
    cache_k = cache_mem_k.reshape(depth, ns, mem_len * X_HEADS, GROUP)
    cache_v = cache_mem_v.reshape(depth, ns, mem_len * X_HEADS, GROUP)

    bf = lambda w: w.astype(BF16)
    w_in_b, w_o_b, w_xq_b, w_xk_b, w_xv_b, w_xo_b = map(bf, (w_in, w_o, w_xq, w_xk, w_xv, w_xo))
    w_branch_b = tuple(map(bf, (w_pool_out, w_sc_out, w_cm_out, w_sg_out)))
    w_pq_t = bf(jnp.swapaxes(w_pq, 1, 2))
    keys_b = bf(peer_keys.reshape(depth, 2 * PEER_HEADS, GROUP, GROUP))
    peer_u_b = bf(peer_u)
    peer_vt = bf(jnp.swapaxes(peer_v, 1, 2))

    outs = {k: [] for k in ("mk", "mv", "pool_p", "sc_p", "cm_p", "pool_s", "sc_s", "cm_s", "cv_s")}
    for l in range(depth):
        lp = dict(
            pool_w=pool_w[l].astype(BF16), pool_scale=col1(pool_scale[l]), sc_w=sc_w[l],
            cm_w=cm_w[l], cm_b=col1(cm_b[l]), cm_ln_g=col1(cm_ln_g[l]), cm_ln_b=col1(cm_ln_b[l]),
            sg_ln_g=col1(sg_ln_g[l]), sg_ln_b=col1(sg_ln_b[l]), sg_w=sg_w[l],
            sg_bias=jnp.repeat(sg_b[l].T, GROUP, axis=1),
            sg_wts=jnp.repeat(jnp.transpose(sg_w[l][:, :td, :td], (1, 2, 0)), GROUP, axis=2))

        z, hn = norm_in(x, g_mix[l], w_in_b, l, 2 * N_BRANCH * D_BR, tm=tm_big, tn=tn_in)
        p_p, pool_p, sc_p, cm_p = mixer_prompt(z, lp, n_seq=nb, seq=seq, ts=ts)
        p_s, pool_s, sc_s, cm_s, vn_s = mixer_sample(
            z, _to_time_major(state_pool[l]), _to_time_major(state_sconv[l]),
            _to_time_major(state_cconv[l]), lp, td=td, ns=ns, t_prompt=t_prompt, start_pos=past_len)
        x = gate_out(hn, p_p, p_s, w_in_b, w_branch_b, b_gate[l], w_o_b, l, x, tn=tn_gate)

        q = norm_mm(x, g_x[l], w_xq_b, l, tm=tm_big, tn=dx)
        tm_mem = _tile(nb * mem_len, 1024, 16)
        k_p = norm_mm(mem2d, g_mem[l], w_xk_b, l, tm=tm_mem, tn=dx)
        v_p = norm_mm(mem2d, g_mem[l], w_xv_b, l, tm=tm_mem, tn=dx)
        o_p = attn_prompt(q, k_p.reshape(nb, mem_len, dx), v_p.reshape(nb, mem_len, dx),
                          n_seq=nb, seq=seq, tq=tq)
        q_s = jnp.swapaxes(q[t_prompt:].reshape(td, ns, dx), 0, 1).reshape(ns, td * X_HEADS, GROUP)
        o_s = attn_sample(q_s, cache_k, cache_v, l, sb=_tile(ns, 8, 1))
        o_s = jnp.swapaxes(o_s.reshape(ns, td, dx), 0, 1).reshape(t_s, dx)
        x = mm_res(o_p, w_xo_b, l, x, tm=t_s, tn=tn_d, a_tail=o_s)

        ht = norm_t(x, g_peer[l], tb=tb)
        s2, theta, lam = peer_pre(ht, w_pq_t, keys_b, l, tb=tb_pre)
        x = peer_dense(ht, peer_u_b, peer_vt, l, s2, theta, lam, x, tb=tb, te=te)

        outs["mk"].append(k_p.reshape(nb, mem_len, X_HEADS, GROUP))
        outs["mv"].append(v_p.reshape(nb, mem_len, X_HEADS, GROUP))
        outs["pool_p"].append(pool_p)
        outs["sc_p"].append(sc_p)
        outs["cm_p"].append(cm_p)
        outs["pool_s"].append(_to_time_major(pool_s))
        outs["sc_s"].append(_to_time_major(sc_s))
        outs["cm_s"].append(_to_time_major(cm_s))
        outs["cv_s"].append(_to_time_major(vn_s))

    y_p, y_s = final_norm(x, g_final, tm=t_s)
    y_prompt = y_p.reshape(nb, seq, d)
    y_sample = _to_time_major(y_s.reshape(td, ns, d))
    st = lambda k: jnp.stack(outs[k])
    return (y_prompt, y_sample, st("mk"), st("mv"), st("pool_p"), st("sc_p"), st("cm_p"),
            st("pool_s"), st("sc_s"), st("cm_s"), st("cv_s"))
```

```python
import functools
import math

import jax
import jax.numpy as jnp
from jax import lax
from jax.experimental import pallas as pl
from jax.experimental.pallas import tpu as pltpu

EPS = 1e-6
GROUP = 128
POOL_WINDOWS = (2, 4, 8, 16)
POOL_STATE = max(POOL_WINDOWS) - 1
SC_WIDTH = 3
CM_WIDTH = 31
D_BR = 4 * GROUP
N_BRANCH = 4
CHUNK = 128
X_HEADS = 4
PEER_HEADS = 8
PEER_TOPK = 16
HALO = 32
LANES = 128
ROWS = 32
VMEM_LIMIT = 56 * 2 ** 20

BF16 = jnp.bfloat16
F32 = jnp.float32
NEG_INF = float("-inf")
LOG2E = math.log2(math.e)


def _tile(n, pref, mult=8):
    best = None
    for t in range(mult, min(n, pref) + 1, mult):
        if n % t == 0:
            best = t
    assert best is not None, (n, pref, mult)
    return best


def _params(*sem):
    return pltpu.CompilerParams(dimension_semantics=sem, vmem_limit_bytes=VMEM_LIMIT)


def _rms(x, g):
    ms = jnp.mean(x * x, axis=-1, keepdims=True)
    return x * lax.rsqrt(ms + EPS) * g


def _ln(x, g, b):
    mu = jnp.mean(x, axis=-1, keepdims=True)
    xc = x - mu
    var = jnp.mean(xc * xc, axis=-1, keepdims=True)
    return xc * lax.rsqrt(var + EPS) * g + b


def _sigmoid(x):
    return 1.0 / (1.0 + jnp.exp(-x))


def _dot(a, b):
    return jnp.dot(a, b, preferred_element_type=F32)


def _norm_mm_kernel(x_ref, g_ref, w_ref, o_ref, hn_ref):
    @pl.when(pl.program_id(1) == 0)
    def _():
        hn_ref[...] = _rms(x_ref[...], g_ref[...]).astype(BF16)

    o_ref[...] = _dot(hn_ref[...], w_ref[...])


def norm_mm(x, g, w, l, *, tm, tn):
    t, d = x.shape
    n = w.shape[2]
    return pl.pallas_call(
        _norm_mm_kernel,
        grid=(t // tm, n // tn),
        in_specs=[pl.BlockSpec((tm, d), lambda i, j: (i, 0)),
                  pl.BlockSpec((1, d), lambda i, j: (0, 0)),
                  pl.BlockSpec((None, d, tn), lambda i, j: (l, 0, j))],
        out_specs=pl.BlockSpec((tm, tn), lambda i, j: (i, j)),
        out_shape=jax.ShapeDtypeStruct((t, n), F32),
        scratch_shapes=[pltpu.VMEM((tm, d), BF16)],
        compiler_params=_params("parallel", "arbitrary"),
        name="norm_mm",
    )(x, g.reshape(1, d), w)


def _rows_of(i, n_head, head_ref, tail_ref):
    if tail_ref is None:
        return head_ref[...]
    return jnp.where(i < n_head, head_ref[...], tail_ref[...])


def _mm_res_kernel(a_ref, w_ref, r_ref, o_ref):
    o_ref[...] = r_ref[...] + _dot(a_ref[...].astype(BF16), w_ref[...])


def _mm_res_tail_kernel(a_ref, at_ref, w_ref, r_ref, o_ref, *, n_head):
    a = _rows_of(pl.program_id(0), n_head, a_ref, at_ref)
    o_ref[...] = r_ref[...] + _dot(a.astype(BF16), w_ref[...])


def mm_res(a, w, l, res, *, tm, tn, a_tail=None):
    t = res.shape[0]
    k, n = w.shape[1:]
    row = pl.BlockSpec((tm, tn), lambda i, j: (i, j))
    wspec = pl.BlockSpec((None, k, tn), lambda i, j: (l, 0, j))
    if a_tail is None:
        body, srcs = _mm_res_kernel, (a,)
        aspecs = [pl.BlockSpec((tm, k), lambda i, j: (i, 0))]
    else:
        n_head = a.shape[0] // tm
        assert a.shape[0] == n_head * tm and a_tail.shape[0] == tm and t == (n_head + 1) * tm
        body, srcs = functools.partial(_mm_res_tail_kernel, n_head=n_head), (a, a_tail)
        aspecs = [pl.BlockSpec((tm, k), lambda i, j: (jnp.minimum(i, n_head - 1), 0)),
                  pl.BlockSpec((tm, k), lambda i, j: (0, 0))]
    return pl.pallas_call(
        body,
        grid=(t // tm, n // tn),
        in_specs=aspecs + [wspec, row],
        out_specs=row,
        out_shape=jax.ShapeDtypeStruct((t, n), F32),
        compiler_params=_params("parallel", "parallel"),
        name="mm_res",
    )(*srcs, w, res)


def _pool_branch(window_sums, a, cnts, poolw_ref, pscale):
    outs = []
    for gi in range(len(POOL_WINDOWS)):
        sl = slice(gi * GROUP, (gi + 1) * GROUP)
        pg = window_sums[gi] / cnts[gi] - a[:, sl]
        outs.append(_dot(pg.astype(BF16), poolw_ref[gi]))
    return jnp.concatenate(outs, axis=1) * pscale


def _mixer_prompt_kernel(z_ref, poolw_ref, pscale_ref, scw_ref, cmw_ref, cmb_ref, cmg_ref,
                         cmbeta_ref, sgg_ref, sgbeta_ref, sgw_ref, sgbias_ref,
                         p_ref, pool_o, sc_o, cm_o, ext_ref, *, ts):
    tb = pl.program_id(1)
    h0 = HALO

    @pl.when(tb == 0)
    def _():
        ext_ref[:, 0:h0, :] = jnp.zeros((3, h0, D_BR), F32)

    a = z_ref[:, 0:D_BR]
    bg = z_ref[:, D_BR:2 * D_BR]
    gated = z_ref[:, 2 * D_BR:3 * D_BR] * z_ref[:, 3 * D_BR:4 * D_BR]
    glu = z_ref[:, 4 * D_BR:5 * D_BR] * _sigmoid(z_ref[:, 5 * D_BR:6 * D_BR])
    ext_ref[0, h0:h0 + ts, :] = a
    ext_ref[1, h0:h0 + ts, :] = gated
    ext_ref[2, h0:h0 + ts, :] = glu

    pos = tb * ts + lax.broadcasted_iota(jnp.int32, (ts, 1), 0)
    sums, cnts = [], []
    for gi, w in enumerate(POOL_WINDOWS):
        sl = slice(gi * GROUP, (gi + 1) * GROUP)
        s = a[:, sl]
        for j in range(1, w):
            s = s + ext_ref[0, h0 - j:h0 - j + ts, sl]
        sums.append(s)
        cnts.append(jnp.minimum(w, pos + 1).astype(F32))
    p_ref[:, 0:D_BR] = _pool_branch(sums, a, cnts, poolw_ref, pscale_ref[...]).astype(BF16)

    conv = scw_ref[SC_WIDTH - 1:SC_WIDTH, :] * gated
    for k in range(SC_WIDTH - 1):
        off = h0 - (SC_WIDTH - 1) + k
        conv = conv + scw_ref[k:k + 1, :] * ext_ref[1, off:off + ts, :]
    p_ref[:, D_BR:2 * D_BR] = (bg * conv).astype(BF16)

    c = cmw_ref[CM_WIDTH - 1:CM_WIDTH, :] * glu
    for k in range(CM_WIDTH - 1):
        off = h0 - (CM_WIDTH - 1) + k
        c = c + cmw_ref[k:k + 1, :] * ext_ref[2, off:off + ts, :]
    c = _ln(c + cmb_ref[...], cmg_ref[...], cmbeta_ref[...])
    p_ref[:, 2 * D_BR:3 * D_BR] = (c * _sigmoid(c)).astype(BF16)

    vn = _ln(z_ref[:, 7 * D_BR:8 * D_BR], sgg_ref[...], sgbeta_ref[...])
    row = lax.broadcasted_iota(jnp.int32, (CHUNK, CHUNK), 0)
    col = lax.broadcasted_iota(jnp.int32, (CHUNK, CHUNK), 1)
    tril = row >= col
    for gi in range(D_BR // GROUP):
        sl = slice(gi * GROUP, (gi + 1) * GROUP)
        wg = jnp.where(tril, sgw_ref[gi], 0.0).astype(BF16)
        for ci in range(ts // CHUNK):
            rows = slice(ci * CHUNK, (ci + 1) * CHUNK)
            mixed = _dot(wg, vn[rows, sl].astype(BF16)) + sgbias_ref[:, sl]
            p_ref[rows, 3 * D_BR + gi * GROUP:3 * D_BR + (gi + 1) * GROUP] = (
                z_ref[rows, 6 * D_BR + gi * GROUP:6 * D_BR + (gi + 1) * GROUP] * mixed).astype(BF16)

    end = h0 + ts
    pool_o[0] = ext_ref[0, end - POOL_STATE:end, :]
    sc_o[0] = ext_ref[1, end - (SC_WIDTH - 1):end, :]
    cm_o[0] = ext_ref[2, end - (CM_WIDTH - 1):end, :]
    ext_ref[:, 0:h0, :] = ext_ref[:, ts:ts + h0, :]


def mixer_prompt(z, lp, *, n_seq, seq, ts):
    nb = seq // ts
    full = lambda *shape: pl.BlockSpec(shape, lambda n, t: (0,) * len(shape))
    state = lambda rows: pl.BlockSpec((1, rows, D_BR), lambda n, t: (n, 0, 0))
    return pl.pallas_call(
        functools.partial(_mixer_prompt_kernel, ts=ts),
        grid=(n_seq, nb),
        in_specs=[pl.BlockSpec((ts, 8 * D_BR), lambda n, t: (n * nb + t, 0)),
                  full(4, GROUP, GROUP), full(1, D_BR), full(SC_WIDTH, D_BR),
                  full(CM_WIDTH, D_BR), full(1, D_BR), full(1, D_BR), full(1, D_BR),
                  full(1, D_BR), full(1, D_BR), full(4, CHUNK, CHUNK), full(CHUNK, D_BR)],
        out_specs=[pl.BlockSpec((ts, N_BRANCH * D_BR), lambda n, t: (n * nb + t, 0)),
                   state(POOL_STATE), state(SC_WIDTH - 1), state(CM_WIDTH - 1)],
        out_shape=[jax.ShapeDtypeStruct((n_seq * seq, N_BRANCH * D_BR), BF16),
                   jax.ShapeDtypeStruct((n_seq, POOL_STATE, D_BR), F32),
                   jax.ShapeDtypeStruct((n_seq, SC_WIDTH - 1, D_BR), F32),
                   jax.ShapeDtypeStruct((n_seq, CM_WIDTH - 1, D_BR), F32)],
        scratch_shapes=[pltpu.VMEM((3, HALO + ts, D_BR), F32)],
        compiler_params=_params("parallel", "arbitrary"),
        name="mixer_prompt",
    )(z, lp["pool_w"], lp["pool_scale"], lp["sc_w"], lp["cm_w"], lp["cm_b"], lp["cm_ln_g"],
      lp["cm_ln_b"], lp["sg_ln_g"], lp["sg_ln_b"], lp["sg_w"], lp["sg_bias"])


def _mixer_sample_kernel(z_ref, pool_ref, sc_ref, cm_ref, poolw_ref, pscale_ref, scw_ref, cmw_ref,
                         cmb_ref, cmg_ref, cmbeta_ref, sgg_ref, sgbeta_ref, sgwts_ref, sgbias_ref,
                         p_ref, pool_o, sc_o, cm_o, vn_o, *, td, ns, start_pos):
    rows = lambda t: slice(t * ns, (t + 1) * ns)
    col = lambda k: slice(k * D_BR, (k + 1) * D_BR)

    ext_a = [pool_ref[j] for j in range(POOL_STATE)] + [z_ref[rows(t), col(0)] for t in range(td)]
    a_all = jnp.concatenate(ext_a[POOL_STATE:], axis=0)
    sums, cnts = [], []
    for gi, w in enumerate(POOL_WINDOWS):
        sl = slice(gi * GROUP, (gi + 1) * GROUP)
        per_t = []
        for t in range(td):
            s = ext_a[POOL_STATE + t][:, sl]
            for j in range(1, w):
                s = s + ext_a[POOL_STATE + t - j][:, sl]
            per_t.append(s)
        sums.append(jnp.concatenate(per_t, axis=0))
        cnt = jnp.concatenate(
            [jnp.full((ns, 1), float(min(w, start_pos + t + 1)), F32) for t in range(td)], axis=0)
        cnts.append(cnt)
    p_ref[:, col(0)] = _pool_branch(sums, a_all, cnts, poolw_ref, pscale_ref[...]).astype(BF16)
    for j in range(POOL_STATE):
        pool_o[j] = ext_a[td + j]

    ext_b = [sc_ref[j] for j in range(SC_WIDTH - 1)]
    ext_b += [z_ref[rows(t), col(2)] * z_ref[rows(t), col(3)] for t in range(td)]
    for t in range(td):
        conv = scw_ref[0:1, :] * ext_b[t]
        for k in range(1, SC_WIDTH):
            conv = conv + scw_ref[k:k + 1, :] * ext_b[t + k]
        p_ref[rows(t), col(1)] = (z_ref[rows(t), col(1)] * conv).astype(BF16)
    for j in range(SC_WIDTH - 1):
        sc_o[j] = ext_b[td + j]

    ext_c = [cm_ref[j] for j in range(CM_WIDTH - 1)]
    ext_c += [z_ref[rows(t), col(4)] * _sigmoid(z_ref[rows(t), col(5)]) for t in range(td)]
    for t in range(td):
        c = cmw_ref[0:1, :] * ext_c[t]
        for k in range(1, CM_WIDTH):
            c = c + cmw_ref[k:k + 1, :] * ext_c[t + k]
        c = _ln(c + cmb_ref[...], cmg_ref[...], cmbeta_ref[...])
        p_ref[rows(t), col(2)] = (c * _sigmoid(c)).astype(BF16)
    for j in range(CM_WIDTH - 1):
        cm_o[j] = ext_c[td + j]

    vn = [_ln(z_ref[rows(t), col(7)], sgg_ref[...], sgbeta_ref[...]) for t in range(td)]
    for t in range(td):
        vn_o[t] = vn[t]
        mixed = sgbias_ref[t:t + 1, :]
        for s in range(t + 1):
            mixed = mixed + sgwts_ref[t, s:s + 1, :] * vn[s]
        p_ref[rows(t), col(3)] = (z_ref[rows(t), col(6)] * mixed).astype(BF16)


def mixer_sample(z, pool_tm, sc_tm, cm_tm, lp, *, td, ns, t_prompt, start_pos):
    blk = t_prompt // (td * ns)
    full = lambda *shape: pl.BlockSpec(shape, lambda i: (0,) * len(shape))
    tm_shape = lambda rows: jax.ShapeDtypeStruct((rows, ns, D_BR), F32)
    return pl.pallas_call(
        functools.partial(_mixer_sample_kernel, td=td, ns=ns, start_pos=start_pos),
        grid=(1,),
        in_specs=[pl.BlockSpec((td * ns, 8 * D_BR), lambda i: (blk, 0)),
                  full(POOL_STATE, ns, D_BR), full(SC_WIDTH - 1, ns, D_BR),
                  full(CM_WIDTH - 1, ns, D_BR),
                  full(4, GROUP, GROUP), full(1, D_BR), full(SC_WIDTH, D_BR),
                  full(CM_WIDTH, D_BR), full(1, D_BR), full(1, D_BR), full(1, D_BR),
                  full(1, D_BR), full(1, D_BR), full(td, td, D_BR), full(CHUNK, D_BR)],
        out_specs=[full(td * ns, N_BRANCH * D_BR),
                   full(POOL_STATE, ns, D_BR), full(SC_WIDTH - 1, ns, D_BR),
                   full(CM_WIDTH - 1, ns, D_BR), full(td, ns, D_BR)],
        out_shape=[jax.ShapeDtypeStruct((td * ns, N_BRANCH * D_BR), BF16), tm_shape(POOL_STATE),
                   tm_shape(SC_WIDTH - 1), tm_shape(CM_WIDTH - 1), tm_shape(td)],
        compiler_params=_params("arbitrary"),
        name="mixer_sample",
    )(z, pool_tm, sc_tm, cm_tm, lp["pool_w"], lp["pool_scale"], lp["sc_w"], lp["cm_w"],
      lp["cm_b"], lp["cm_ln_g"], lp["cm_ln_b"], lp["sg_ln_g"], lp["sg_ln_b"], lp["sg_wts"],
      lp["sg_bias"])


def _norm_in_kernel(x_ref, g_ref, w_ref, z_ref, hn_ref):
    @pl.when(pl.program_id(1) == 0)
    def _():
        hn_ref[...] = _rms(x_ref[...], g_ref[...]).astype(BF16)

    z_ref[...] = _dot(hn_ref[...], w_ref[...])


def norm_in(x, g, w, l, n_cols, *, tm, tn):
    t, d = x.shape
    return pl.pallas_call(
        _norm_in_kernel,
        grid=(t // tm, n_cols // tn),
        in_specs=[pl.BlockSpec((tm, d), lambda i, j: (i, 0)),
                  pl.BlockSpec((1, d), lambda i, j: (0, 0)),
                  pl.BlockSpec((None, d, tn), lambda i, j: (l, 0, j))],
        out_specs=[pl.BlockSpec((tm, tn), lambda i, j: (i, j)),
                   pl.BlockSpec((tm, d), lambda i, j: (i, 0))],
        out_shape=[jax.ShapeDtypeStruct((t, n_cols), F32), jax.ShapeDtypeStruct((t, d), BF16)],
        compiler_params=_params("parallel", "arbitrary"),
        name="norm_in",
    )(x, g.reshape(1, d), w)


def _gate_out_kernel(hn_ref, p_ref, pt_ref, g0_ref, g1_ref, g2_ref, g3_ref, w0_ref, w1_ref,
                     w2_ref, w3_ref, b_ref, wo_ref, x_ref, o_ref, merged_ref, *, n_head, tn):
    j = pl.program_id(1)
    p = _rows_of(pl.program_id(0), n_head, p_ref, pt_ref)
    hn = hn_ref[...]
    acc = None
    for i, (g_ref, w_ref) in enumerate(((g0_ref, w0_ref), (g1_ref, w1_ref), (g2_ref, w2_ref),
                                        (g3_ref, w3_ref))):
        gate = _sigmoid(_dot(hn, g_ref[...]) + b_ref[i:i + 1, :])
        term = gate * _dot(p[:, i * D_BR:(i + 1) * D_BR].astype(BF16), w_ref[...])
        acc = term if acc is None else acc + term
    merged_ref[:, pl.ds(pl.multiple_of(j * tn, tn), tn)] = acc.astype(BF16)

    @pl.when(j == pl.num_programs(1) - 1)
    def _():
        o_ref[...] = x_ref[...] + _dot(merged_ref[...], wo_ref[...])


def gate_out(hn, p_head, p_tail, w_in, w_outs, b_gate, w_o, l, x, *, tn):
    t, d = x.shape
    tm = p_tail.shape[0]
    n_head = p_head.shape[0] // tm
    assert p_head.shape[0] == n_head * tm and t == (n_head + 1) * tm
    first = N_BRANCH * 2 * D_BR // tn
    gate_spec = lambda i: pl.BlockSpec((None, d, tn), lambda r, c: (l, 0, first + i * (d // tn) + c))
    w_spec = pl.BlockSpec((None, D_BR, tn), lambda r, c: (l, 0, c))
    row = pl.BlockSpec((tm, d), lambda r, c: (r, 0))
    return pl.pallas_call(
        functools.partial(_gate_out_kernel, n_head=n_head, tn=tn),
        grid=(t // tm, d // tn),
        in_specs=[row,
                  pl.BlockSpec((tm, N_BRANCH * D_BR), lambda r, c: (jnp.minimum(r, n_head - 1), 0)),
                  pl.BlockSpec((tm, N_BRANCH * D_BR), lambda r, c: (0, 0)),
                  gate_spec(0), gate_spec(1), gate_spec(2), gate_spec(3),
                  w_spec, w_spec, w_spec, w_spec,
                  pl.BlockSpec((N_BRANCH, tn), lambda r, c: (0, c)),
                  pl.BlockSpec((None, d, d), lambda r, c: (l, 0, 0), pipeline_mode=pl.Buffered(1)),
                  row],
        out_specs=row,
        out_shape=jax.ShapeDtypeStruct((t, d), F32),
        scratch_shapes=[pltpu.VMEM((tm, d), BF16)],
        compiler_params=_params("parallel", "arbitrary"),
        name="gate_out",
    )(hn, p_head, p_tail, w_in, w_in, w_in, w_in, *w_outs, b_gate, w_o, x)


def _softmax_rows(s):
    m = jnp.max(s, axis=-1, keepdims=True)
    p = jnp.exp(s - m)
    return p / jnp.sum(p, axis=-1, keepdims=True)


def _attn_prompt_kernel(q_ref, k_ref, v_ref, o_ref, *, scale):
    for h in range(X_HEADS):
        sl = slice(h * GROUP, (h + 1) * GROUP)
        s = lax.dot_general(q_ref[:, sl].astype(BF16), k_ref[0, :, sl].astype(BF16),
                            (((1,), (1,)), ((), ())), preferred_element_type=F32) * scale
        o_ref[:, sl] = _dot(_softmax_rows(s).astype(BF16), v_ref[0, :, sl].astype(BF16))


def attn_prompt(q, k, v, *, n_seq, seq, tq):
    dx = q.shape[1]
    mem = k.shape[1]
    nb = seq // tq
    return pl.pallas_call(
        functools.partial(_attn_prompt_kernel, scale=GROUP ** -0.5),
        grid=(n_seq, nb),
        in_specs=[pl.BlockSpec((tq, dx), lambda n, t: (n * nb + t, 0)),
                  pl.BlockSpec((1, mem, dx), lambda n, t: (n, 0, 0)),
                  pl.BlockSpec((1, mem, dx), lambda n, t: (n, 0, 0))],
        out_specs=pl.BlockSpec((tq, dx), lambda n, t: (n * nb + t, 0)),
        out_shape=jax.ShapeDtypeStruct((n_seq * seq, dx), F32),
        compiler_params=_params("parallel", "parallel"),
        name="attn_prompt",
    )(q, k, v)


def _attn_sample_kernel(q_ref, k_ref, v_ref, o_ref, *, scale):
    rows, cols = q_ref.shape[1], k_ref.shape[1]
    row_head = lax.broadcasted_iota(jnp.int32, (rows, cols), 0) % X_HEADS
    col_head = lax.broadcasted_iota(jnp.int32, (rows, cols), 1) % X_HEADS
    s = jnp.einsum("bqd,bkd->bqk", q_ref[...].astype(BF16), k_ref[...].astype(BF16),
                   preferred_element_type=F32) * scale
    s = jnp.where((row_head == col_head)[None], s, NEG_INF)
    o_ref[...] = jnp.einsum("bqk,bkd->bqd", _softmax_rows(s).astype(BF16),
                            v_ref[...].astype(BF16), preferred_element_type=F32)


def attn_sample(q, k, v, l, *, sb):
    n_seq, rows, hd = q.shape
    kv_spec = pl.BlockSpec((None, sb, k.shape[2], hd), lambda i: (l, i, 0, 0))
    return pl.pallas_call(
        functools.partial(_attn_sample_kernel, scale=GROUP ** -0.5),
        grid=(n_seq // sb,),
        in_specs=[pl.BlockSpec((sb, rows, hd), lambda i: (i, 0, 0)), kv_spec, kv_spec],
        out_specs=pl.BlockSpec((sb, rows, hd), lambda i: (i, 0, 0)),
        out_shape=jax.ShapeDtypeStruct((n_seq, rows, hd), F32),
        compiler_params=_params("parallel"),
        name="attn_sample",
    )(q, k, v)


def _norm_t_kernel(x_ref, g_ref, o_ref):
    o_ref[...] = _rms(x_ref[...], g_ref[...]).T.astype(BF16)


def norm_t(x, g, *, tb):
    t, d = x.shape
    return pl.pallas_call(
        _norm_t_kernel,
        grid=(t // tb,),
        in_specs=[pl.BlockSpec((tb, d), lambda i: (i, 0)), pl.BlockSpec((1, d), lambda i: (0, 0))],
        out_specs=pl.BlockSpec((d, tb), lambda i: (0, i)),
        out_shape=jax.ShapeDtypeStruct((d, t), BF16),
        compiler_params=_params("parallel"),
        name="norm_t",
    )(x, g.reshape(1, d))


_CAND = [(i, j) for i in range(PEER_TOPK + 1) for j in range(PEER_TOPK + 1)
         if (i + 1) * (j + 1) <= PEER_TOPK + 1]


def _peer_pre_kernel(ht_ref, wq_ref, keys_ref, s2_o, th_o, lam_o, q_scr, s_scr, v_scr, *, tb):
    nk = GROUP
    q_scr[...] = _dot(wq_ref[...], ht_ref[...]).astype(BF16)
    for hp in range(2 * PEER_HEADS):
        s_scr[hp] = _dot(keys_ref[hp], q_scr[hp * nk:(hp + 1) * nk, :])

    idx = lax.broadcasted_iota(jnp.int32, (nk, LANES), 0)

    def chunk(ci, carry):
        lanes = pl.ds(pl.multiple_of(ci * LANES, LANES), LANES)
        tied = jnp.zeros((1, LANES), F32)
        for hp in range(2 * PEER_HEADS):
            h, p = divmod(hp, 2)
            cur = s_scr[hp, :, lanes]
            m = jnp.max(cur, axis=0, keepdims=True)
            v_scr[p, 0, h:h + 1, lanes] = m
            for r in range(1, PEER_TOPK + 1):
                m = jnp.max(jnp.where(cur < m, cur, NEG_INF), axis=0, keepdims=True)
                v_scr[p, r, h:h + 1, lanes] = m
            n_ge = jnp.sum(jnp.where(cur >= m, 1.0, 0.0), axis=0, keepdims=True)
            tied = jnp.maximum(tied, jnp.where(n_ge != PEER_TOPK + 1.0, 1.0, 0.0))

        @pl.when(jnp.max(tied) > 0.0)
        def _():
            for hp in range(2 * PEER_HEADS):
                h, p = divmod(hp, 2)
                cur = s_scr[hp, :, lanes]
                for r in range(PEER_TOPK + 1):
                    m = jnp.max(cur, axis=0, keepdims=True)
                    v_scr[p, r, h:h + 1, lanes] = m
                    if r < PEER_TOPK:
                        first = jnp.min(jnp.where(cur == m, idx, nk), axis=0, keepdims=True)
                        cur = jnp.where(idx == first, NEG_INF, cur)

        v1 = [v_scr[0, r, :, lanes] for r in range(PEER_TOPK + 1)]
        v2 = [v_scr[1, r, :, lanes] for r in range(PEER_TOPK + 1)]
        cands = [v1[i] + v2[j] for i, j in _CAND]
        kth = jnp.full_like(cands[0], NEG_INF)
        nxt = jnp.full_like(cands[0], NEG_INF)
        for ck in cands:
            cnt = jnp.zeros_like(ck)
            for cl in cands:
                cnt = cnt + jnp.where(cl >= ck, 1.0, 0.0)
            kth = jnp.maximum(kth, jnp.where(cnt >= PEER_TOPK, ck, NEG_INF))
            nxt = jnp.maximum(nxt, jnp.where(cnt >= PEER_TOPK + 1, ck, NEG_INF))
        tau = 0.5 * (kth + nxt)
        top = cands[0]
        zsum = jnp.zeros_like(top)
        for ck in cands:
            zsum = zsum + jnp.where(ck >= tau, jnp.exp(ck - top), 0.0)
        shift = top + jnp.log(zsum)
        for h in range(PEER_HEADS):
            s1 = s_scr[2 * h, :, lanes]
            th_o[h, :, lanes] = (tau[h:h + 1, :] - s1) * LOG2E
            lam_o[h, :, lanes] = (s1 - shift[h:h + 1, :]) * LOG2E
            s2_o[h * nk:(h + 1) * nk, lanes] = s_scr[2 * h + 1, :, lanes] * LOG2E
        return carry

    lax.fori_loop(0, tb // LANES, chunk, 0)


def peer_pre(ht, wq_t, keys, l, *, tb):
    d, t = ht.shape
    nq = wq_t.shape[1]
    hk = PEER_HEADS * GROUP
    flat = pl.BlockSpec((hk, tb), lambda i: (0, i))
    cube = pl.BlockSpec((PEER_HEADS, GROUP, tb), lambda i: (0, 0, i))
    return pl.pallas_call(
        functools.partial(_peer_pre_kernel, tb=tb),
        grid=(t // tb,),
        in_specs=[pl.BlockSpec((d, tb), lambda i: (0, i)),
                  pl.BlockSpec((None, nq, d), lambda i: (l, 0, 0)),
                  pl.BlockSpec((None, 2 * PEER_HEADS, GROUP, GROUP), lambda i: (l, 0, 0, 0))],
        out_specs=[flat, cube, cube],
        out_shape=[jax.ShapeDtypeStruct((hk, t), F32),
                   jax.ShapeDtypeStruct((PEER_HEADS, GROUP, t), F32),
                   jax.ShapeDtypeStruct((PEER_HEADS, GROUP, t), F32)],
        scratch_shapes=[pltpu.VMEM((nq, tb), BF16),
                        pltpu.VMEM((2 * PEER_HEADS, GROUP, tb), F32),
                        pltpu.VMEM((2, PEER_TOPK + 1, PEER_HEADS, tb), F32)],
        compiler_params=_params("parallel"),
        name="peer_pre",
    )(ht, wq_t, keys)


def _peer_dense_kernel(ht_ref, htn_ref, ua_ref, ub_ref, vt_ref, s2_ref, th_ref, lam_ref, x_ref,
                       o_ref, acc_ref, at_ref, gw_ref, *, tb, te, ne):
    s = pl.program_id(0)
    e_cur = jnp.maximum(s - 1, 0) % ne
    half = te // 2
    tbh = tb // 2
    n_a = half // GROUP
    piece = lambda k, c: 2 * k + c
    cols = lambda c: slice(c * tbh, (c + 1) * tbh)

    @pl.when(s == 0)
    def _():
        at_ref[0:2] = jnp.zeros((2, half, tbh), F32)

    @pl.when(e_cur == 0)
    def _():
        acc_ref[...] = jnp.zeros_like(acc_ref)

    def pre_activations(k, c, u_ref, h_ref):
        at_ref[piece(k, c)] = _dot(u_ref[...], h_ref[:, cols(c)])

    def weighted_activations(k, c):
        p = piece(k, c)
        for cl in range(tbh // LANES):
            lanes = slice(cl * LANES, (cl + 1) * LANES)
            glanes = slice(c * tbh + cl * LANES, c * tbh + (cl + 1) * LANES)
            for r in range(GROUP // ROWS):
                w = [jnp.zeros((ROWS, LANES), F32) for _ in range(n_a)]
                for h in range(PEER_HEADS):
                    s2 = s2_ref[h * GROUP + r * ROWS:h * GROUP + (r + 1) * ROWS, glanes]
                    for a in range(n_a):
                        ag = k * n_a + a
                        keep = s2 >= th_ref[h, ag:ag + 1, glanes]
                        w[a] = w[a] + jnp.where(keep, jnp.exp2(s2 + lam_ref[h, ag:ag + 1, glanes]), 0.0)
                for a in range(n_a):
                    rows = slice(a * GROUP + r * ROWS, a * GROUP + (r + 1) * ROWS)
                    pre = at_ref[p, rows, lanes]
                    act = 0.5 * pre * (1.0 + lax.erf(pre * math.sqrt(0.5)))
                    gw_ref[p, rows, lanes] = (w[a] * act).astype(BF16)

    def accumulate(k, c):
        acc_ref[:, cols(c)] += _dot(vt_ref[:, k * half:(k + 1) * half], gw_ref[piece(k, c)])

    pre_activations(1, 0, ub_ref, ht_ref)
    pre_activations(1, 1, ub_ref, ht_ref)
    for c in range(2):
        weighted_activations(0, c)
        accumulate(0, c)
        pre_activations(0, c, ua_ref, htn_ref)
    for c in range(2):
        weighted_activations(1, c)
        accumulate(1, c)

    @pl.when(jnp.logical_and(e_cur == ne - 1, s > 0))
    def _():
        o_ref[...] = x_ref[...] + acc_ref[...].T


def peer_dense(ht, u, vt, l, s2, theta, lam, x, *, tb, te):
    d, t = ht.shape
    ne = u.shape[1] // te
    assert vt.shape[1:] == (ne, d, te)
    n_tiles = (t // tb) * ne
    half = te // 2
    a_blk = te // GROUP
    once = pl.Buffered(1)
    cur = lambda s: jnp.clip(s - 1, 0, n_tiles - 1)
    nxt = lambda s: jnp.minimum(s, n_tiles - 1)
    return pl.pallas_call(
        functools.partial(_peer_dense_kernel, tb=tb, te=te, ne=ne),
        grid=(n_tiles + 1,),
        in_specs=[pl.BlockSpec((d, tb), lambda s: (0, cur(s) // ne), pipeline_mode=once),
                  pl.BlockSpec((d, tb), lambda s: (0, nxt(s) // ne), pipeline_mode=once),
                  pl.BlockSpec((None, half, d), lambda s: (l, 2 * (nxt(s) % ne), 0)),
                  pl.BlockSpec((None, half, d), lambda s: (l, 2 * (cur(s) % ne) + 1, 0)),
                  pl.BlockSpec((None, None, d, te), lambda s: (l, cur(s) % ne, 0, 0)),
                  pl.BlockSpec((PEER_HEADS * GROUP, tb), lambda s: (0, cur(s) // ne),
                               pipeline_mode=once),
                  pl.BlockSpec((PEER_HEADS, a_blk, tb), lambda s: (0, cur(s) % ne, cur(s) // ne)),
                  pl.BlockSpec((PEER_HEADS, a_blk, tb), lambda s: (0, cur(s) % ne, cur(s) // ne)),
                  pl.BlockSpec((tb, d), lambda s: (cur(s) // ne, 0), pipeline_mode=once)],
        out_specs=pl.BlockSpec((tb, d), lambda s: (cur(s) // ne, 0)),
        out_shape=jax.ShapeDtypeStruct((t, d), F32),
        scratch_shapes=[pltpu.VMEM((d, tb), F32), pltpu.VMEM((4, half, tb // 2), F32),
                        pltpu.VMEM((4, half, tb // 2), BF16)],
        compiler_params=_params("arbitrary"),
        name="peer_dense",
    )(ht, ht, u, u, vt, s2, theta, lam, x)


def _final_norm_kernel(x_ref, g_ref, head_o, tail_o, *, n_head):
    i = pl.program_id(0)
    y = _rms(x_ref[...], g_ref[...])

    @pl.when(i < n_head)
    def _():
        head_o[...] = y

    @pl.when(i == n_head)
    def _():
        tail_o[...] = y


def final_norm(x, g, *, tm):
    t, d = x.shape
    n_head = t // tm - 1
    assert t == (n_head + 1) * tm
    return pl.pallas_call(
        functools.partial(_final_norm_kernel, n_head=n_head),
        grid=(n_head + 1,),
        in_specs=[pl.BlockSpec((tm, d), lambda i: (i, 0)), pl.BlockSpec((1, d), lambda i: (0, 0))],
        out_specs=[pl.BlockSpec((tm, d), lambda i: (jnp.minimum(i, n_head - 1), 0)),
                   pl.BlockSpec((tm, d), lambda i: (0, 0))],
        out_shape=[jax.ShapeDtypeStruct((n_head * tm, d), F32),
                   jax.ShapeDtypeStruct((tm, d), F32)],
        compiler_params=_params("arbitrary"),
        name="final_norm",
    )(x, g.reshape(1, d))


def _to_time_major(s):
    return jnp.swapaxes(s, 0, 1)


def kernel(x_prompt, x_sample, mem_prompt, cache_mem_k, cache_mem_v, state_pool, state_sconv, state_cconv, g_mix, w_in, b_gate, pool_w, pool_scale, w_pool_out, sc_w, w_sc_out, cm_w, cm_b, cm_ln_g, cm_ln_b, w_cm_out, sg_ln_g, sg_ln_b, sg_w, sg_b, w_sg_out, w_o, g_x, g_mem, w_xq, w_xk, w_xv, w_xo, g_peer, w_pq, peer_keys, peer_u, peer_v, g_final):
    nb, seq, d = x_prompt.shape
    ns, td, _ = x_sample.shape
    depth = w_in.shape[0]
    mem_len = mem_prompt.shape[1]
    past_len = 16384
    t_prompt = nb * seq
    t_all = t_prompt + ns * td
    dx = X_HEADS * GROUP

    tm_big = _tile(t_all, 1088, 16)
    t_s = ns * td
    assert t_prompt % t_s == 0
    tn_in = _tile(w_in.shape[2], 1024, 128)
    tn_d = _tile(d, 512, 128)
    tn_gate = _tile(d, 256, 128)
    ts = _tile(seq, 256, CHUNK)
    tq = _tile(seq, 512, 8)
    tb = _tile(t_all, 512, LANES)
    tb_pre = _tile(t_all, 256, LANES)
    te = _tile(peer_u.shape[1], 1024, 8 * GROUP)

    x = jnp.concatenate([x_prompt.reshape(t_prompt, d),
                         _to_time_major(x_sample).reshape(ns * td, d)], axis=0)
    mem2d = mem_prompt.reshape(nb * mem_len, d)
    col1 = lambda v: v.reshape(1, -1)
    cache_k = cache_mem_k.reshape(depth, ns, mem_len * X_HEADS, GROUP)
    cache_v = cache_mem_v.reshape(depth, ns, mem_len * X_HEADS, GROUP)

    bf = lambda w: w.astype(BF16)
    w_in_b, w_o_b, w_xq_b, w_xk_b, w_xv_b, w_xo_b = map(bf, (w_in, w_o, w_xq, w_xk, w_xv, w_xo))
    w_branch_b = tuple(map(bf, (w_pool_out, w_sc_out, w_cm_out, w_sg_out)))
    w_pq_t = bf(jnp.swapaxes(w_pq, 1, 2))
    keys_b = bf(peer_keys.reshape(depth, 2 * PEER_HEADS, GROUP, GROUP))
    peer_u_b = bf(peer_u)
    peer_vt = bf(jnp.swapaxes(peer_v.reshape(depth, -1, te, d), 2, 3))

    outs = {k: [] for k in ("mk", "mv", "pool_p", "sc_p", "cm_p", "pool_s", "sc_s", "cm_s", "cv_s")}
    for l in range(depth):
        lp = dict(
            pool_w=pool_w[l].astype(BF16), pool_scale=col1(pool_scale[l]), sc_w=sc_w[l],
            cm_w=cm_w[l], cm_b=col1(cm_b[l]), cm_ln_g=col1(cm_ln_g[l]), cm_ln_b=col1(cm_ln_b[l]),
            sg_ln_g=col1(sg_ln_g[l]), sg_ln_b=col1(sg_ln_b[l]), sg_w=sg_w[l],
            sg_bias=jnp.repeat(sg_b[l].T, GROUP, axis=1),
            sg_wts=jnp.repeat(jnp.transpose(sg_w[l][:, :td, :td], (1, 2, 0)), GROUP, axis=2))

        z, hn = norm_in(x, g_mix[l], w_in_b, l, 2 * N_BRANCH * D_BR, tm=tm_big, tn=tn_in)
        p_p, pool_p, sc_p, cm_p = mixer_prompt(z, lp, n_seq=nb, seq=seq, ts=ts)
        p_s, pool_s, sc_s, cm_s, vn_s = mixer_sample(
            z, _to_time_major(state_pool[l]), _to_time_major(state_sconv[l]),
            _to_time_major(state_cconv[l]), lp, td=td, ns=ns, t_prompt=t_prompt, start_pos=past_len)
        x = gate_out(hn, p_p, p_s, w_in_b, w_branch_b, b_gate[l], w_o_b, l, x, tn=tn_gate)

        q = norm_mm(x, g_x[l], w_xq_b, l, tm=tm_big, tn=dx)
        tm_mem = _tile(nb * mem_len, 1024, 16)
        k_p = norm_mm(mem2d, g_mem[l], w_xk_b, l, tm=tm_mem, tn=dx)
        v_p = norm_mm(mem2d, g_mem[l], w_xv_b, l, tm=tm_mem, tn=dx)
        o_p = attn_prompt(q, k_p.reshape(nb, mem_len, dx), v_p.reshape(nb, mem_len, dx),
                          n_seq=nb, seq=seq, tq=tq)
        q_s = jnp.swapaxes(q[t_prompt:].reshape(td, ns, dx), 0, 1).reshape(ns, td * X_HEADS, GROUP)
        o_s = attn_sample(q_s, cache_k, cache_v, l, sb=_tile(ns, 8, 1))
        o_s = jnp.swapaxes(o_s.reshape(ns, td, dx), 0, 1).reshape(t_s, dx)
        x = mm_res(o_p, w_xo_b, l, x, tm=t_s, tn=tn_d, a_tail=o_s)

        ht = norm_t(x, g_peer[l], tb=tb)
        s2, theta, lam = peer_pre(ht, w_pq_t, keys_b, l, tb=tb_pre)
        x = peer_dense(ht, peer_u_b, peer_vt, l, s2, theta, lam, x, tb=tb, te=te)

        outs["mk"].append(k_p.reshape(nb, mem_len, X_HEADS, GROUP))
        outs["mv"].append(v_p.reshape(nb, mem_len, X_HEADS, GROUP))
        outs["pool_p"].append(pool_p)
        outs["sc_p"].append(sc_p)
        outs["cm_p"].append(cm_p)
        outs["pool_s"].append(_to_time_major(pool_s))
        outs["sc_s"].append(_to_time_major(sc_s))
        outs["cm_s"].append(_to_time_major(cm_s))
        outs["cv_s"].append(_to_time_major(vn_s))

    y_p, y_s = final_norm(x, g_final, tm=t_s)
    y_prompt = y_p.reshape(nb, seq, d)
    y_sample = _to_time_major(y_s.reshape(td, ns, d))
    st = lambda k: jnp.stack(outs[k])
    return (y_prompt, y_sample, st("mk"), st("mv"), st("pool_p"), st("sc_p"), st("cm_p"),
            st("pool_s"), st("sc_s"), st("cm_s"), st("cv_s"))
```

```python
import functools
import math

import jax
import jax.numpy as jnp
from jax import lax
from jax.experimental import pallas as pl
from jax.experimental.pallas import tpu as pltpu

EPS = 1e-6
GROUP = 128
POOL_WINDOWS = (2, 4, 8, 16)
POOL_STATE = max(POOL_WINDOWS) - 1
SC_WIDTH = 3
CM_WIDTH = 31
D_BR = 4 * GROUP
N_BRANCH = 4
CHUNK = 128
X_HEADS = 4
PEER_HEADS = 8
PEER_TOPK = 16
HALO = 32
LANES = 128
ROWS = 32
VMEM_LIMIT = 56 * 2 ** 20

BF16 = jnp.bfloat16
F32 = jnp.float32
NEG_INF = float("-inf")
LOG2E = math.log2(math.e)


def _tile(n, pref, mult=8):
    best = None
    for t in range(mult, min(n, pref) + 1, mult):
        if n % t == 0:
            best = t
    assert best is not None, (n, pref, mult)
    return best


def _params(*sem):
    return pltpu.CompilerParams(dimension_semantics=sem, vmem_limit_bytes=VMEM_LIMIT)


def _rms(x, g):
    ms = jnp.mean(x * x, axis=-1, keepdims=True)
    return x * lax.rsqrt(ms + EPS) * g


def _ln(x, g, b):
    mu = jnp.mean(x, axis=-1, keepdims=True)
    xc = x - mu
    var = jnp.mean(xc * xc, axis=-1, keepdims=True)
    return xc * lax.rsqrt(var + EPS) * g + b


def _sigmoid(x):
    return 1.0 / (1.0 + jnp.exp(-x))


def _dot(a, b):
    return jnp.dot(a, b, preferred_element_type=F32)


def _norm_mm_kernel(x_ref, g_ref, w_ref, o_ref, hn_ref):
    @pl.when(pl.program_id(1) == 0)
    def _():
        hn_ref[...] = _rms(x_ref[...], g_ref[...]).astype(BF16)

    o_ref[...] = _dot(hn_ref[...], w_ref[...])


def norm_mm(x, g, w, l, *, tm, tn):
    t, d = x.shape
    n = w.shape[2]
    return pl.pallas_call(
        _norm_mm_kernel,
        grid=(t // tm, n // tn),
        in_specs=[pl.BlockSpec((tm, d), lambda i, j: (i, 0)),
                  pl.BlockSpec((1, d), lambda i, j: (0, 0)),
                  pl.BlockSpec((None, d, tn), lambda i, j: (l, 0, j))],
        out_specs=pl.BlockSpec((tm, tn), lambda i, j: (i, j)),
        out_shape=jax.ShapeDtypeStruct((t, n), F32),
        scratch_shapes=[pltpu.VMEM((tm, d), BF16)],
        compiler_params=_params("parallel", "arbitrary"),
        name="norm_mm",
    )(x, g.reshape(1, d), w)


def _rows_of(i, n_head, head_ref, tail_ref):
    return jnp.where(i < n_head, head_ref[...], tail_ref[...])


def _attn_out_kernel(a_ref, at_ref, w_ref, r_ref, g_ref, o_ref, ht_ref, *, n_head):
    a = _rows_of(pl.program_id(0), n_head, a_ref, at_ref)
    x = r_ref[...] + _dot(a.astype(BF16), w_ref[...])
    o_ref[...] = x
    ht_ref[...] = _rms(x, g_ref[...]).T.astype(BF16)


def attn_out(a, a_tail, w, l, res, g):
    t, d = res.shape
    tm, k = a_tail.shape
    n_head = a.shape[0] // tm
    assert a.shape[0] == n_head * tm and t == (n_head + 1) * tm and tm % LANES == 0
    row = pl.BlockSpec((tm, d), lambda i: (i, 0))
    return pl.pallas_call(
        functools.partial(_attn_out_kernel, n_head=n_head),
        grid=(t // tm,),
        in_specs=[pl.BlockSpec((tm, k), lambda i: (jnp.minimum(i, n_head - 1), 0)),
                  pl.BlockSpec((tm, k), lambda i: (0, 0)),
                  pl.BlockSpec((None, k, d), lambda i: (l, 0, 0)),
                  row, pl.BlockSpec((1, d), lambda i: (0, 0))],
        out_specs=[row, pl.BlockSpec((d, tm), lambda i: (0, i))],
        out_shape=[jax.ShapeDtypeStruct((t, d), F32), jax.ShapeDtypeStruct((d, t), BF16)],
        compiler_params=_params("parallel"),
        name="attn_out",
    )(a, a_tail, w, res, g.reshape(1, d))


def _pool_branch(window_sums, a, cnts, poolw_ref, pscale):
    outs = []
    for gi in range(len(POOL_WINDOWS)):
        sl = slice(gi * GROUP, (gi + 1) * GROUP)
        pg = window_sums[gi] / cnts[gi] - a[:, sl]
        outs.append(_dot(pg.astype(BF16), poolw_ref[gi]))
    return jnp.concatenate(outs, axis=1) * pscale


def _mixer_prompt_kernel(z_ref, poolw_ref, pscale_ref, scw_ref, cmw_ref, cmb_ref, cmg_ref,
                         cmbeta_ref, sgg_ref, sgbeta_ref, sgw_ref, sgbias_ref,
                         p_ref, pool_o, sc_o, cm_o, ext_ref, *, ts):
    tb = pl.program_id(1)
    h0 = HALO

    @pl.when(tb == 0)
    def _():
        ext_ref[:, 0:h0, :] = jnp.zeros((3, h0, D_BR), F32)

    a = z_ref[:, 0:D_BR]
    bg = z_ref[:, D_BR:2 * D_BR]
    gated = z_ref[:, 2 * D_BR:3 * D_BR] * z_ref[:, 3 * D_BR:4 * D_BR]
    glu = z_ref[:, 4 * D_BR:5 * D_BR] * _sigmoid(z_ref[:, 5 * D_BR:6 * D_BR])
    ext_ref[0, h0:h0 + ts, :] = a
    ext_ref[1, h0:h0 + ts, :] = gated
    ext_ref[2, h0:h0 + ts, :] = glu

    pos = tb * ts + lax.broadcasted_iota(jnp.int32, (ts, 1), 0)
    sums, cnts = [], []
    for gi, w in enumerate(POOL_WINDOWS):
        sl = slice(gi * GROUP, (gi + 1) * GROUP)
        s = a[:, sl]
        for j in range(1, w):
            s = s + ext_ref[0, h0 - j:h0 - j + ts, sl]
        sums.append(s)
        cnts.append(jnp.minimum(w, pos + 1).astype(F32))
    p_ref[:, 0:D_BR] = _pool_branch(sums, a, cnts, poolw_ref, pscale_ref[...]).astype(BF16)

    conv = scw_ref[SC_WIDTH - 1:SC_WIDTH, :] * gated
    for k in range(SC_WIDTH - 1):
        off = h0 - (SC_WIDTH - 1) + k
        conv = conv + scw_ref[k:k + 1, :] * ext_ref[1, off:off + ts, :]
    p_ref[:, D_BR:2 * D_BR] = (bg * conv).astype(BF16)

    c = cmw_ref[CM_WIDTH - 1:CM_WIDTH, :] * glu
    for k in range(CM_WIDTH - 1):
        off = h0 - (CM_WIDTH - 1) + k
        c = c + cmw_ref[k:k + 1, :] * ext_ref[2, off:off + ts, :]
    c = _ln(c + cmb_ref[...], cmg_ref[...], cmbeta_ref[...])
    p_ref[:, 2 * D_BR:3 * D_BR] = (c * _sigmoid(c)).astype(BF16)

    vn = _ln(z_ref[:, 7 * D_BR:8 * D_BR], sgg_ref[...], sgbeta_ref[...])
    row = lax.broadcasted_iota(jnp.int32, (CHUNK, CHUNK), 0)
    col = lax.broadcasted_iota(jnp.int32, (CHUNK, CHUNK), 1)
    tril = row >= col
    for gi in range(D_BR // GROUP):
        sl = slice(gi * GROUP, (gi + 1) * GROUP)
        wg = jnp.where(tril, sgw_ref[gi], 0.0).astype(BF16)
        for ci in range(ts // CHUNK):
            rows = slice(ci * CHUNK, (ci + 1) * CHUNK)
            mixed = _dot(wg, vn[rows, sl].astype(BF16)) + sgbias_ref[:, sl]
            p_ref[rows, 3 * D_BR + gi * GROUP:3 * D_BR + (gi + 1) * GROUP] = (
                z_ref[rows, 6 * D_BR + gi * GROUP:6 * D_BR + (gi + 1) * GROUP] * mixed).astype(BF16)

    end = h0 + ts
    pool_o[0] = ext_ref[0, end - POOL_STATE:end, :]
    sc_o[0] = ext_ref[1, end - (SC_WIDTH - 1):end, :]
    cm_o[0] = ext_ref[2, end - (CM_WIDTH - 1):end, :]
    ext_ref[:, 0:h0, :] = ext_ref[:, ts:ts + h0, :]


def mixer_prompt(z, lp, *, n_seq, seq, ts):
    nb = seq // ts
    full = lambda *shape: pl.BlockSpec(shape, lambda n, t: (0,) * len(shape))
    state = lambda rows: pl.BlockSpec((1, rows, D_BR), lambda n, t: (n, 0, 0))
    return pl.pallas_call(
        functools.partial(_mixer_prompt_kernel, ts=ts),
        grid=(n_seq, nb),
        in_specs=[pl.BlockSpec((ts, 8 * D_BR), lambda n, t: (n * nb + t, 0)),
                  full(4, GROUP, GROUP), full(1, D_BR), full(SC_WIDTH, D_BR),
                  full(CM_WIDTH, D_BR), full(1, D_BR), full(1, D_BR), full(1, D_BR),
                  full(1, D_BR), full(1, D_BR), full(4, CHUNK, CHUNK), full(CHUNK, D_BR)],
        out_specs=[pl.BlockSpec((ts, N_BRANCH * D_BR), lambda n, t: (n * nb + t, 0)),
                   state(POOL_STATE), state(SC_WIDTH - 1), state(CM_WIDTH - 1)],
        out_shape=[jax.ShapeDtypeStruct((n_seq * seq, N_BRANCH * D_BR), BF16),
                   jax.ShapeDtypeStruct((n_seq, POOL_STATE, D_BR), F32),
                   jax.ShapeDtypeStruct((n_seq, SC_WIDTH - 1, D_BR), F32),
                   jax.ShapeDtypeStruct((n_seq, CM_WIDTH - 1, D_BR), F32)],
        scratch_shapes=[pltpu.VMEM((3, HALO + ts, D_BR), F32)],
        compiler_params=_params("parallel", "arbitrary"),
        name="mixer_prompt",
    )(z, lp["pool_w"], lp["pool_scale"], lp["sc_w"], lp["cm_w"], lp["cm_b"], lp["cm_ln_g"],
      lp["cm_ln_b"], lp["sg_ln_g"], lp["sg_ln_b"], lp["sg_w"], lp["sg_bias"])


def _mixer_sample_kernel(z_ref, pool_ref, sc_ref, cm_ref, poolw_ref, pscale_ref, scw_ref, cmw_ref,
                         cmb_ref, cmg_ref, cmbeta_ref, sgg_ref, sgbeta_ref, sgwts_ref, sgbias_ref,
                         p_ref, pool_o, sc_o, cm_o, vn_o, *, td, ns, start_pos):
    rows = lambda t: slice(t * ns, (t + 1) * ns)
    col = lambda k: slice(k * D_BR, (k + 1) * D_BR)

    ext_a = [pool_ref[j] for j in range(POOL_STATE)] + [z_ref[rows(t), col(0)] for t in range(td)]
    a_all = jnp.concatenate(ext_a[POOL_STATE:], axis=0)
    sums, cnts = [], []
    for gi, w in enumerate(POOL_WINDOWS):
        sl = slice(gi * GROUP, (gi + 1) * GROUP)
        per_t = []
        for t in range(td):
            s = ext_a[POOL_STATE + t][:, sl]
            for j in range(1, w):
                s = s + ext_a[POOL_STATE + t - j][:, sl]
            per_t.append(s)
        sums.append(jnp.concatenate(per_t, axis=0))
        cnt = jnp.concatenate(
            [jnp.full((ns, 1), float(min(w, start_pos + t + 1)), F32) for t in range(td)], axis=0)
        cnts.append(cnt)
    p_ref[:, col(0)] = _pool_branch(sums, a_all, cnts, poolw_ref, pscale_ref[...]).astype(BF16)
    for j in range(POOL_STATE):
        pool_o[j] = ext_a[td + j]

    ext_b = [sc_ref[j] for j in range(SC_WIDTH - 1)]
    ext_b += [z_ref[rows(t), col(2)] * z_ref[rows(t), col(3)] for t in range(td)]
    for t in range(td):
        conv = scw_ref[0:1, :] * ext_b[t]
        for k in range(1, SC_WIDTH):
            conv = conv + scw_ref[k:k + 1, :] * ext_b[t + k]
        p_ref[rows(t), col(1)] = (z_ref[rows(t), col(1)] * conv).astype(BF16)
    for j in range(SC_WIDTH - 1):
        sc_o[j] = ext_b[td + j]

    ext_c = [cm_ref[j] for j in range(CM_WIDTH - 1)]
    ext_c += [z_ref[rows(t), col(4)] * _sigmoid(z_ref[rows(t), col(5)]) for t in range(td)]
    for t in range(td):
        c = cmw_ref[0:1, :] * ext_c[t]
        for k in range(1, CM_WIDTH):
            c = c + cmw_ref[k:k + 1, :] * ext_c[t + k]
        c = _ln(c + cmb_ref[...], cmg_ref[...], cmbeta_ref[...])
        p_ref[rows(t), col(2)] = (c * _sigmoid(c)).astype(BF16)
    for j in range(CM_WIDTH - 1):
        cm_o[j] = ext_c[td + j]

    vn = [_ln(z_ref[rows(t), col(7)], sgg_ref[...], sgbeta_ref[...]) for t in range(td)]
    for t in range(td):
        vn_o[t] = vn[t]
        mixed = sgbias_ref[t:t + 1, :]
        for s in range(t + 1):
            mixed = mixed + sgwts_ref[t, s:s + 1, :] * vn[s]
        p_ref[rows(t), col(3)] = (z_ref[rows(t), col(6)] * mixed).astype(BF16)


def mixer_sample(z, pool_tm, sc_tm, cm_tm, lp, *, td, ns, t_prompt, start_pos):
    blk = t_prompt // (td * ns)
    full = lambda *shape: pl.BlockSpec(shape, lambda i: (0,) * len(shape))
    tm_shape = lambda rows: jax.ShapeDtypeStruct((rows, ns, D_BR), F32)
    return pl.pallas_call(
        functools.partial(_mixer_sample_kernel, td=td, ns=ns, start_pos=start_pos),
        grid=(1,),
        in_specs=[pl.BlockSpec((td * ns, 8 * D_BR), lambda i: (blk, 0)),
                  full(POOL_STATE, ns, D_BR), full(SC_WIDTH - 1, ns, D_BR),
                  full(CM_WIDTH - 1, ns, D_BR),
                  full(4, GROUP, GROUP), full(1, D_BR), full(SC_WIDTH, D_BR),
                  full(CM_WIDTH, D_BR), full(1, D_BR), full(1, D_BR), full(1, D_BR),
                  full(1, D_BR), full(1, D_BR), full(td, td, D_BR), full(CHUNK, D_BR)],
        out_specs=[full(td * ns, N_BRANCH * D_BR),
                   full(POOL_STATE, ns, D_BR), full(SC_WIDTH - 1, ns, D_BR),
                   full(CM_WIDTH - 1, ns, D_BR), full(td, ns, D_BR)],
        out_shape=[jax.ShapeDtypeStruct((td * ns, N_BRANCH * D_BR), BF16), tm_shape(POOL_STATE),
                   tm_shape(SC_WIDTH - 1), tm_shape(CM_WIDTH - 1), tm_shape(td)],
        compiler_params=_params("arbitrary"),
        name="mixer_sample",
    )(z, pool_tm, sc_tm, cm_tm, lp["pool_w"], lp["pool_scale"], lp["sc_w"], lp["cm_w"],
      lp["cm_b"], lp["cm_ln_g"], lp["cm_ln_b"], lp["sg_ln_g"], lp["sg_ln_b"], lp["sg_wts"],
      lp["sg_bias"])


def _norm_in_kernel(x_ref, g_ref, w_ref, z_ref, hn_ref):
    @pl.when(pl.program_id(1) == 0)
    def _():
        hn_ref[...] = _rms(x_ref[...], g_ref[...]).astype(BF16)

    z_ref[...] = _dot(hn_ref[...], w_ref[...])


def norm_in(x, g, w, l, n_cols, *, tm, tn):
    t, d = x.shape
    return pl.pallas_call(
        _norm_in_kernel,
        grid=(t // tm, n_cols // tn),
        in_specs=[pl.BlockSpec((tm, d), lambda i, j: (i, 0)),
                  pl.BlockSpec((1, d), lambda i, j: (0, 0)),
                  pl.BlockSpec((None, d, tn), lambda i, j: (l, 0, j))],
        out_specs=[pl.BlockSpec((tm, tn), lambda i, j: (i, j)),
                   pl.BlockSpec((tm, d), lambda i, j: (i, 0))],
        out_shape=[jax.ShapeDtypeStruct((t, n_cols), F32), jax.ShapeDtypeStruct((t, d), BF16)],
        compiler_params=_params("parallel", "arbitrary"),
        name="norm_in",
    )(x, g.reshape(1, d), w)


def _gate_out_kernel(hn_ref, p_ref, pt_ref, g0_ref, g1_ref, g2_ref, g3_ref, w0_ref, w1_ref,
                     w2_ref, w3_ref, b_ref, wo_ref, x_ref, o_ref, merged_ref, *, n_head, tn):
    j = pl.program_id(1)
    p = _rows_of(pl.program_id(0), n_head, p_ref, pt_ref)
    hn = hn_ref[...]
    acc = None
    for i, (g_ref, w_ref) in enumerate(((g0_ref, w0_ref), (g1_ref, w1_ref), (g2_ref, w2_ref),
                                        (g3_ref, w3_ref))):
        gate = _sigmoid(_dot(hn, g_ref[...]) + b_ref[i:i + 1, :])
        term = gate * _dot(p[:, i * D_BR:(i + 1) * D_BR].astype(BF16), w_ref[...])
        acc = term if acc is None else acc + term
    merged_ref[:, pl.ds(pl.multiple_of(j * tn, tn), tn)] = acc.astype(BF16)

    @pl.when(j == pl.num_programs(1) - 1)
    def _():
        o_ref[...] = x_ref[...] + _dot(merged_ref[...], wo_ref[...])


def gate_out(hn, p_head, p_tail, w_in, w_outs, b_gate, w_o, l, x, *, tn):
    t, d = x.shape
    tm = p_tail.shape[0]
    n_head = p_head.shape[0] // tm
    assert p_head.shape[0] == n_head * tm and t == (n_head + 1) * tm
    first = N_BRANCH * 2 * D_BR // tn
    gate_spec = lambda i: pl.BlockSpec((None, d, tn), lambda r, c: (l, 0, first + i * (d // tn) + c))
    w_spec = pl.BlockSpec((None, D_BR, tn), lambda r, c: (l, 0, c))
    row = pl.BlockSpec((tm, d), lambda r, c: (r, 0))
    return pl.pallas_call(
        functools.partial(_gate_out_kernel, n_head=n_head, tn=tn),
        grid=(t // tm, d // tn),
        in_specs=[row,
                  pl.BlockSpec((tm, N_BRANCH * D_BR), lambda r, c: (jnp.minimum(r, n_head - 1), 0)),
                  pl.BlockSpec((tm, N_BRANCH * D_BR), lambda r, c: (0, 0)),
                  gate_spec(0), gate_spec(1), gate_spec(2), gate_spec(3),
                  w_spec, w_spec, w_spec, w_spec,
                  pl.BlockSpec((N_BRANCH, tn), lambda r, c: (0, c)),
                  pl.BlockSpec((None, d, d), lambda r, c: (l, 0, 0), pipeline_mode=pl.Buffered(1)),
                  row],
        out_specs=row,
        out_shape=jax.ShapeDtypeStruct((t, d), F32),
        scratch_shapes=[pltpu.VMEM((tm, d), BF16)],
        compiler_params=_params("parallel", "arbitrary"),
        name="gate_out",
    )(hn, p_head, p_tail, w_in, w_in, w_in, w_in, *w_outs, b_gate, w_o, x)


def _softmax_rows(s):
    m = jnp.max(s, axis=-1, keepdims=True)
    p = jnp.exp(s - m)
    return p / jnp.sum(p, axis=-1, keepdims=True)


def _attn_prompt_kernel(q_ref, k_ref, v_ref, o_ref, *, scale):
    for h in range(X_HEADS):
        sl = slice(h * GROUP, (h + 1) * GROUP)
        s = lax.dot_general(q_ref[:, sl].astype(BF16), k_ref[0, :, sl].astype(BF16),
                            (((1,), (1,)), ((), ())), preferred_element_type=F32) * scale
        o_ref[:, sl] = _dot(_softmax_rows(s).astype(BF16), v_ref[0, :, sl].astype(BF16))


def attn_prompt(q, k, v, *, n_seq, seq, tq):
    dx = q.shape[1]
    mem = k.shape[1]
    nb = seq // tq
    return pl.pallas_call(
        functools.partial(_attn_prompt_kernel, scale=GROUP ** -0.5),
        grid=(n_seq, nb),
        in_specs=[pl.BlockSpec((tq, dx), lambda n, t: (n * nb + t, 0)),
                  pl.BlockSpec((1, mem, dx), lambda n, t: (n, 0, 0)),
                  pl.BlockSpec((1, mem, dx), lambda n, t: (n, 0, 0))],
        out_specs=pl.BlockSpec((tq, dx), lambda n, t: (n * nb + t, 0)),
        out_shape=jax.ShapeDtypeStruct((n_seq * seq, dx), F32),
        compiler_params=_params("parallel", "parallel"),
        name="attn_prompt",
    )(q, k, v)


def _attn_sample_kernel(q_ref, k_ref, v_ref, o_ref, *, scale):
    rows, cols = q_ref.shape[1], k_ref.shape[1]
    row_head = lax.broadcasted_iota(jnp.int32, (rows, cols), 0) % X_HEADS
    col_head = lax.broadcasted_iota(jnp.int32, (rows, cols), 1) % X_HEADS
    s = jnp.einsum("bqd,bkd->bqk", q_ref[...].astype(BF16), k_ref[...].astype(BF16),
                   preferred_element_type=F32) * scale
    s = jnp.where((row_head == col_head)[None], s, NEG_INF)
    o_ref[...] = jnp.einsum("bqk,bkd->bqd", _softmax_rows(s).astype(BF16),
                            v_ref[...].astype(BF16), preferred_element_type=F32)


def attn_sample(q, k, v, l, *, sb):
    n_seq, rows, hd = q.shape
    kv_spec = pl.BlockSpec((None, sb, k.shape[2], hd), lambda i: (l, i, 0, 0))
    return pl.pallas_call(
        functools.partial(_attn_sample_kernel, scale=GROUP ** -0.5),
        grid=(n_seq // sb,),
        in_specs=[pl.BlockSpec((sb, rows, hd), lambda i: (i, 0, 0)), kv_spec, kv_spec],
        out_specs=pl.BlockSpec((sb, rows, hd), lambda i: (i, 0, 0)),
        out_shape=jax.ShapeDtypeStruct((n_seq, rows, hd), F32),
        compiler_params=_params("parallel"),
        name="attn_sample",
    )(q, k, v)


_CAND = [(i, j) for i in range(PEER_TOPK + 1) for j in range(PEER_TOPK + 1)
         if (i + 1) * (j + 1) <= PEER_TOPK + 1]


def _peer_pre_kernel(ht_ref, wq_ref, keys_ref, s2_o, th_o, lam_o, q_scr, s_scr, v_scr, *, tb):
    nk = GROUP
    q_scr[...] = _dot(wq_ref[...], ht_ref[...]).astype(BF16)
    for hp in range(2 * PEER_HEADS):
        s_scr[hp] = _dot(keys_ref[hp], q_scr[hp * nk:(hp + 1) * nk, :])

    idx = lax.broadcasted_iota(jnp.int32, (nk, LANES), 0)

    def chunk(ci, carry):
        lanes = pl.ds(pl.multiple_of(ci * LANES, LANES), LANES)
        tied = jnp.zeros((1, LANES), F32)
        for hp in range(2 * PEER_HEADS):
            h, p = divmod(hp, 2)
            cur = s_scr[hp, :, lanes]
            m = jnp.max(cur, axis=0, keepdims=True)
            v_scr[p, 0, h:h + 1, lanes] = m
            for r in range(1, PEER_TOPK + 1):
                m = jnp.max(jnp.where(cur < m, cur, NEG_INF), axis=0, keepdims=True)
                v_scr[p, r, h:h + 1, lanes] = m
            n_ge = jnp.sum(jnp.where(cur >= m, 1.0, 0.0), axis=0, keepdims=True)
            tied = jnp.maximum(tied, jnp.where(n_ge != PEER_TOPK + 1.0, 1.0, 0.0))

        @pl.when(jnp.max(tied) > 0.0)
        def _():
            for hp in range(2 * PEER_HEADS):
                h, p = divmod(hp, 2)
                cur = s_scr[hp, :, lanes]
                for r in range(PEER_TOPK + 1):
                    m = jnp.max(cur, axis=0, keepdims=True)
                    v_scr[p, r, h:h + 1, lanes] = m
                    if r < PEER_TOPK:
                        first = jnp.min(jnp.where(cur == m, idx, nk), axis=0, keepdims=True)
                        cur = jnp.where(idx == first, NEG_INF, cur)

        v1 = [v_scr[0, r, :, lanes] for r in range(PEER_TOPK + 1)]
        v2 = [v_scr[1, r, :, lanes] for r in range(PEER_TOPK + 1)]
        cands = [v1[i] + v2[j] for i, j in _CAND]
        kth = jnp.full_like(cands[0], NEG_INF)
        nxt = jnp.full_like(cands[0], NEG_INF)
        for ck in cands:
            cnt = jnp.zeros_like(ck)
            for cl in cands:
                cnt = cnt + jnp.where(cl >= ck, 1.0, 0.0)
            kth = jnp.maximum(kth, jnp.where(cnt >= PEER_TOPK, ck, NEG_INF))
            nxt = jnp.maximum(nxt, jnp.where(cnt >= PEER_TOPK + 1, ck, NEG_INF))
        tau = 0.5 * (kth + nxt)
        top = cands[0]
        zsum = jnp.zeros_like(top)
        for ck in cands:
            zsum = zsum + jnp.where(ck >= tau, jnp.exp(ck - top), 0.0)
        shift = top + jnp.log(zsum)
        for h in range(PEER_HEADS):
            s1 = s_scr[2 * h, :, lanes]
            th_o[h, :, lanes] = (tau[h:h + 1, :] - s1) * LOG2E
            lam_o[h, :, lanes] = (s1 - shift[h:h + 1, :]) * LOG2E - 0.5
            s2_o[h * nk:(h + 1) * nk, lanes] = s_scr[2 * h + 1, :, lanes] * LOG2E
        return carry

    lax.fori_loop(0, tb // LANES, chunk, 0)


def peer_pre(ht, wq_t, keys, l, *, tb):
    d, t = ht.shape
    nq = wq_t.shape[1]
    hk = PEER_HEADS * GROUP
    flat = pl.BlockSpec((hk, tb), lambda i: (0, i))
    cube = pl.BlockSpec((PEER_HEADS, GROUP, tb), lambda i: (0, 0, i))
    return pl.pallas_call(
        functools.partial(_peer_pre_kernel, tb=tb),
        grid=(t // tb,),
        in_specs=[pl.BlockSpec((d, tb), lambda i: (0, i)),
                  pl.BlockSpec((None, nq, d), lambda i: (l, 0, 0)),
                  pl.BlockSpec((None, 2 * PEER_HEADS, GROUP, GROUP), lambda i: (l, 0, 0, 0))],
        out_specs=[flat, cube, cube],
        out_shape=[jax.ShapeDtypeStruct((hk, t), F32),
                   jax.ShapeDtypeStruct((PEER_HEADS, GROUP, t), F32),
                   jax.ShapeDtypeStruct((PEER_HEADS, GROUP, t), F32)],
        scratch_shapes=[pltpu.VMEM((nq, tb), BF16),
                        pltpu.VMEM((2 * PEER_HEADS, GROUP, tb), F32),
                        pltpu.VMEM((2, PEER_TOPK + 1, PEER_HEADS, tb), F32)],
        compiler_params=_params("parallel"),
        name="peer_pre",
    )(ht, wq_t, keys)


def _peer_dense_kernel(ht_ref, htn_ref, ua_ref, ub_ref, vt_ref, s2_ref, th_ref, lam_ref, x_ref,
                       o_ref, acc_ref, at_ref, gw_ref, *, tb, te, ne):
    s = pl.program_id(0)
    e_cur = jnp.maximum(s - 1, 0) % ne
    half = te // 2
    tbh = tb // 2
    n_a = half // GROUP
    piece = lambda k, c: 2 * k + c
    cols = lambda c: slice(c * tbh, (c + 1) * tbh)

    @pl.when(s == 0)
    def _():
        at_ref[0:2] = jnp.zeros((2, half, tbh), F32)

    @pl.when(e_cur == 0)
    def _():
        acc_ref[...] = jnp.zeros_like(acc_ref)

    def pre_activations(k, c, u_ref, h_ref):
        at_ref[piece(k, c)] = _dot(u_ref[...], h_ref[:, cols(c)])

    def weighted_activations(k, c):
        p = piece(k, c)
        for cl in range(tbh // LANES):
            lanes = slice(cl * LANES, (cl + 1) * LANES)
            glanes = slice(c * tbh + cl * LANES, c * tbh + (cl + 1) * LANES)
            for r in range(GROUP // ROWS):
                w = [jnp.zeros((ROWS, LANES), F32) for _ in range(n_a)]
                for h in range(PEER_HEADS):
                    s2 = s2_ref[h * GROUP + r * ROWS:h * GROUP + (r + 1) * ROWS, glanes]
                    for a in range(n_a):
                        ag = k * n_a + a
                        keep = s2 >= th_ref[h, ag:ag + 1, glanes]
                        w[a] = w[a] + jnp.where(keep, jnp.exp2(s2 + lam_ref[h, ag:ag + 1, glanes]), 0.0)
                for a in range(n_a):
                    rows = slice(a * GROUP + r * ROWS, a * GROUP + (r + 1) * ROWS)
                    z = at_ref[p, rows, lanes]
                    gw_ref[p, rows, lanes] = (w[a] * (z * (1.0 + lax.erf(z)))).astype(BF16)

    def accumulate(k, c):
        acc_ref[:, cols(c)] += _dot(vt_ref[:, k * half:(k + 1) * half], gw_ref[piece(k, c)])

    pre_activations(1, 0, ub_ref, ht_ref)
    pre_activations(1, 1, ub_ref, ht_ref)
    for c in range(2):
        weighted_activations(0, c)
        accumulate(0, c)
        pre_activations(0, c, ua_ref, htn_ref)
    for c in range(2):
        weighted_activations(1, c)
        accumulate(1, c)

    @pl.when(jnp.logical_and(e_cur == ne - 1, s > 0))
    def _():
        o_ref[...] = x_ref[...] + acc_ref[...].T


def peer_dense(ht, u, vt, l, s2, theta, lam, x, *, tb, te):
    d, t = ht.shape
    ne = u.shape[1] // te
    assert vt.shape[1:] == (ne, d, te)
    n_tiles = (t // tb) * ne
    half = te // 2
    a_blk = te // GROUP
    once = pl.Buffered(1)
    cur = lambda s: jnp.clip(s - 1, 0, n_tiles - 1)
    nxt = lambda s: jnp.minimum(s, n_tiles - 1)
    return pl.pallas_call(
        functools.partial(_peer_dense_kernel, tb=tb, te=te, ne=ne),
        grid=(n_tiles + 1,),
        in_specs=[pl.BlockSpec((d, tb), lambda s: (0, cur(s) // ne), pipeline_mode=once),
                  pl.BlockSpec((d, tb), lambda s: (0, nxt(s) // ne), pipeline_mode=once),
                  pl.BlockSpec((None, half, d), lambda s: (l, 2 * (nxt(s) % ne), 0)),
                  pl.BlockSpec((None, half, d), lambda s: (l, 2 * (cur(s) % ne) + 1, 0)),
                  pl.BlockSpec((None, None, d, te), lambda s: (l, cur(s) % ne, 0, 0)),
                  pl.BlockSpec((PEER_HEADS * GROUP, tb), lambda s: (0, cur(s) // ne),
                               pipeline_mode=once),
                  pl.BlockSpec((PEER_HEADS, a_blk, tb), lambda s: (0, cur(s) % ne, cur(s) // ne)),
                  pl.BlockSpec((PEER_HEADS, a_blk, tb), lambda s: (0, cur(s) % ne, cur(s) // ne)),
                  pl.BlockSpec((tb, d), lambda s: (cur(s) // ne, 0), pipeline_mode=once)],
        out_specs=pl.BlockSpec((tb, d), lambda s: (cur(s) // ne, 0)),
        out_shape=jax.ShapeDtypeStruct((t, d), F32),
        scratch_shapes=[pltpu.VMEM((d, tb), F32), pltpu.VMEM((4, half, tb // 2), F32),
                        pltpu.VMEM((4, half, tb // 2), BF16)],
        compiler_params=_params("arbitrary"),
        name="peer_dense",
    )(ht, ht, u, u, vt, s2, theta, lam, x)


def _final_norm_kernel(x_ref, g_ref, head_o, tail_o, *, n_head):
    i = pl.program_id(0)
    y = _rms(x_ref[...], g_ref[...])

    @pl.when(i < n_head)
    def _():
        head_o[...] = y

    @pl.when(i == n_head)
    def _():
        tail_o[...] = y


def final_norm(x, g, *, tm):
    t, d = x.shape
    n_head = t // tm - 1
    assert t == (n_head + 1) * tm
    return pl.pallas_call(
        functools.partial(_final_norm_kernel, n_head=n_head),
        grid=(n_head + 1,),
        in_specs=[pl.BlockSpec((tm, d), lambda i: (i, 0)), pl.BlockSpec((1, d), lambda i: (0, 0))],
        out_specs=[pl.BlockSpec((tm, d), lambda i: (jnp.minimum(i, n_head - 1), 0)),
                   pl.BlockSpec((tm, d), lambda i: (0, 0))],
        out_shape=[jax.ShapeDtypeStruct((n_head * tm, d), F32),
                   jax.ShapeDtypeStruct((tm, d), F32)],
        compiler_params=_params("arbitrary"),
        name="final_norm",
    )(x, g.reshape(1, d))


def _to_time_major(s):
    return jnp.swapaxes(s, 0, 1)


def kernel(x_prompt, x_sample, mem_prompt, cache_mem_k, cache_mem_v, state_pool, state_sconv, state_cconv, g_mix, w_in, b_gate, pool_w, pool_scale, w_pool_out, sc_w, w_sc_out, cm_w, cm_b, cm_ln_g, cm_ln_b, w_cm_out, sg_ln_g, sg_ln_b, sg_w, sg_b, w_sg_out, w_o, g_x, g_mem, w_xq, w_xk, w_xv, w_xo, g_peer, w_pq, peer_keys, peer_u, peer_v, g_final):
    nb, seq, d = x_prompt.shape
    ns, td, _ = x_sample.shape
    depth = w_in.shape[0]
    mem_len = mem_prompt.shape[1]
    past_len = 16384
    t_prompt = nb * seq
    t_all = t_prompt + ns * td
    dx = X_HEADS * GROUP

    tm_big = _tile(t_all, 1088, 16)
    t_s = ns * td
    assert t_prompt % t_s == 0
    tn_in = _tile(w_in.shape[2], 1024, 128)
    tn_d = _tile(d, 512, 128)
    tn_gate = _tile(d, 256, 128)
    ts = _tile(seq, 256, CHUNK)
    tq = _tile(seq, 512, 8)
    tb = _tile(t_all, 512, LANES)
    tb_pre = _tile(t_all, 256, LANES)
    te = _tile(peer_u.shape[1], 1024, 8 * GROUP)

    x = jnp.concatenate([x_prompt.reshape(t_prompt, d),
                         _to_time_major(x_sample).reshape(ns * td, d)], axis=0)
    mem2d = mem_prompt.reshape(nb * mem_len, d)
    col1 = lambda v: v.reshape(1, -1)
    cache_k = cache_mem_k.reshape(depth, ns, mem_len * X_HEADS, GROUP)
    cache_v = cache_mem_v.reshape(depth, ns, mem_len * X_HEADS, GROUP)

    bf = lambda w: w.astype(BF16)
    w_in_b, w_o_b, w_xq_b, w_xk_b, w_xv_b, w_xo_b = map(bf, (w_in, w_o, w_xq, w_xk, w_xv, w_xo))
    w_branch_b = tuple(map(bf, (w_pool_out, w_sc_out, w_cm_out, w_sg_out)))
    w_pq_t = bf(jnp.swapaxes(w_pq, 1, 2))
    keys_b = bf(peer_keys.reshape(depth, 2 * PEER_HEADS, GROUP, GROUP))
    peer_u_b = bf(peer_u * math.sqrt(0.5))
    peer_vt = bf(jnp.swapaxes(peer_v.reshape(depth, -1, te, d), 2, 3))

    outs = {k: [] for k in ("mk", "mv", "pool_p", "sc_p", "cm_p", "pool_s", "sc_s", "cm_s", "cv_s")}
    for l in range(depth):
        lp = dict(
            pool_w=pool_w[l].astype(BF16), pool_scale=col1(pool_scale[l]), sc_w=sc_w[l],
            cm_w=cm_w[l], cm_b=col1(cm_b[l]), cm_ln_g=col1(cm_ln_g[l]), cm_ln_b=col1(cm_ln_b[l]),
            sg_ln_g=col1(sg_ln_g[l]), sg_ln_b=col1(sg_ln_b[l]), sg_w=sg_w[l],
            sg_bias=jnp.repeat(sg_b[l].T, GROUP, axis=1),
            sg_wts=jnp.repeat(jnp.transpose(sg_w[l][:, :td, :td], (1, 2, 0)), GROUP, axis=2))

        z, hn = norm_in(x, g_mix[l], w_in_b, l, 2 * N_BRANCH * D_BR, tm=tm_big, tn=tn_in)
        p_p, pool_p, sc_p, cm_p = mixer_prompt(z, lp, n_seq=nb, seq=seq, ts=ts)
        p_s, pool_s, sc_s, cm_s, vn_s = mixer_sample(
            z, _to_time_major(state_pool[l]), _to_time_major(state_sconv[l]),
            _to_time_major(state_cconv[l]), lp, td=td, ns=ns, t_prompt=t_prompt, start_pos=past_len)
        x = gate_out(hn, p_p, p_s, w_in_b, w_branch_b, b_gate[l], w_o_b, l, x, tn=tn_gate)

        q = norm_mm(x, g_x[l], w_xq_b, l, tm=tm_big, tn=dx)
        tm_mem = _tile(nb * mem_len, 1024, 16)
        k_p = norm_mm(mem2d, g_mem[l], w_xk_b, l, tm=tm_mem, tn=dx)
        v_p = norm_mm(mem2d, g_mem[l], w_xv_b, l, tm=tm_mem, tn=dx)
        o_p = attn_prompt(q, k_p.reshape(nb, mem_len, dx), v_p.reshape(nb, mem_len, dx),
                          n_seq=nb, seq=seq, tq=tq)
        q_s = jnp.swapaxes(q[t_prompt:].reshape(td, ns, dx), 0, 1).reshape(ns, td * X_HEADS, GROUP)
        o_s = attn_sample(q_s, cache_k, cache_v, l, sb=_tile(ns, 8, 1))
        o_s = jnp.swapaxes(o_s.reshape(ns, td, dx), 0, 1).reshape(t_s, dx)
        x, ht = attn_out(o_p, o_s, w_xo_b, l, x, g_peer[l])

        s2, theta, lam = peer_pre(ht, w_pq_t, keys_b, l, tb=tb_pre)
        x = peer_dense(ht, peer_u_b, peer_vt, l, s2, theta, lam, x, tb=tb, te=te)

        outs["mk"].append(k_p.reshape(nb, mem_len, X_HEADS, GROUP))
        outs["mv"].append(v_p.reshape(nb, mem_len, X_HEADS, GROUP))
        outs["pool_p"].append(pool_p)
        outs["sc_p"].append(sc_p)
        outs["cm_p"].append(cm_p)
        outs["pool_s"].append(_to_time_major(pool_s))
        outs["sc_s"].append(_to_time_major(sc_s))
        outs["cm_s"].append(_to_time_major(cm_s))
        outs["cv_s"].append(_to_time_major(vn_s))

    y_p, y_s = final_norm(x, g_final, tm=t_s)
    y_prompt = y_p.reshape(nb, seq, d)
    y_sample = _to_time_major(y_s.reshape(td, ns, d))
    st = lambda k: jnp.stack(outs[k])
    return (y_prompt, y_sample, st("mk"), st("mv"), st("pool_p"), st("sc_p"), st("cm_p"),
            st("pool_s"), st("sc_s"), st("cm_s"), st("cv_s"))
```

```python
import functools
import math

import jax
import jax.numpy as jnp
from jax import lax
from jax.experimental import pallas as pl
from jax.experimental.pallas import tpu as pltpu

EPS = 1e-6
GROUP = 128
POOL_WINDOWS = (2, 4, 8, 16)
POOL_STATE = max(POOL_WINDOWS) - 1
SC_WIDTH = 3
CM_WIDTH = 31
D_BR = 4 * GROUP
N_BRANCH = 4
CHUNK = 128
X_HEADS = 4
PEER_HEADS = 8
PEER_TOPK = 16
HALO = 32
LANES = 128
ROWS = 32
VMEM_LIMIT = 56 * 2 ** 20

BF16 = jnp.bfloat16
F32 = jnp.float32
NEG_INF = float("-inf")
LOG2E = math.log2(math.e)


def _tile(n, pref, mult=8):
    best = None
    for t in range(mult, min(n, pref) + 1, mult):
        if n % t == 0:
            best = t
    assert best is not None, (n, pref, mult)
    return best


def _params(*sem):
    return pltpu.CompilerParams(dimension_semantics=sem, vmem_limit_bytes=VMEM_LIMIT)


def _rms(x, g):
    ms = jnp.mean(x * x, axis=-1, keepdims=True)
    return x * lax.rsqrt(ms + EPS) * g


def _ln(x, g, b):
    mu = jnp.mean(x, axis=-1, keepdims=True)
    xc = x - mu
    var = jnp.mean(xc * xc, axis=-1, keepdims=True)
    return xc * lax.rsqrt(var + EPS) * g + b


def _sigmoid(x):
    return 1.0 / (1.0 + jnp.exp(-x))


def _dot(a, b):
    return jnp.dot(a, b, preferred_element_type=F32)


def _norm_mm_kernel(x_ref, g_ref, w_ref, o_ref, hn_ref):
    @pl.when(pl.program_id(1) == 0)
    def _():
        hn_ref[...] = _rms(x_ref[...], g_ref[...]).astype(BF16)

    o_ref[...] = _dot(hn_ref[...], w_ref[...])


def norm_mm(x, g, w, l, *, tm, tn):
    t, d = x.shape
    n = w.shape[2]
    return pl.pallas_call(
        _norm_mm_kernel,
        grid=(t // tm, n // tn),
        in_specs=[pl.BlockSpec((tm, d), lambda i, j: (i, 0)),
                  pl.BlockSpec((1, d), lambda i, j: (0, 0)),
                  pl.BlockSpec((None, d, tn), lambda i, j: (l, 0, j))],
        out_specs=pl.BlockSpec((tm, tn), lambda i, j: (i, j)),
        out_shape=jax.ShapeDtypeStruct((t, n), F32),
        scratch_shapes=[pltpu.VMEM((tm, d), BF16)],
        compiler_params=_params("parallel", "arbitrary"),
        name="norm_mm",
    )(x, g.reshape(1, d), w)


def _rows_of(i, n_head, head_ref, tail_ref):
    return jnp.where(i < n_head, head_ref[...], tail_ref[...])


def _attn_out_kernel(a_ref, at_ref, w_ref, r_ref, g_ref, o_ref, ht_ref, *, n_head):
    a = _rows_of(pl.program_id(0), n_head, a_ref, at_ref)
    x = r_ref[...] + _dot(a.astype(BF16), w_ref[...])
    o_ref[...] = x
    ht_ref[...] = _rms(x, g_ref[...]).T.astype(BF16)


def attn_out(a, a_tail, w, l, res, g):
    t, d = res.shape
    tm, k = a_tail.shape
    n_head = a.shape[0] // tm
    assert a.shape[0] == n_head * tm and t == (n_head + 1) * tm and tm % LANES == 0
    row = pl.BlockSpec((tm, d), lambda i: (i, 0))
    return pl.pallas_call(
        functools.partial(_attn_out_kernel, n_head=n_head),
        grid=(t // tm,),
        in_specs=[pl.BlockSpec((tm, k), lambda i: (jnp.minimum(i, n_head - 1), 0)),
                  pl.BlockSpec((tm, k), lambda i: (0, 0)),
                  pl.BlockSpec((None, k, d), lambda i: (l, 0, 0)),
                  row, pl.BlockSpec((1, d), lambda i: (0, 0))],
        out_specs=[row, pl.BlockSpec((d, tm), lambda i: (0, i))],
        out_shape=[jax.ShapeDtypeStruct((t, d), F32), jax.ShapeDtypeStruct((d, t), BF16)],
        compiler_params=_params("parallel"),
        name="attn_out",
    )(a, a_tail, w, res, g.reshape(1, d))


def _pool_branch(window_sums, a, cnts, poolw_ref, pscale):
    outs = []
    for gi in range(len(POOL_WINDOWS)):
        sl = slice(gi * GROUP, (gi + 1) * GROUP)
        pg = window_sums[gi] / cnts[gi] - a[:, sl]
        outs.append(_dot(pg.astype(BF16), poolw_ref[gi]))
    return jnp.concatenate(outs, axis=1) * pscale


def _mixer_prompt_kernel(z_ref, poolw_ref, pscale_ref, scw_ref, cmw_ref, cmb_ref, cmg_ref,
                         cmbeta_ref, sgg_ref, sgbeta_ref, sgw_ref, sgbias_ref,
                         p_ref, pool_o, sc_o, cm_o, ext_ref, *, ts):
    tb = pl.program_id(1)
    h0 = HALO

    @pl.when(tb == 0)
    def _():
        ext_ref[:, 0:h0, :] = jnp.zeros((3, h0, D_BR), F32)

    a = z_ref[:, 0:D_BR]
    bg = z_ref[:, D_BR:2 * D_BR]
    gated = z_ref[:, 2 * D_BR:3 * D_BR] * z_ref[:, 3 * D_BR:4 * D_BR]
    glu = z_ref[:, 4 * D_BR:5 * D_BR] * _sigmoid(z_ref[:, 5 * D_BR:6 * D_BR])
    ext_ref[0, h0:h0 + ts, :] = a
    ext_ref[1, h0:h0 + ts, :] = gated
    ext_ref[2, h0:h0 + ts, :] = glu

    pos = tb * ts + lax.broadcasted_iota(jnp.int32, (ts, 1), 0)
    sums, cnts = [], []
    for gi, w in enumerate(POOL_WINDOWS):
        sl = slice(gi * GROUP, (gi + 1) * GROUP)
        s = a[:, sl]
        for j in range(1, w):
            s = s + ext_ref[0, h0 - j:h0 - j + ts, sl]
        sums.append(s)
        cnts.append(jnp.minimum(w, pos + 1).astype(F32))
    p_ref[:, 0:D_BR] = _pool_branch(sums, a, cnts, poolw_ref, pscale_ref[...]).astype(BF16)

    conv = scw_ref[SC_WIDTH - 1:SC_WIDTH, :] * gated
    for k in range(SC_WIDTH - 1):
        off = h0 - (SC_WIDTH - 1) + k
        conv = conv + scw_ref[k:k + 1, :] * ext_ref[1, off:off + ts, :]
    p_ref[:, D_BR:2 * D_BR] = (bg * conv).astype(BF16)

    c = cmw_ref[CM_WIDTH - 1:CM_WIDTH, :] * glu
    for k in range(CM_WIDTH - 1):
        off = h0 - (CM_WIDTH - 1) + k
        c = c + cmw_ref[k:k + 1, :] * ext_ref[2, off:off + ts, :]
    c = _ln(c + cmb_ref[...], cmg_ref[...], cmbeta_ref[...])
    p_ref[:, 2 * D_BR:3 * D_BR] = (c * _sigmoid(c)).astype(BF16)

    vn = _ln(z_ref[:, 7 * D_BR:8 * D_BR], sgg_ref[...], sgbeta_ref[...])
    row = lax.broadcasted_iota(jnp.int32, (CHUNK, CHUNK), 0)
    col = lax.broadcasted_iota(jnp.int32, (CHUNK, CHUNK), 1)
    tril = row >= col
    for gi in range(D_BR // GROUP):
        sl = slice(gi * GROUP, (gi + 1) * GROUP)
        wg = jnp.where(tril, sgw_ref[gi], 0.0).astype(BF16)
        for ci in range(ts // CHUNK):
            rows = slice(ci * CHUNK, (ci + 1) * CHUNK)
            mixed = _dot(wg, vn[rows, sl].astype(BF16)) + sgbias_ref[:, sl]
            p_ref[rows, 3 * D_BR + gi * GROUP:3 * D_BR + (gi + 1) * GROUP] = (
                z_ref[rows, 6 * D_BR + gi * GROUP:6 * D_BR + (gi + 1) * GROUP] * mixed).astype(BF16)

    end = h0 + ts
    pool_o[0] = ext_ref[0, end - POOL_STATE:end, :]
    sc_o[0] = ext_ref[1, end - (SC_WIDTH - 1):end, :]
    cm_o[0] = ext_ref[2, end - (CM_WIDTH - 1):end, :]
    ext_ref[:, 0:h0, :] = ext_ref[:, ts:ts + h0, :]


def mixer_prompt(z, lp, *, n_seq, seq, ts):
    nb = seq // ts
    full = lambda *shape: pl.BlockSpec(shape, lambda n, t: (0,) * len(shape))
    state = lambda rows: pl.BlockSpec((1, rows, D_BR), lambda n, t: (n, 0, 0))
    return pl.pallas_call(
        functools.partial(_mixer_prompt_kernel, ts=ts),
        grid=(n_seq, nb),
        in_specs=[pl.BlockSpec((ts, 8 * D_BR), lambda n, t: (n * nb + t, 0)),
                  full(4, GROUP, GROUP), full(1, D_BR), full(SC_WIDTH, D_BR),
                  full(CM_WIDTH, D_BR), full(1, D_BR), full(1, D_BR), full(1, D_BR),
                  full(1, D_BR), full(1, D_BR), full(4, CHUNK, CHUNK), full(CHUNK, D_BR)],
        out_specs=[pl.BlockSpec((ts, N_BRANCH * D_BR), lambda n, t: (n * nb + t, 0)),
                   state(POOL_STATE), state(SC_WIDTH - 1), state(CM_WIDTH - 1)],
        out_shape=[jax.ShapeDtypeStruct((n_seq * seq, N_BRANCH * D_BR), BF16),
                   jax.ShapeDtypeStruct((n_seq, POOL_STATE, D_BR), F32),
                   jax.ShapeDtypeStruct((n_seq, SC_WIDTH - 1, D_BR), F32),
                   jax.ShapeDtypeStruct((n_seq, CM_WIDTH - 1, D_BR), F32)],
        scratch_shapes=[pltpu.VMEM((3, HALO + ts, D_BR), F32)],
        compiler_params=_params("parallel", "arbitrary"),
        name="mixer_prompt",
    )(z, lp["pool_w"], lp["pool_scale"], lp["sc_w"], lp["cm_w"], lp["cm_b"], lp["cm_ln_g"],
      lp["cm_ln_b"], lp["sg_ln_g"], lp["sg_ln_b"], lp["sg_w"], lp["sg_bias"])


def _mixer_sample_kernel(z_ref, pool_ref, sc_ref, cm_ref, poolw_ref, pscale_ref, scw_ref, cmw_ref,
                         cmb_ref, cmg_ref, cmbeta_ref, sgg_ref, sgbeta_ref, sgwts_ref, sgbias_ref,
                         p_ref, pool_o, sc_o, cm_o, vn_o, *, td, ns, start_pos):
    rows = lambda t: slice(t * ns, (t + 1) * ns)
    col = lambda k: slice(k * D_BR, (k + 1) * D_BR)

    ext_a = [pool_ref[j] for j in range(POOL_STATE)] + [z_ref[rows(t), col(0)] for t in range(td)]
    a_all = jnp.concatenate(ext_a[POOL_STATE:], axis=0)
    sums, cnts = [], []
    for gi, w in enumerate(POOL_WINDOWS):
        sl = slice(gi * GROUP, (gi + 1) * GROUP)
        per_t = []
        for t in range(td):
            s = ext_a[POOL_STATE + t][:, sl]
            for j in range(1, w):
                s = s + ext_a[POOL_STATE + t - j][:, sl]
            per_t.append(s)
        sums.append(jnp.concatenate(per_t, axis=0))
        cnt = jnp.concatenate(
            [jnp.full((ns, 1), float(min(w, start_pos + t + 1)), F32) for t in range(td)], axis=0)
        cnts.append(cnt)
    p_ref[:, col(0)] = _pool_branch(sums, a_all, cnts, poolw_ref, pscale_ref[...]).astype(BF16)
    for j in range(POOL_STATE):
        pool_o[j] = ext_a[td + j]

    ext_b = [sc_ref[j] for j in range(SC_WIDTH - 1)]
    ext_b += [z_ref[rows(t), col(2)] * z_ref[rows(t), col(3)] for t in range(td)]
    for t in range(td):
        conv = scw_ref[0:1, :] * ext_b[t]
        for k in range(1, SC_WIDTH):
            conv = conv + scw_ref[k:k + 1, :] * ext_b[t + k]
        p_ref[rows(t), col(1)] = (z_ref[rows(t), col(1)] * conv).astype(BF16)
    for j in range(SC_WIDTH - 1):
        sc_o[j] = ext_b[td + j]

    ext_c = [cm_ref[j] for j in range(CM_WIDTH - 1)]
    ext_c += [z_ref[rows(t), col(4)] * _sigmoid(z_ref[rows(t), col(5)]) for t in range(td)]
    for t in range(td):
        c = cmw_ref[0:1, :] * ext_c[t]
        for k in range(1, CM_WIDTH):
            c = c + cmw_ref[k:k + 1, :] * ext_c[t + k]
        c = _ln(c + cmb_ref[...], cmg_ref[...], cmbeta_ref[...])
        p_ref[rows(t), col(2)] = (c * _sigmoid(c)).astype(BF16)
    for j in range(CM_WIDTH - 1):
        cm_o[j] = ext_c[td + j]

    vn = [_ln(z_ref[rows(t), col(7)], sgg_ref[...], sgbeta_ref[...]) for t in range(td)]
    for t in range(td):
        vn_o[t] = vn[t]
        mixed = sgbias_ref[t:t + 1, :]
        for s in range(t + 1):
            mixed = mixed + sgwts_ref[t, s:s + 1, :] * vn[s]
        p_ref[rows(t), col(3)] = (z_ref[rows(t), col(6)] * mixed).astype(BF16)


def mixer_sample(z, pool_tm, sc_tm, cm_tm, lp, *, td, ns, t_prompt, start_pos):
    blk = t_prompt // (td * ns)
    full = lambda *shape: pl.BlockSpec(shape, lambda i: (0,) * len(shape))
    tm_shape = lambda rows: jax.ShapeDtypeStruct((rows, ns, D_BR), F32)
    return pl.pallas_call(
        functools.partial(_mixer_sample_kernel, td=td, ns=ns, start_pos=start_pos),
        grid=(1,),
        in_specs=[pl.BlockSpec((td * ns, 8 * D_BR), lambda i: (blk, 0)),
                  full(POOL_STATE, ns, D_BR), full(SC_WIDTH - 1, ns, D_BR),
                  full(CM_WIDTH - 1, ns, D_BR),
                  full(4, GROUP, GROUP), full(1, D_BR), full(SC_WIDTH, D_BR),
                  full(CM_WIDTH, D_BR), full(1, D_BR), full(1, D_BR), full(1, D_BR),
                  full(1, D_BR), full(1, D_BR), full(td, td, D_BR), full(CHUNK, D_BR)],
        out_specs=[full(td * ns, N_BRANCH * D_BR),
                   full(POOL_STATE, ns, D_BR), full(SC_WIDTH - 1, ns, D_BR),
                   full(CM_WIDTH - 1, ns, D_BR), full(td, ns, D_BR)],
        out_shape=[jax.ShapeDtypeStruct((td * ns, N_BRANCH * D_BR), BF16), tm_shape(POOL_STATE),
                   tm_shape(SC_WIDTH - 1), tm_shape(CM_WIDTH - 1), tm_shape(td)],
        compiler_params=_params("arbitrary"),
        name="mixer_sample",
    )(z, pool_tm, sc_tm, cm_tm, lp["pool_w"], lp["pool_scale"], lp["sc_w"], lp["cm_w"],
      lp["cm_b"], lp["cm_ln_g"], lp["cm_ln_b"], lp["sg_ln_g"], lp["sg_ln_b"], lp["sg_wts"],
      lp["sg_bias"])


def _norm_in_kernel(x_ref, g_ref, w_ref, z_ref, hn_ref):
    @pl.when(pl.program_id(1) == 0)
    def _():
        hn_ref[...] = _rms(x_ref[...], g_ref[...]).astype(BF16)

    z_ref[...] = _dot(hn_ref[...], w_ref[...])


def norm_in(x, g, w, l, n_cols, *, tm, tn):
    t, d = x.shape
    return pl.pallas_call(
        _norm_in_kernel,
        grid=(t // tm, n_cols // tn),
        in_specs=[pl.BlockSpec((tm, d), lambda i, j: (i, 0)),
                  pl.BlockSpec((1, d), lambda i, j: (0, 0)),
                  pl.BlockSpec((None, d, tn), lambda i, j: (l, 0, j))],
        out_specs=[pl.BlockSpec((tm, tn), lambda i, j: (i, j)),
                   pl.BlockSpec((tm, d), lambda i, j: (i, 0))],
        out_shape=[jax.ShapeDtypeStruct((t, n_cols), F32), jax.ShapeDtypeStruct((t, d), BF16)],
        compiler_params=_params("parallel", "arbitrary"),
        name="norm_in",
    )(x, g.reshape(1, d), w)


def _gate_out_kernel(hn_ref, p_ref, pt_ref, g0_ref, g1_ref, g2_ref, g3_ref, w0_ref, w1_ref,
                     w2_ref, w3_ref, b_ref, wo_ref, x_ref, o_ref, merged_ref, *, n_head, tn):
    j = pl.program_id(1)
    p = _rows_of(pl.program_id(0), n_head, p_ref, pt_ref)
    hn = hn_ref[...]
    acc = None
    for i, (g_ref, w_ref) in enumerate(((g0_ref, w0_ref), (g1_ref, w1_ref), (g2_ref, w2_ref),
                                        (g3_ref, w3_ref))):
        gate = _sigmoid(_dot(hn, g_ref[...]) + b_ref[i:i + 1, :])
        term = gate * _dot(p[:, i * D_BR:(i + 1) * D_BR].astype(BF16), w_ref[...])
        acc = term if acc is None else acc + term
    merged_ref[:, pl.ds(pl.multiple_of(j * tn, tn), tn)] = acc.astype(BF16)

    @pl.when(j == pl.num_programs(1) - 1)
    def _():
        o_ref[...] = x_ref[...] + _dot(merged_ref[...], wo_ref[...])


def gate_out(hn, p_head, p_tail, w_in, w_outs, b_gate, w_o, l, x, *, tn):
    t, d = x.shape
    tm = p_tail.shape[0]
    n_head = p_head.shape[0] // tm
    assert p_head.shape[0] == n_head * tm and t == (n_head + 1) * tm
    first = N_BRANCH * 2 * D_BR // tn
    gate_spec = lambda i: pl.BlockSpec((None, d, tn), lambda r, c: (l, 0, first + i * (d // tn) + c))
    w_spec = pl.BlockSpec((None, D_BR, tn), lambda r, c: (l, 0, c))
    row = pl.BlockSpec((tm, d), lambda r, c: (r, 0))
    return pl.pallas_call(
        functools.partial(_gate_out_kernel, n_head=n_head, tn=tn),
        grid=(t // tm, d // tn),
        in_specs=[row,
                  pl.BlockSpec((tm, N_BRANCH * D_BR), lambda r, c: (jnp.minimum(r, n_head - 1), 0)),
                  pl.BlockSpec((tm, N_BRANCH * D_BR), lambda r, c: (0, 0)),
                  gate_spec(0), gate_spec(1), gate_spec(2), gate_spec(3),
                  w_spec, w_spec, w_spec, w_spec,
                  pl.BlockSpec((N_BRANCH, tn), lambda r, c: (0, c)),
                  pl.BlockSpec((None, d, d), lambda r, c: (l, 0, 0), pipeline_mode=pl.Buffered(1)),
                  row],
        out_specs=row,
        out_shape=jax.ShapeDtypeStruct((t, d), F32),
        scratch_shapes=[pltpu.VMEM((tm, d), BF16)],
        compiler_params=_params("parallel", "arbitrary"),
        name="gate_out",
    )(hn, p_head, p_tail, w_in, w_in, w_in, w_in, *w_outs, b_gate, w_o, x)


def _softmax_rows(s):
    m = jnp.max(s, axis=-1, keepdims=True)
    p = jnp.exp(s - m)
    return p / jnp.sum(p, axis=-1, keepdims=True)


def _attn_prompt_kernel(q_ref, k_ref, v_ref, o_ref, *, scale):
    for h in range(X_HEADS):
        sl = slice(h * GROUP, (h + 1) * GROUP)
        s = lax.dot_general(q_ref[:, sl].astype(BF16), k_ref[0, :, sl].astype(BF16),
                            (((1,), (1,)), ((), ())), preferred_element_type=F32) * scale
        o_ref[:, sl] = _dot(_softmax_rows(s).astype(BF16), v_ref[0, :, sl].astype(BF16))


def attn_prompt(q, k, v, *, n_seq, seq, tq):
    dx = q.shape[1]
    mem = k.shape[1]
    nb = seq // tq
    return pl.pallas_call(
        functools.partial(_attn_prompt_kernel, scale=GROUP ** -0.5),
        grid=(n_seq, nb),
        in_specs=[pl.BlockSpec((tq, dx), lambda n, t: (n * nb + t, 0)),
                  pl.BlockSpec((1, mem, dx), lambda n, t: (n, 0, 0)),
                  pl.BlockSpec((1, mem, dx), lambda n, t: (n, 0, 0))],
        out_specs=pl.BlockSpec((tq, dx), lambda n, t: (n * nb + t, 0)),
        out_shape=jax.ShapeDtypeStruct((n_seq * seq, dx), F32),
        compiler_params=_params("parallel", "parallel"),
        name="attn_prompt",
    )(q, k, v)


def _attn_sample_kernel(q_ref, k_ref, v_ref, o_ref, *, scale):
    rows, cols = q_ref.shape[1], k_ref.shape[1]
    row_head = lax.broadcasted_iota(jnp.int32, (rows, cols), 0) % X_HEADS
    col_head = lax.broadcasted_iota(jnp.int32, (rows, cols), 1) % X_HEADS
    s = jnp.einsum("bqd,bkd->bqk", q_ref[...].astype(BF16), k_ref[...].astype(BF16),
                   preferred_element_type=F32) * scale
    s = jnp.where((row_head == col_head)[None], s, NEG_INF)
    o_ref[...] = jnp.einsum("bqk,bkd->bqd", _softmax_rows(s).astype(BF16),
                            v_ref[...].astype(BF16), preferred_element_type=F32)


def attn_sample(q, k, v, l, *, sb):
    n_seq, rows, hd = q.shape
    kv_spec = pl.BlockSpec((None, sb, k.shape[2], hd), lambda i: (l, i, 0, 0))
    return pl.pallas_call(
        functools.partial(_attn_sample_kernel, scale=GROUP ** -0.5),
        grid=(n_seq // sb,),
        in_specs=[pl.BlockSpec((sb, rows, hd), lambda i: (i, 0, 0)), kv_spec, kv_spec],
        out_specs=pl.BlockSpec((sb, rows, hd), lambda i: (i, 0, 0)),
        out_shape=jax.ShapeDtypeStruct((n_seq, rows, hd), F32),
        compiler_params=_params("parallel"),
        name="attn_sample",
    )(q, k, v)


def _merge_exchange_network(n):
    pairs, p = [], 1
    while p < n:
        k = p
        while k >= 1:
            for j in range(k % p, n - k, 2 * k):
                for i in range(min(k, n - j - k)):
                    if (i + j) // (2 * p) == (i + j + k) // (2 * p):
                        pairs.append((i + j, i + j + k))
            k //= 2
        p *= 2
    return pairs


def _sort_descending(values):
    n = pl.next_power_of_2(len(values))
    vals = list(values) + [None] * (n - len(values))
    for i, j in _merge_exchange_network(n):
        hi, lo = vals[i], vals[j]
        if lo is None:
            continue
        if hi is None:
            vals[i], vals[j] = lo, None
        else:
            vals[i], vals[j] = jnp.maximum(hi, lo), jnp.minimum(hi, lo)
    return vals[:len(values)]


_CAND = [(i, j) for i in range(PEER_TOPK + 1) for j in range(PEER_TOPK + 1)
         if (i + 1) * (j + 1) <= PEER_TOPK + 1]


def _peer_pre_kernel(ht_ref, wq_ref, keys_ref, s2_o, th_o, lam_o, q_scr, s_scr, v_scr, *, tb):
    nk = GROUP
    q_scr[...] = _dot(wq_ref[...], ht_ref[...]).astype(BF16)
    for hp in range(2 * PEER_HEADS):
        s_scr[hp] = _dot(keys_ref[hp], q_scr[hp * nk:(hp + 1) * nk, :])

    idx = lax.broadcasted_iota(jnp.int32, (nk, LANES), 0)

    def chunk(ci, carry):
        lanes = pl.ds(pl.multiple_of(ci * LANES, LANES), LANES)
        tied = jnp.zeros((1, LANES), F32)
        for hp in range(2 * PEER_HEADS):
            h, p = divmod(hp, 2)
            cur = s_scr[hp, :, lanes]
            m = jnp.max(cur, axis=0, keepdims=True)
            v_scr[p, 0, h:h + 1, lanes] = m
            for r in range(1, PEER_TOPK + 1):
                m = jnp.max(jnp.where(cur < m, cur, NEG_INF), axis=0, keepdims=True)
                v_scr[p, r, h:h + 1, lanes] = m
            n_ge = jnp.sum(jnp.where(cur >= m, 1.0, 0.0), axis=0, keepdims=True)
            tied = jnp.maximum(tied, jnp.where(n_ge != PEER_TOPK + 1.0, 1.0, 0.0))

        @pl.when(jnp.max(tied) > 0.0)
        def _():
            for hp in range(2 * PEER_HEADS):
                h, p = divmod(hp, 2)
                cur = s_scr[hp, :, lanes]
                for r in range(PEER_TOPK + 1):
                    m = jnp.max(cur, axis=0, keepdims=True)
                    v_scr[p, r, h:h + 1, lanes] = m
                    if r < PEER_TOPK:
                        first = jnp.min(jnp.where(cur == m, idx, nk), axis=0, keepdims=True)
                        cur = jnp.where(idx == first, NEG_INF, cur)

        v1 = [v_scr[0, r, :, lanes] for r in range(PEER_TOPK + 1)]
        v2 = [v_scr[1, r, :, lanes] for r in range(PEER_TOPK + 1)]
        cands = [v1[i] + v2[j] for i, j in _CAND]
        ranked = _sort_descending(cands)
        kth, nxt = ranked[PEER_TOPK - 1], ranked[PEER_TOPK]
        tau = 0.5 * (kth + nxt)
        top = cands[0]
        zsum = jnp.zeros_like(top)
        for ck in cands:
            zsum = zsum + jnp.where(ck >= tau, jnp.exp(ck - top), 0.0)
        shift = top + jnp.log(zsum)
        for h in range(PEER_HEADS):
            s1 = s_scr[2 * h, :, lanes]
            th_o[h, :, lanes] = (tau[h:h + 1, :] - s1) * LOG2E
            lam_o[h, :, lanes] = (s1 - shift[h:h + 1, :]) * LOG2E - 0.5
            s2_o[h * nk:(h + 1) * nk, lanes] = s_scr[2 * h + 1, :, lanes] * LOG2E
        return carry

    lax.fori_loop(0, tb // LANES, chunk, 0)


def peer_pre(ht, wq_t, keys, l, *, tb):
    d, t = ht.shape
    nq = wq_t.shape[1]
    hk = PEER_HEADS * GROUP
    flat = pl.BlockSpec((hk, tb), lambda i: (0, i))
    cube = pl.BlockSpec((PEER_HEADS, GROUP, tb), lambda i: (0, 0, i))
    return pl.pallas_call(
        functools.partial(_peer_pre_kernel, tb=tb),
        grid=(t // tb,),
        in_specs=[pl.BlockSpec((d, tb), lambda i: (0, i)),
                  pl.BlockSpec((None, nq, d), lambda i: (l, 0, 0)),
                  pl.BlockSpec((None, 2 * PEER_HEADS, GROUP, GROUP), lambda i: (l, 0, 0, 0))],
        out_specs=[flat, cube, cube],
        out_shape=[jax.ShapeDtypeStruct((hk, t), F32),
                   jax.ShapeDtypeStruct((PEER_HEADS, GROUP, t), F32),
                   jax.ShapeDtypeStruct((PEER_HEADS, GROUP, t), F32)],
        scratch_shapes=[pltpu.VMEM((nq, tb), BF16),
                        pltpu.VMEM((2 * PEER_HEADS, GROUP, tb), F32),
                        pltpu.VMEM((2, PEER_TOPK + 1, PEER_HEADS, tb), F32)],
        compiler_params=_params("parallel"),
        name="peer_pre",
    )(ht, wq_t, keys)


def _peer_dense_kernel(ht_ref, htn_ref, ua_ref, ub_ref, vt_ref, s2_ref, th_ref, lam_ref, x_ref,
                       o_ref, acc_ref, at_ref, gw_ref, *, tb, te, ne):
    s = pl.program_id(0)
    e_cur = jnp.maximum(s - 1, 0) % ne
    half = te // 2
    tbh = tb // 2
    n_a = half // GROUP
    piece = lambda k, c: 2 * k + c
    cols = lambda c: slice(c * tbh, (c + 1) * tbh)

    @pl.when(s == 0)
    def _():
        at_ref[0:2] = jnp.zeros((2, half, tbh), F32)

    @pl.when(e_cur == 0)
    def _():
        acc_ref[...] = jnp.zeros_like(acc_ref)

    def pre_activations(k, c, u_ref, h_ref):
        at_ref[piece(k, c)] = _dot(u_ref[...], h_ref[:, cols(c)])

    def weighted_activations(k, c):
        p = piece(k, c)
        for cl in range(tbh // LANES):
            lanes = slice(cl * LANES, (cl + 1) * LANES)
            glanes = slice(c * tbh + cl * LANES, c * tbh + (cl + 1) * LANES)
            for r in range(GROUP // ROWS):
                w = [jnp.zeros((ROWS, LANES), F32) for _ in range(n_a)]
                for h in range(PEER_HEADS):
                    s2 = s2_ref[h * GROUP + r * ROWS:h * GROUP + (r + 1) * ROWS, glanes]
                    for a in range(n_a):
                        ag = k * n_a + a
                        keep = s2 >= th_ref[h, ag:ag + 1, glanes]
                        w[a] = w[a] + jnp.where(keep, jnp.exp2(s2 + lam_ref[h, ag:ag + 1, glanes]), 0.0)
                for a in range(n_a):
                    rows = slice(a * GROUP + r * ROWS, a * GROUP + (r + 1) * ROWS)
                    z = at_ref[p, rows, lanes]
                    gw_ref[p, rows, lanes] = (w[a] * (z * (1.0 + lax.erf(z)))).astype(BF16)

    def accumulate(k, c):
        acc_ref[:, cols(c)] += _dot(vt_ref[:, k * half:(k + 1) * half], gw_ref[piece(k, c)])

    pre_activations(1, 0, ub_ref, ht_ref)
    pre_activations(1, 1, ub_ref, ht_ref)
    for c in range(2):
        weighted_activations(0, c)
        accumulate(0, c)
        pre_activations(0, c, ua_ref, htn_ref)
    for c in range(2):
        weighted_activations(1, c)
        accumulate(1, c)

    @pl.when(jnp.logical_and(e_cur == ne - 1, s > 0))
    def _():
        o_ref[...] = x_ref[...] + acc_ref[...].T


def peer_dense(ht, u, vt, l, s2, theta, lam, x, *, tb, te):
    d, t = ht.shape
    ne = u.shape[1] // te
    assert vt.shape[1:] == (ne, d, te)
    n_tiles = (t // tb) * ne
    half = te // 2
    a_blk = te // GROUP
    once = pl.Buffered(1)
    cur = lambda s: jnp.clip(s - 1, 0, n_tiles - 1)
    nxt = lambda s: jnp.minimum(s, n_tiles - 1)
    return pl.pallas_call(
        functools.partial(_peer_dense_kernel, tb=tb, te=te, ne=ne),
        grid=(n_tiles + 1,),
        in_specs=[pl.BlockSpec((d, tb), lambda s: (0, cur(s) // ne), pipeline_mode=once),
                  pl.BlockSpec((d, tb), lambda s: (0, nxt(s) // ne), pipeline_mode=once),
                  pl.BlockSpec((None, half, d), lambda s: (l, 2 * (nxt(s) % ne), 0)),
                  pl.BlockSpec((None, half, d), lambda s: (l, 2 * (cur(s) % ne) + 1, 0)),
                  pl.BlockSpec((None, None, d, te), lambda s: (l, cur(s) % ne, 0, 0)),
                  pl.BlockSpec((PEER_HEADS * GROUP, tb), lambda s: (0, cur(s) // ne),
                               pipeline_mode=once),
                  pl.BlockSpec((PEER_HEADS, a_blk, tb), lambda s: (0, cur(s) % ne, cur(s) // ne)),
                  pl.BlockSpec((PEER_HEADS, a_blk, tb), lambda s: (0, cur(s) % ne, cur(s) // ne)),
                  pl.BlockSpec((tb, d), lambda s: (cur(s) // ne, 0), pipeline_mode=once)],
        out_specs=pl.BlockSpec((tb, d), lambda s: (cur(s) // ne, 0)),
        out_shape=jax.ShapeDtypeStruct((t, d), F32),
        scratch_shapes=[pltpu.VMEM((d, tb), F32), pltpu.VMEM((4, half, tb // 2), F32),
                        pltpu.VMEM((4, half, tb // 2), BF16)],
        compiler_params=_params("arbitrary"),
        name="peer_dense",
    )(ht, ht, u, u, vt, s2, theta, lam, x)


def _final_norm_kernel(x_ref, g_ref, head_o, tail_o, *, n_head):
    i = pl.program_id(0)
    y = _rms(x_ref[...], g_ref[...])

    @pl.when(i < n_head)
    def _():
        head_o[...] = y

    @pl.when(i == n_head)
    def _():
        tail_o[...] = y


def final_norm(x, g, *, tm):
    t, d = x.shape
    n_head = t // tm - 1
    assert t == (n_head + 1) * tm
    return pl.pallas_call(
        functools.partial(_final_norm_kernel, n_head=n_head),
        grid=(n_head + 1,),
        in_specs=[pl.BlockSpec((tm, d), lambda i: (i, 0)), pl.BlockSpec((1, d), lambda i: (0, 0))],
        out_specs=[pl.BlockSpec((tm, d), lambda i: (jnp.minimum(i, n_head - 1), 0)),
                   pl.BlockSpec((tm, d), lambda i: (0, 0))],
        out_shape=[jax.ShapeDtypeStruct((n_head * tm, d), F32),
                   jax.ShapeDtypeStruct((tm, d), F32)],
        compiler_params=_params("arbitrary"),
        name="final_norm",
    )(x, g.reshape(1, d))


def _to_time_major(s):
    return jnp.swapaxes(s, 0, 1)


def kernel(x_prompt, x_sample, mem_prompt, cache_mem_k, cache_mem_v, state_pool, state_sconv, state_cconv, g_mix, w_in, b_gate, pool_w, pool_scale, w_pool_out, sc_w, w_sc_out, cm_w, cm_b, cm_ln_g, cm_ln_b, w_cm_out, sg_ln_g, sg_ln_b, sg_w, sg_b, w_sg_out, w_o, g_x, g_mem, w_xq, w_xk, w_xv, w_xo, g_peer, w_pq, peer_keys, peer_u, peer_v, g_final):
    nb, seq, d = x_prompt.shape
    ns, td, _ = x_sample.shape
    depth = w_in.shape[0]
    mem_len = mem_prompt.shape[1]
    past_len = 16384
    t_prompt = nb * seq
    t_all = t_prompt + ns * td
    dx = X_HEADS * GROUP

    tm_big = _tile(t_all, 1088, 16)
    t_s = ns * td
    assert t_prompt % t_s == 0
    tn_in = _tile(w_in.shape[2], 1024, 128)
    tn_d = _tile(d, 512, 128)
    tn_gate = _tile(d, 256, 128)
    ts = _tile(seq, 256, CHUNK)
    tq = _tile(seq, 512, 8)
    tb = _tile(t_all, 512, LANES)
    tb_pre = _tile(t_all, 256, LANES)
    te = _tile(peer_u.shape[1], 1024, 8 * GROUP)

    x = jnp.concatenate([x_prompt.reshape(t_prompt, d),
                         _to_time_major(x_sample).reshape(ns * td, d)], axis=0)
    mem2d = mem_prompt.reshape(nb * mem_len, d)
    col1 = lambda v: v.reshape(1, -1)
    cache_k = cache_mem_k.reshape(depth, ns, mem_len * X_HEADS, GROUP)
    cache_v = cache_mem_v.reshape(depth, ns, mem_len * X_HEADS, GROUP)

    bf = lambda w: w.astype(BF16)
    w_in_b, w_o_b, w_xq_b, w_xk_b, w_xv_b, w_xo_b = map(bf, (w_in, w_o, w_xq, w_xk, w_xv, w_xo))
    w_branch_b = tuple(map(bf, (w_pool_out, w_sc_out, w_cm_out, w_sg_out)))
    w_pq_t = bf(jnp.swapaxes(w_pq, 1, 2))
    keys_b = bf(peer_keys.reshape(depth, 2 * PEER_HEADS, GROUP, GROUP))
    peer_u_b = bf(peer_u * math.sqrt(0.5))
    peer_vt = bf(jnp.swapaxes(peer_v.reshape(depth, -1, te, d), 2, 3))

    outs = {k: [] for k in ("mk", "mv", "pool_p", "sc_p", "cm_p", "pool_s", "sc_s", "cm_s", "cv_s")}
    for l in range(depth):
        lp = dict(
            pool_w=pool_w[l].astype(BF16), pool_scale=col1(pool_scale[l]), sc_w=sc_w[l],
            cm_w=cm_w[l], cm_b=col1(cm_b[l]), cm_ln_g=col1(cm_ln_g[l]), cm_ln_b=col1(cm_ln_b[l]),
            sg_ln_g=col1(sg_ln_g[l]), sg_ln_b=col1(sg_ln_b[l]), sg_w=sg_w[l],
            sg_bias=jnp.repeat(sg_b[l].T, GROUP, axis=1),
            sg_wts=jnp.repeat(jnp.transpose(sg_w[l][:, :td, :td], (1, 2, 0)), GROUP, axis=2))

        z, hn = norm_in(x, g_mix[l], w_in_b, l, 2 * N_BRANCH * D_BR, tm=tm_big, tn=tn_in)
        p_p, pool_p, sc_p, cm_p = mixer_prompt(z, lp, n_seq=nb, seq=seq, ts=ts)
        p_s, pool_s, sc_s, cm_s, vn_s = mixer_sample(
            z, _to_time_major(state_pool[l]), _to_time_major(state_sconv[l]),
            _to_time_major(state_cconv[l]), lp, td=td, ns=ns, t_prompt=t_prompt, start_pos=past_len)
        x = gate_out(hn, p_p, p_s, w_in_b, w_branch_b, b_gate[l], w_o_b, l, x, tn=tn_gate)

        q = norm_mm(x, g_x[l], w_xq_b, l, tm=tm_big, tn=dx)
        tm_mem = _tile(nb * mem_len, 1024, 16)
        k_p = norm_mm(mem2d, g_mem[l], w_xk_b, l, tm=tm_mem, tn=dx)
        v_p = norm_mm(mem2d, g_mem[l], w_xv_b, l, tm=tm_mem, tn=dx)
        o_p = attn_prompt(q, k_p.reshape(nb, mem_len, dx), v_p.reshape(nb, mem_len, dx),
                          n_seq=nb, seq=seq, tq=tq)
        q_s = jnp.swapaxes(q[t_prompt:].reshape(td, ns, dx), 0, 1).reshape(ns, td * X_HEADS, GROUP)
        o_s = attn_sample(q_s, cache_k, cache_v, l, sb=_tile(ns, 8, 1))
        o_s = jnp.swapaxes(o_s.reshape(ns, td, dx), 0, 1).reshape(t_s, dx)
        x, ht = attn_out(o_p, o_s, w_xo_b, l, x, g_peer[l])

        s2, theta, lam = peer_pre(ht, w_pq_t, keys_b, l, tb=tb_pre)
        x = peer_dense(ht, peer_u_b, peer_vt, l, s2, theta, lam, x, tb=tb, te=te)

        outs["mk"].append(k_p.reshape(nb, mem_len, X_HEADS, GROUP))
        outs["mv"].append(v_p.reshape(nb, mem_len, X_HEADS, GROUP))
        outs["pool_p"].append(pool_p)
        outs["sc_p"].append(sc_p)
        outs["cm_p"].append(cm_p)
        outs["pool_s"].append(_to_time_major(pool_s))
        outs["sc_s"].append(_to_time_major(sc_s))
        outs["cm_s"].append(_to_time_major(cm_s))
        outs["cv_s"].append(_to_time_major(vn_s))

    y_p, y_s = final_norm(x, g_final, tm=t_s)
    y_prompt = y_p.reshape(nb, seq, d)
    y_sample = _to_time_major(y_s.reshape(td, ns, d))
    st = lambda k: jnp.stack(outs[k])
    return (y_prompt, y_sample, st("mk"), st("mv"), st("pool_p"), st("sc_p"), st("cm_p"),
            st("pool_s"), st("sc_s"), st("cm_s"), st("cv_s"))
```

```python
import functools
import math

import jax
import jax.numpy as jnp
from jax import lax
from jax.experimental import pallas as pl
from jax.experimental.pallas import tpu as pltpu

EPS = 1e-6
GROUP = 128
POOL_WINDOWS = (2, 4, 8, 16)
POOL_STATE = max(POOL_WINDOWS) - 1
SC_WIDTH = 3
CM_WIDTH = 31
D_BR = 4 * GROUP
N_BRANCH = 4
CHUNK = 128
X_HEADS = 4
PEER_HEADS = 8
PEER_TOPK = 16
HALO = 32
LANES = 128
ROWS = 32
VMEM_LIMIT = 56 * 2 ** 20

BF16 = jnp.bfloat16
F32 = jnp.float32
NEG_INF = float("-inf")
LOG2E = math.log2(math.e)


def _tile(n, pref, mult=8):
    best = None
    for t in range(mult, min(n, pref) + 1, mult):
        if n % t == 0:
            best = t
    assert best is not None, (n, pref, mult)
    return best


def _params(*sem):
    return pltpu.CompilerParams(dimension_semantics=sem, vmem_limit_bytes=VMEM_LIMIT)


def _rms(x, g):
    ms = jnp.mean(x * x, axis=-1, keepdims=True)
    return x * lax.rsqrt(ms + EPS) * g


def _ln(x, g, b):
    mu = jnp.mean(x, axis=-1, keepdims=True)
    xc = x - mu
    var = jnp.mean(xc * xc, axis=-1, keepdims=True)
    return xc * lax.rsqrt(var + EPS) * g + b


def _sigmoid(x):
    return 1.0 / (1.0 + jnp.exp(-x))


def _dot(a, b):
    return jnp.dot(a, b, preferred_element_type=F32)


def _norm_mm_kernel(x_ref, g_ref, w_ref, o_ref, hn_ref):
    @pl.when(pl.program_id(1) == 0)
    def _():
        hn_ref[...] = _rms(x_ref[...], g_ref[...]).astype(BF16)

    o_ref[...] = _dot(hn_ref[...], w_ref[...])


def norm_mm(x, g, w, l, *, tm, tn):
    t, d = x.shape
    n = w.shape[2]
    return pl.pallas_call(
        _norm_mm_kernel,
        grid=(t // tm, n // tn),
        in_specs=[pl.BlockSpec((tm, d), lambda i, j: (i, 0)),
                  pl.BlockSpec((1, d), lambda i, j: (0, 0)),
                  pl.BlockSpec((None, d, tn), lambda i, j: (l, 0, j))],
        out_specs=pl.BlockSpec((tm, tn), lambda i, j: (i, j)),
        out_shape=jax.ShapeDtypeStruct((t, n), F32),
        scratch_shapes=[pltpu.VMEM((tm, d), BF16)],
        compiler_params=_params("parallel", "arbitrary"),
        name="norm_mm",
    )(x, g.reshape(1, d), w)


def _rows_of(i, n_head, head_ref, tail_ref):
    return jnp.where(i < n_head, head_ref[...], tail_ref[...])


def _attn_out_kernel(a_ref, at_ref, w_ref, r_ref, g_ref, o_ref, ht_ref, *, n_head):
    a = _rows_of(pl.program_id(0), n_head, a_ref, at_ref)
    x = r_ref[...] + _dot(a.astype(BF16), w_ref[...])
    o_ref[...] = x
    ht_ref[...] = _rms(x, g_ref[...]).T.astype(BF16)


def attn_out(a, a_tail, w, l, res, g):
    t, d = res.shape
    tm, k = a_tail.shape
    n_head = a.shape[0] // tm
    assert a.shape[0] == n_head * tm and t == (n_head + 1) * tm and tm % LANES == 0
    row = pl.BlockSpec((tm, d), lambda i: (i, 0))
    return pl.pallas_call(
        functools.partial(_attn_out_kernel, n_head=n_head),
        grid=(t // tm,),
        in_specs=[pl.BlockSpec((tm, k), lambda i: (jnp.minimum(i, n_head - 1), 0)),
                  pl.BlockSpec((tm, k), lambda i: (0, 0)),
                  pl.BlockSpec((None, k, d), lambda i: (l, 0, 0)),
                  row, pl.BlockSpec((1, d), lambda i: (0, 0))],
        out_specs=[row, pl.BlockSpec((d, tm), lambda i: (0, i))],
        out_shape=[jax.ShapeDtypeStruct((t, d), F32), jax.ShapeDtypeStruct((d, t), BF16)],
        compiler_params=_params("parallel"),
        name="attn_out",
    )(a, a_tail, w, res, g.reshape(1, d))


def _pool_branch(window_sums, a, cnts, poolw_ref, pscale):
    outs = []
    for gi in range(len(POOL_WINDOWS)):
        sl = slice(gi * GROUP, (gi + 1) * GROUP)
        pg = window_sums[gi] / cnts[gi] - a[:, sl]
        outs.append(_dot(pg.astype(BF16), poolw_ref[gi]))
    return jnp.concatenate(outs, axis=1) * pscale


def _mixer_prompt_kernel(z_ref, poolw_ref, pscale_ref, scw_ref, cmw_ref, cmb_ref, cmg_ref,
                         cmbeta_ref, sgg_ref, sgbeta_ref, sgw_ref, sgbias_ref,
                         p_ref, pool_o, sc_o, cm_o, ext_ref, *, ts):
    tb = pl.program_id(1)
    h0 = HALO

    @pl.when(tb == 0)
    def _():
        ext_ref[:, 0:h0, :] = jnp.zeros((3, h0, D_BR), F32)

    a = z_ref[:, 0:D_BR]
    bg = z_ref[:, D_BR:2 * D_BR]
    gated = z_ref[:, 2 * D_BR:3 * D_BR] * z_ref[:, 3 * D_BR:4 * D_BR]
    glu = z_ref[:, 4 * D_BR:5 * D_BR] * _sigmoid(z_ref[:, 5 * D_BR:6 * D_BR])
    ext_ref[0, h0:h0 + ts, :] = a
    ext_ref[1, h0:h0 + ts, :] = gated
    ext_ref[2, h0:h0 + ts, :] = glu

    pos = tb * ts + lax.broadcasted_iota(jnp.int32, (ts, 1), 0)
    sums, cnts = [], []
    for gi, w in enumerate(POOL_WINDOWS):
        sl = slice(gi * GROUP, (gi + 1) * GROUP)
        s = a[:, sl]
        for j in range(1, w):
            s = s + ext_ref[0, h0 - j:h0 - j + ts, sl]
        sums.append(s)
        cnts.append(jnp.minimum(w, pos + 1).astype(F32))
    p_ref[:, 0:D_BR] = _pool_branch(sums, a, cnts, poolw_ref, pscale_ref[...]).astype(BF16)

    conv = scw_ref[SC_WIDTH - 1:SC_WIDTH, :] * gated
    for k in range(SC_WIDTH - 1):
        off = h0 - (SC_WIDTH - 1) + k
        conv = conv + scw_ref[k:k + 1, :] * ext_ref[1, off:off + ts, :]
    p_ref[:, D_BR:2 * D_BR] = (bg * conv).astype(BF16)

    c = cmw_ref[CM_WIDTH - 1:CM_WIDTH, :] * glu
    for k in range(CM_WIDTH - 1):
        off = h0 - (CM_WIDTH - 1) + k
        c = c + cmw_ref[k:k + 1, :] * ext_ref[2, off:off + ts, :]
    c = _ln(c + cmb_ref[...], cmg_ref[...], cmbeta_ref[...])
    p_ref[:, 2 * D_BR:3 * D_BR] = (c * _sigmoid(c)).astype(BF16)

    vn = _ln(z_ref[:, 7 * D_BR:8 * D_BR], sgg_ref[...], sgbeta_ref[...])
    row = lax.broadcasted_iota(jnp.int32, (CHUNK, CHUNK), 0)
    col = lax.broadcasted_iota(jnp.int32, (CHUNK, CHUNK), 1)
    tril = row >= col
    for gi in range(D_BR // GROUP):
        sl = slice(gi * GROUP, (gi + 1) * GROUP)
        wg = jnp.where(tril, sgw_ref[gi], 0.0).astype(BF16)
        for ci in range(ts // CHUNK):
            rows = slice(ci * CHUNK, (ci + 1) * CHUNK)
            mixed = _dot(wg, vn[rows, sl].astype(BF16)) + sgbias_ref[:, sl]
            p_ref[rows, 3 * D_BR + gi * GROUP:3 * D_BR + (gi + 1) * GROUP] = (
                z_ref[rows, 6 * D_BR + gi * GROUP:6 * D_BR + (gi + 1) * GROUP] * mixed).astype(BF16)

    end = h0 + ts
    pool_o[0] = ext_ref[0, end - POOL_STATE:end, :]
    sc_o[0] = ext_ref[1, end - (SC_WIDTH - 1):end, :]
    cm_o[0] = ext_ref[2, end - (CM_WIDTH - 1):end, :]
    ext_ref[:, 0:h0, :] = ext_ref[:, ts:ts + h0, :]


def mixer_prompt(z, lp, *, n_seq, seq, ts):
    nb = seq // ts
    full = lambda *shape: pl.BlockSpec(shape, lambda n, t: (0,) * len(shape))
    state = lambda rows: pl.BlockSpec((1, rows, D_BR), lambda n, t: (n, 0, 0))
    return pl.pallas_call(
        functools.partial(_mixer_prompt_kernel, ts=ts),
        grid=(n_seq, nb),
        in_specs=[pl.BlockSpec((ts, 8 * D_BR), lambda n, t: (n * nb + t, 0)),
                  full(4, GROUP, GROUP), full(1, D_BR), full(SC_WIDTH, D_BR),
                  full(CM_WIDTH, D_BR), full(1, D_BR), full(1, D_BR), full(1, D_BR),
                  full(1, D_BR), full(1, D_BR), full(4, CHUNK, CHUNK), full(CHUNK, D_BR)],
        out_specs=[pl.BlockSpec((ts, N_BRANCH * D_BR), lambda n, t: (n * nb + t, 0)),
                   state(POOL_STATE), state(SC_WIDTH - 1), state(CM_WIDTH - 1)],
        out_shape=[jax.ShapeDtypeStruct((n_seq * seq, N_BRANCH * D_BR), BF16),
                   jax.ShapeDtypeStruct((n_seq, POOL_STATE, D_BR), F32),
                   jax.ShapeDtypeStruct((n_seq, SC_WIDTH - 1, D_BR), F32),
                   jax.ShapeDtypeStruct((n_seq, CM_WIDTH - 1, D_BR), F32)],
        scratch_shapes=[pltpu.VMEM((3, HALO + ts, D_BR), F32)],
        compiler_params=_params("parallel", "arbitrary"),
        name="mixer_prompt",
    )(z, lp["pool_w"], lp["pool_scale"], lp["sc_w"], lp["cm_w"], lp["cm_b"], lp["cm_ln_g"],
      lp["cm_ln_b"], lp["sg_ln_g"], lp["sg_ln_b"], lp["sg_w"], lp["sg_bias"])


def _mixer_sample_kernel(z_ref, pool_ref, sc_ref, cm_ref, poolw_ref, pscale_ref, scw_ref, cmw_ref,
                         cmb_ref, cmg_ref, cmbeta_ref, sgg_ref, sgbeta_ref, sgwts_ref, sgbias_ref,
                         p_ref, pool_o, sc_o, cm_o, vn_o, *, td, ns, start_pos):
    rows = lambda t: slice(t * ns, (t + 1) * ns)
    col = lambda k: slice(k * D_BR, (k + 1) * D_BR)

    ext_a = [pool_ref[j] for j in range(POOL_STATE)] + [z_ref[rows(t), col(0)] for t in range(td)]
    a_all = jnp.concatenate(ext_a[POOL_STATE:], axis=0)
    sums, cnts = [], []
    for gi, w in enumerate(POOL_WINDOWS):
        sl = slice(gi * GROUP, (gi + 1) * GROUP)
        per_t = []
        for t in range(td):
            s = ext_a[POOL_STATE + t][:, sl]
            for j in range(1, w):
                s = s + ext_a[POOL_STATE + t - j][:, sl]
            per_t.append(s)
        sums.append(jnp.concatenate(per_t, axis=0))
        cnt = jnp.concatenate(
            [jnp.full((ns, 1), float(min(w, start_pos + t + 1)), F32) for t in range(td)], axis=0)
        cnts.append(cnt)
    p_ref[:, col(0)] = _pool_branch(sums, a_all, cnts, poolw_ref, pscale_ref[...]).astype(BF16)
    for j in range(POOL_STATE):
        pool_o[j] = ext_a[td + j]

    ext_b = [sc_ref[j] for j in range(SC_WIDTH - 1)]
    ext_b += [z_ref[rows(t), col(2)] * z_ref[rows(t), col(3)] for t in range(td)]
    for t in range(td):
        conv = scw_ref[0:1, :] * ext_b[t]
        for k in range(1, SC_WIDTH):
            conv = conv + scw_ref[k:k + 1, :] * ext_b[t + k]
        p_ref[rows(t), col(1)] = (z_ref[rows(t), col(1)] * conv).astype(BF16)
    for j in range(SC_WIDTH - 1):
        sc_o[j] = ext_b[td + j]

    ext_c = [cm_ref[j] for j in range(CM_WIDTH - 1)]
    ext_c += [z_ref[rows(t), col(4)] * _sigmoid(z_ref[rows(t), col(5)]) for t in range(td)]
    for t in range(td):
        c = cmw_ref[0:1, :] * ext_c[t]
        for k in range(1, CM_WIDTH):
            c = c + cmw_ref[k:k + 1, :] * ext_c[t + k]
        c = _ln(c + cmb_ref[...], cmg_ref[...], cmbeta_ref[...])
        p_ref[rows(t), col(2)] = (c * _sigmoid(c)).astype(BF16)
    for j in range(CM_WIDTH - 1):
        cm_o[j] = ext_c[td + j]

    vn = [_ln(z_ref[rows(t), col(7)], sgg_ref[...], sgbeta_ref[...]) for t in range(td)]
    for t in range(td):
        vn_o[t] = vn[t]
        mixed = sgbias_ref[t:t + 1, :]
        for s in range(t + 1):
            mixed = mixed + sgwts_ref[t, s:s + 1, :] * vn[s]
        p_ref[rows(t), col(3)] = (z_ref[rows(t), col(6)] * mixed).astype(BF16)


def mixer_sample(z, pool_tm, sc_tm, cm_tm, lp, *, td, ns, t_prompt, start_pos):
    blk = t_prompt // (td * ns)
    full = lambda *shape: pl.BlockSpec(shape, lambda i: (0,) * len(shape))
    tm_shape = lambda rows: jax.ShapeDtypeStruct((rows, ns, D_BR), F32)
    return pl.pallas_call(
        functools.partial(_mixer_sample_kernel, td=td, ns=ns, start_pos=start_pos),
        grid=(1,),
        in_specs=[pl.BlockSpec((td * ns, 8 * D_BR), lambda i: (blk, 0)),
                  full(POOL_STATE, ns, D_BR), full(SC_WIDTH - 1, ns, D_BR),
                  full(CM_WIDTH - 1, ns, D_BR),
                  full(4, GROUP, GROUP), full(1, D_BR), full(SC_WIDTH, D_BR),
                  full(CM_WIDTH, D_BR), full(1, D_BR), full(1, D_BR), full(1, D_BR),
                  full(1, D_BR), full(1, D_BR), full(td, td, D_BR), full(CHUNK, D_BR)],
        out_specs=[full(td * ns, N_BRANCH * D_BR),
                   full(POOL_STATE, ns, D_BR), full(SC_WIDTH - 1, ns, D_BR),
                   full(CM_WIDTH - 1, ns, D_BR), full(td, ns, D_BR)],
        out_shape=[jax.ShapeDtypeStruct((td * ns, N_BRANCH * D_BR), BF16), tm_shape(POOL_STATE),
                   tm_shape(SC_WIDTH - 1), tm_shape(CM_WIDTH - 1), tm_shape(td)],
        compiler_params=_params("arbitrary"),
        name="mixer_sample",
    )(z, pool_tm, sc_tm, cm_tm, lp["pool_w"], lp["pool_scale"], lp["sc_w"], lp["cm_w"],
      lp["cm_b"], lp["cm_ln_g"], lp["cm_ln_b"], lp["sg_ln_g"], lp["sg_ln_b"], lp["sg_wts"],
      lp["sg_bias"])


def _norm_in_kernel(x_ref, g_ref, w_ref, z_ref, hn_ref):
    @pl.when(pl.program_id(1) == 0)
    def _():
        hn_ref[...] = _rms(x_ref[...], g_ref[...]).astype(BF16)

    z_ref[...] = _dot(hn_ref[...], w_ref[...])


def norm_in(x, g, w, l, n_cols, *, tm, tn):
    t, d = x.shape
    return pl.pallas_call(
        _norm_in_kernel,
        grid=(t // tm, n_cols // tn),
        in_specs=[pl.BlockSpec((tm, d), lambda i, j: (i, 0)),
                  pl.BlockSpec((1, d), lambda i, j: (0, 0)),
                  pl.BlockSpec((None, d, tn), lambda i, j: (l, 0, j))],
        out_specs=[pl.BlockSpec((tm, tn), lambda i, j: (i, j)),
                   pl.BlockSpec((tm, d), lambda i, j: (i, 0))],
        out_shape=[jax.ShapeDtypeStruct((t, n_cols), F32), jax.ShapeDtypeStruct((t, d), BF16)],
        compiler_params=_params("parallel", "arbitrary"),
        name="norm_in",
    )(x, g.reshape(1, d), w)


def _gate_out_kernel(hn_ref, p_ref, pt_ref, g0_ref, g1_ref, g2_ref, g3_ref, w0_ref, w1_ref,
                     w2_ref, w3_ref, b_ref, wo_ref, x_ref, o_ref, merged_ref, *, n_head, tn):
    j = pl.program_id(1)
    p = _rows_of(pl.program_id(0), n_head, p_ref, pt_ref)
    hn = hn_ref[...]
    acc = None
    for i, (g_ref, w_ref) in enumerate(((g0_ref, w0_ref), (g1_ref, w1_ref), (g2_ref, w2_ref),
                                        (g3_ref, w3_ref))):
        gate = _sigmoid(_dot(hn, g_ref[...]) + b_ref[i:i + 1, :])
        term = gate * _dot(p[:, i * D_BR:(i + 1) * D_BR].astype(BF16), w_ref[...])
        acc = term if acc is None else acc + term
    merged_ref[:, pl.ds(pl.multiple_of(j * tn, tn), tn)] = acc.astype(BF16)

    @pl.when(j == pl.num_programs(1) - 1)
    def _():
        o_ref[...] = x_ref[...] + _dot(merged_ref[...], wo_ref[...])


def gate_out(hn, p_head, p_tail, w_in, w_outs, b_gate, w_o, l, x, *, tn):
    t, d = x.shape
    tm = p_tail.shape[0]
    n_head = p_head.shape[0] // tm
    assert p_head.shape[0] == n_head * tm and t == (n_head + 1) * tm
    first = N_BRANCH * 2 * D_BR // tn
    gate_spec = lambda i: pl.BlockSpec((None, d, tn), lambda r, c: (l, 0, first + i * (d // tn) + c))
    w_spec = pl.BlockSpec((None, D_BR, tn), lambda r, c: (l, 0, c))
    row = pl.BlockSpec((tm, d), lambda r, c: (r, 0))
    return pl.pallas_call(
        functools.partial(_gate_out_kernel, n_head=n_head, tn=tn),
        grid=(t // tm, d // tn),
        in_specs=[row,
                  pl.BlockSpec((tm, N_BRANCH * D_BR), lambda r, c: (jnp.minimum(r, n_head - 1), 0)),
                  pl.BlockSpec((tm, N_BRANCH * D_BR), lambda r, c: (0, 0)),
                  gate_spec(0), gate_spec(1), gate_spec(2), gate_spec(3),
                  w_spec, w_spec, w_spec, w_spec,
                  pl.BlockSpec((N_BRANCH, tn), lambda r, c: (0, c)),
                  pl.BlockSpec((None, d, d), lambda r, c: (l, 0, 0), pipeline_mode=pl.Buffered(1)),
                  row],
        out_specs=row,
        out_shape=jax.ShapeDtypeStruct((t, d), F32),
        scratch_shapes=[pltpu.VMEM((tm, d), BF16)],
        compiler_params=_params("parallel", "arbitrary"),
        name="gate_out",
    )(hn, p_head, p_tail, w_in, w_in, w_in, w_in, *w_outs, b_gate, w_o, x)


def _softmax_rows(s):
    m = jnp.max(s, axis=-1, keepdims=True)
    p = jnp.exp(s - m)
    return p / jnp.sum(p, axis=-1, keepdims=True)


def _attn_prompt_kernel(q_ref, k_ref, v_ref, o_ref, *, scale):
    for h in range(X_HEADS):
        sl = slice(h * GROUP, (h + 1) * GROUP)
        s = lax.dot_general(q_ref[:, sl].astype(BF16), k_ref[0, :, sl].astype(BF16),
                            (((1,), (1,)), ((), ())), preferred_element_type=F32) * scale
        o_ref[:, sl] = _dot(_softmax_rows(s).astype(BF16), v_ref[0, :, sl].astype(BF16))


def attn_prompt(q, k, v, *, n_seq, seq, tq):
    dx = q.shape[1]
    mem = k.shape[1]
    nb = seq // tq
    return pl.pallas_call(
        functools.partial(_attn_prompt_kernel, scale=GROUP ** -0.5),
        grid=(n_seq, nb),
        in_specs=[pl.BlockSpec((tq, dx), lambda n, t: (n * nb + t, 0)),
                  pl.BlockSpec((1, mem, dx), lambda n, t: (n, 0, 0)),
                  pl.BlockSpec((1, mem, dx), lambda n, t: (n, 0, 0))],
        out_specs=pl.BlockSpec((tq, dx), lambda n, t: (n * nb + t, 0)),
        out_shape=jax.ShapeDtypeStruct((n_seq * seq, dx), F32),
        compiler_params=_params("parallel", "parallel"),
        name="attn_prompt",
    )(q, k, v)


def _attn_sample_kernel(q_ref, k_ref, v_ref, o_ref, *, scale):
    rows, cols = q_ref.shape[1], k_ref.shape[1]
    row_head = lax.broadcasted_iota(jnp.int32, (rows, cols), 0) % X_HEADS
    col_head = lax.broadcasted_iota(jnp.int32, (rows, cols), 1) % X_HEADS
    s = jnp.einsum("bqd,bkd->bqk", q_ref[...].astype(BF16), k_ref[...].astype(BF16),
                   preferred_element_type=F32) * scale
    s = jnp.where((row_head == col_head)[None], s, NEG_INF)
    o_ref[...] = jnp.einsum("bqk,bkd->bqd", _softmax_rows(s).astype(BF16),
                            v_ref[...].astype(BF16), preferred_element_type=F32)


def attn_sample(q, k, v, l, *, sb):
    n_seq, rows, hd = q.shape
    kv_spec = pl.BlockSpec((None, sb, k.shape[2], hd), lambda i: (l, i, 0, 0))
    return pl.pallas_call(
        functools.partial(_attn_sample_kernel, scale=GROUP ** -0.5),
        grid=(n_seq // sb,),
        in_specs=[pl.BlockSpec((sb, rows, hd), lambda i: (i, 0, 0)), kv_spec, kv_spec],
        out_specs=pl.BlockSpec((sb, rows, hd), lambda i: (i, 0, 0)),
        out_shape=jax.ShapeDtypeStruct((n_seq, rows, hd), F32),
        compiler_params=_params("parallel"),
        name="attn_sample",
    )(q, k, v)


def _merge_exchange_network(n):
    pairs, p = [], 1
    while p < n:
        k = p
        while k >= 1:
            for j in range(k % p, n - k, 2 * k):
                for i in range(min(k, n - j - k)):
                    if (i + j) // (2 * p) == (i + j + k) // (2 * p):
                        pairs.append((i + j, i + j + k))
            k //= 2
        p *= 2
    return pairs


def _sort_descending(values):
    n = pl.next_power_of_2(len(values))
    vals = list(values) + [None] * (n - len(values))
    for i, j in _merge_exchange_network(n):
        hi, lo = vals[i], vals[j]
        if lo is None:
            continue
        if hi is None:
            vals[i], vals[j] = lo, None
        else:
            vals[i], vals[j] = jnp.maximum(hi, lo), jnp.minimum(hi, lo)
    return vals[:len(values)]


_CAND = [(i, j) for i in range(PEER_TOPK + 1) for j in range(PEER_TOPK + 1)
         if (i + 1) * (j + 1) <= PEER_TOPK + 1]


def _peer_pre_kernel(ht_ref, wq_ref, keys_ref, s2_o, th_o, lam_o, q_scr, s_scr, v_scr, *, tb):
    nk = GROUP
    q_scr[...] = _dot(wq_ref[...], ht_ref[...]).astype(BF16)
    for hp in range(2 * PEER_HEADS):
        s_scr[hp] = _dot(keys_ref[hp], q_scr[hp * nk:(hp + 1) * nk, :])

    idx = lax.broadcasted_iota(jnp.int32, (nk, LANES), 0)

    def chunk(ci, carry):
        lanes = pl.ds(pl.multiple_of(ci * LANES, LANES), LANES)
        tied = jnp.zeros((1, LANES), F32)
        for hp in range(2 * PEER_HEADS):
            h, p = divmod(hp, 2)
            cur = s_scr[hp, :, lanes]
            stacks = _sort_descending([cur[8 * g:8 * (g + 1), :] for g in range(nk // 8)])
            for r in range(PEER_TOPK + 1):
                m = jnp.max(stacks[0], axis=0, keepdims=True)
                v_scr[p, r, h:h + 1, lanes] = m
                depth = PEER_TOPK - r
                if depth > 0:
                    hit = stacks[0] == m
                    below = stacks[1:] + [jnp.full_like(stacks[0], NEG_INF)]
                    stacks = [jnp.where(hit, below[q], stacks[q])
                              for q in range(min(depth, len(stacks)))]
            n_ge = jnp.sum(jnp.where(cur >= m, 1.0, 0.0), axis=0, keepdims=True)
            tied = jnp.maximum(tied, jnp.where(n_ge != PEER_TOPK + 1.0, 1.0, 0.0))

        @pl.when(jnp.max(tied) > 0.0)
        def _():
            for hp in range(2 * PEER_HEADS):
                h, p = divmod(hp, 2)
                cur = s_scr[hp, :, lanes]
                for r in range(PEER_TOPK + 1):
                    m = jnp.max(cur, axis=0, keepdims=True)
                    v_scr[p, r, h:h + 1, lanes] = m
                    if r < PEER_TOPK:
                        first = jnp.min(jnp.where(cur == m, idx, nk), axis=0, keepdims=True)
                        cur = jnp.where(idx == first, NEG_INF, cur)

        v1 = [v_scr[0, r, :, lanes] for r in range(PEER_TOPK + 1)]
        v2 = [v_scr[1, r, :, lanes] for r in range(PEER_TOPK + 1)]
        cands = [v1[i] + v2[j] for i, j in _CAND]
        ranked = _sort_descending(cands)
        kth, nxt = ranked[PEER_TOPK - 1], ranked[PEER_TOPK]
        tau = 0.5 * (kth + nxt)
        top = cands[0]
        zsum = jnp.zeros_like(top)
        for ck in cands:
            zsum = zsum + jnp.where(ck >= tau, jnp.exp(ck - top), 0.0)
        shift = top + jnp.log(zsum)
        for h in range(PEER_HEADS):
            s1 = s_scr[2 * h, :, lanes]
            th_o[h, :, lanes] = (tau[h:h + 1, :] - s1) * LOG2E
            lam_o[h, :, lanes] = (s1 - shift[h:h + 1, :]) * LOG2E - 0.5
            s2_o[h * nk:(h + 1) * nk, lanes] = s_scr[2 * h + 1, :, lanes] * LOG2E
        return carry

    lax.fori_loop(0, tb // LANES, chunk, 0)


def peer_pre(ht, wq_t, keys, l, *, tb):
    d, t = ht.shape
    nq = wq_t.shape[1]
    hk = PEER_HEADS * GROUP
    flat = pl.BlockSpec((hk, tb), lambda i: (0, i))
    cube = pl.BlockSpec((PEER_HEADS, GROUP, tb), lambda i: (0, 0, i))
    return pl.pallas_call(
        functools.partial(_peer_pre_kernel, tb=tb),
        grid=(t // tb,),
        in_specs=[pl.BlockSpec((d, tb), lambda i: (0, i)),
                  pl.BlockSpec((None, nq, d), lambda i: (l, 0, 0)),
                  pl.BlockSpec((None, 2 * PEER_HEADS, GROUP, GROUP), lambda i: (l, 0, 0, 0))],
        out_specs=[flat, cube, cube],
        out_shape=[jax.ShapeDtypeStruct((hk, t), F32),
                   jax.ShapeDtypeStruct((PEER_HEADS, GROUP, t), F32),
                   jax.ShapeDtypeStruct((PEER_HEADS, GROUP, t), F32)],
        scratch_shapes=[pltpu.VMEM((nq, tb), BF16),
                        pltpu.VMEM((2 * PEER_HEADS, GROUP, tb), F32),
                        pltpu.VMEM((2, PEER_TOPK + 1, PEER_HEADS, tb), F32)],
        compiler_params=_params("parallel"),
        name="peer_pre",
    )(ht, wq_t, keys)


def _peer_dense_kernel(ht_ref, htn_ref, ua_ref, ub_ref, vt_ref, s2_ref, th_ref, lam_ref, x_ref,
                       o_ref, acc_ref, at_ref, gw_ref, *, tb, te, ne):
    s = pl.program_id(0)
    e_cur = jnp.maximum(s - 1, 0) % ne
    half = te // 2
    tbh = tb // 2
    n_a = half // GROUP
    piece = lambda k, c: 2 * k + c
    cols = lambda c: slice(c * tbh, (c + 1) * tbh)

    @pl.when(s == 0)
    def _():
        at_ref[0:2] = jnp.zeros((2, half, tbh), F32)

    @pl.when(e_cur == 0)
    def _():
        acc_ref[...] = jnp.zeros_like(acc_ref)

    def pre_activations(k, c, u_ref, h_ref):
        at_ref[piece(k, c)] = _dot(u_ref[...], h_ref[:, cols(c)])

    def weighted_activations(k, c):
        p = piece(k, c)
        for cl in range(tbh // LANES):
            lanes = slice(cl * LANES, (cl + 1) * LANES)
            glanes = slice(c * tbh + cl * LANES, c * tbh + (cl + 1) * LANES)
            for r in range(GROUP // ROWS):
                w = [jnp.zeros((ROWS, LANES), F32) for _ in range(n_a)]
                for h in range(PEER_HEADS):
                    s2 = s2_ref[h * GROUP + r * ROWS:h * GROUP + (r + 1) * ROWS, glanes]
                    for a in range(n_a):
                        ag = k * n_a + a
                        keep = s2 >= th_ref[h, ag:ag + 1, glanes]
                        w[a] = w[a] + jnp.where(keep, jnp.exp2(s2 + lam_ref[h, ag:ag + 1, glanes]), 0.0)
                for a in range(n_a):
                    rows = slice(a * GROUP + r * ROWS, a * GROUP + (r + 1) * ROWS)
                    z = at_ref[p, rows, lanes]
                    gw_ref[p, rows, lanes] = (w[a] * (z * (1.0 + lax.erf(z)))).astype(BF16)

    def accumulate(k, c):
        acc_ref[:, cols(c)] += _dot(vt_ref[:, k * half:(k + 1) * half], gw_ref[piece(k, c)])

    pre_activations(1, 0, ub_ref, ht_ref)
    pre_activations(1, 1, ub_ref, ht_ref)
    for c in range(2):
        weighted_activations(0, c)
        accumulate(0, c)
        pre_activations(0, c, ua_ref, htn_ref)
    for c in range(2):
        weighted_activations(1, c)
        accumulate(1, c)

    @pl.when(jnp.logical_and(e_cur == ne - 1, s > 0))
    def _():
        o_ref[...] = x_ref[...] + acc_ref[...].T


def peer_dense(ht, u, vt, l, s2, theta, lam, x, *, tb, te):
    d, t = ht.shape
    ne = u.shape[1] // te
    assert vt.shape[1:] == (ne, d, te)
    n_tiles = (t // tb) * ne
    half = te // 2
    a_blk = te // GROUP
    once = pl.Buffered(1)
    cur = lambda s: jnp.clip(s - 1, 0, n_tiles - 1)
    nxt = lambda s: jnp.minimum(s, n_tiles - 1)
    return pl.pallas_call(
        functools.partial(_peer_dense_kernel, tb=tb, te=te, ne=ne),
        grid=(n_tiles + 1,),
        in_specs=[pl.BlockSpec((d, tb), lambda s: (0, cur(s) // ne), pipeline_mode=once),
                  pl.BlockSpec((d, tb), lambda s: (0, nxt(s) // ne), pipeline_mode=once),
                  pl.BlockSpec((None, half, d), lambda s: (l, 2 * (nxt(s) % ne), 0)),
                  pl.BlockSpec((None, half, d), lambda s: (l, 2 * (cur(s) % ne) + 1, 0)),
                  pl.BlockSpec((None, None, d, te), lambda s: (l, cur(s) % ne, 0, 0)),
                  pl.BlockSpec((PEER_HEADS * GROUP, tb), lambda s: (0, cur(s) // ne),
                               pipeline_mode=once),
                  pl.BlockSpec((PEER_HEADS, a_blk, tb), lambda s: (0, cur(s) % ne, cur(s) // ne)),
                  pl.BlockSpec((PEER_HEADS, a_blk, tb), lambda s: (0, cur(s) % ne, cur(s) // ne)),
                  pl.BlockSpec((tb, d), lambda s: (cur(s) // ne, 0), pipeline_mode=once)],
        out_specs=pl.BlockSpec((tb, d), lambda s: (cur(s) // ne, 0)),
        out_shape=jax.ShapeDtypeStruct((t, d), F32),
        scratch_shapes=[pltpu.VMEM((d, tb), F32), pltpu.VMEM((4, half, tb // 2), F32),
                        pltpu.VMEM((4, half, tb // 2), BF16)],
        compiler_params=_params("arbitrary"),
        name="peer_dense",
    )(ht, ht, u, u, vt, s2, theta, lam, x)


def _final_norm_kernel(x_ref, g_ref, head_o, tail_o, *, n_head):
    i = pl.program_id(0)
    y = _rms(x_ref[...], g_ref[...])

    @pl.when(i < n_head)
    def _():
        head_o[...] = y

    @pl.when(i == n_head)
    def _():
        tail_o[...] = y


def final_norm(x, g, *, tm):
    t, d = x.shape
    n_head = t // tm - 1
    assert t == (n_head + 1) * tm
    return pl.pallas_call(
        functools.partial(_final_norm_kernel, n_head=n_head),
        grid=(n_head + 1,),
        in_specs=[pl.BlockSpec((tm, d), lambda i: (i, 0)), pl.BlockSpec((1, d), lambda i: (0, 0))],
        out_specs=[pl.BlockSpec((tm, d), lambda i: (jnp.minimum(i, n_head - 1), 0)),
                   pl.BlockSpec((tm, d), lambda i: (0, 0))],
        out_shape=[jax.ShapeDtypeStruct((n_head * tm, d), F32),
                   jax.ShapeDtypeStruct((tm, d), F32)],
        compiler_params=_params("arbitrary"),
        name="final_norm",
    )(x, g.reshape(1, d))


def _to_time_major(s):
    return jnp.swapaxes(s, 0, 1)


def kernel(x_prompt, x_sample, mem_prompt, cache_mem_k, cache_mem_v, state_pool, state_sconv, state_cconv, g_mix, w_in, b_gate, pool_w, pool_scale, w_pool_out, sc_w, w_sc_out, cm_w, cm_b, cm_ln_g, cm_ln_b, w_cm_out, sg_ln_g, sg_ln_b, sg_w, sg_b, w_sg_out, w_o, g_x, g_mem, w_xq, w_xk, w_xv, w_xo, g_peer, w_pq, peer_keys, peer_u, peer_v, g_final):
    nb, seq, d = x_prompt.shape
    ns, td, _ = x_sample.shape
    depth = w_in.shape[0]
    mem_len = mem_prompt.shape[1]
    past_len = 16384
    t_prompt = nb * seq
    t_all = t_prompt + ns * td
    dx = X_HEADS * GROUP

    tm_big = _tile(t_all, 1088, 16)
    t_s = ns * td
    assert t_prompt % t_s == 0
    tn_in = _tile(w_in.shape[2], 1024, 128)
    tn_d = _tile(d, 512, 128)
    tn_gate = _tile(d, 256, 128)
    ts = _tile(seq, 256, CHUNK)
    tq = _tile(seq, 512, 8)
    tb = _tile(t_all, 512, LANES)
    tb_pre = _tile(t_all, 256, LANES)
    te = _tile(peer_u.shape[1], 1024, 8 * GROUP)

    x = jnp.concatenate([x_prompt.reshape(t_prompt, d),
                         _to_time_major(x_sample).reshape(ns * td, d)], axis=0)
    mem2d = mem_prompt.reshape(nb * mem_len, d)
    col1 = lambda v: v.reshape(1, -1)
    cache_k = cache_mem_k.reshape(depth, ns, mem_len * X_HEADS, GROUP)
    cache_v = cache_mem_v.reshape(depth, ns, mem_len * X_HEADS, GROUP)

    bf = lambda w: w.astype(BF16)
    w_in_b, w_o_b, w_xq_b, w_xk_b, w_xv_b, w_xo_b = map(bf, (w_in, w_o, w_xq, w_xk, w_xv, w_xo))
    w_branch_b = tuple(map(bf, (w_pool_out, w_sc_out, w_cm_out, w_sg_out)))
    w_pq_t = bf(jnp.swapaxes(w_pq, 1, 2))
    keys_b = bf(peer_keys.reshape(depth, 2 * PEER_HEADS, GROUP, GROUP))
    peer_u_b = bf(peer_u * math.sqrt(0.5))
    peer_vt = bf(jnp.swapaxes(peer_v.reshape(depth, -1, te, d), 2, 3))

    outs = {k: [] for k in ("mk", "mv", "pool_p", "sc_p", "cm_p", "pool_s", "sc_s", "cm_s", "cv_s")}
    for l in range(depth):
        lp = dict(
            pool_w=pool_w[l].astype(BF16), pool_scale=col1(pool_scale[l]), sc_w=sc_w[l],
            cm_w=cm_w[l], cm_b=col1(cm_b[l]), cm_ln_g=col1(cm_ln_g[l]), cm_ln_b=col1(cm_ln_b[l]),
            sg_ln_g=col1(sg_ln_g[l]), sg_ln_b=col1(sg_ln_b[l]), sg_w=sg_w[l],
            sg_bias=jnp.repeat(sg_b[l].T, GROUP, axis=1),
            sg_wts=jnp.repeat(jnp.transpose(sg_w[l][:, :td, :td], (1, 2, 0)), GROUP, axis=2))

        z, hn = norm_in(x, g_mix[l], w_in_b, l, 2 * N_BRANCH * D_BR, tm=tm_big, tn=tn_in)
        p_p, pool_p, sc_p, cm_p = mixer_prompt(z, lp, n_seq=nb, seq=seq, ts=ts)
        p_s, pool_s, sc_s, cm_s, vn_s = mixer_sample(
            z, _to_time_major(state_pool[l]), _to_time_major(state_sconv[l]),
            _to_time_major(state_cconv[l]), lp, td=td, ns=ns, t_prompt=t_prompt, start_pos=past_len)
        x = gate_out(hn, p_p, p_s, w_in_b, w_branch_b, b_gate[l], w_o_b, l, x, tn=tn_gate)

        q = norm_mm(x, g_x[l], w_xq_b, l, tm=tm_big, tn=dx)
        tm_mem = _tile(nb * mem_len, 1024, 16)
        k_p = norm_mm(mem2d, g_mem[l], w_xk_b, l, tm=tm_mem, tn=dx)
        v_p = norm_mm(mem2d, g_mem[l], w_xv_b, l, tm=tm_mem, tn=dx)
        o_p = attn_prompt(q, k_p.reshape(nb, mem_len, dx), v_p.reshape(nb, mem_len, dx),
                          n_seq=nb, seq=seq, tq=tq)
        q_s = jnp.swapaxes(q[t_prompt:].reshape(td, ns, dx), 0, 1).reshape(ns, td * X_HEADS, GROUP)
        o_s = attn_sample(q_s, cache_k, cache_v, l, sb=_tile(ns, 8, 1))
        o_s = jnp.swapaxes(o_s.reshape(ns, td, dx), 0, 1).reshape(t_s, dx)
        x, ht = attn_out(o_p, o_s, w_xo_b, l, x, g_peer[l])

        s2, theta, lam = peer_pre(ht, w_pq_t, keys_b, l, tb=tb_pre)
        x = peer_dense(ht, peer_u_b, peer_vt, l, s2, theta, lam, x, tb=tb, te=te)

        outs["mk"].append(k_p.reshape(nb, mem_len, X_HEADS, GROUP))
        outs["mv"].append(v_p.reshape(nb, mem_len, X_HEADS, GROUP))
        outs["pool_p"].append(pool_p)
        outs["sc_p"].append(sc_p)
        outs["cm_p"].append(cm_p)
        outs["pool_s"].append(_to_time_major(pool_s))
        outs["sc_s"].append(_to_time_major(sc_s))
        outs["cm_s"].append(_to_time_major(cm_s))
        outs["cv_s"].append(_to_time_major(vn_s))

    y_p, y_s = final_norm(x, g_final, tm=t_s)
    y_prompt = y_p.reshape(nb, seq, d)
    y_sample = _to_time_major(y_s.reshape(td, ns, d))
    st = lambda k: jnp.stack(outs[k])
    return (y_prompt, y_sample, st("mk"), st("mv"), st("pool_p"), st("sc_p"), st("cm_p"),
            st("pool_s"), st("sc_s"), st("cm_s"), st("cv_s"))
```

```python
import functools
import math

import jax
import jax.numpy as jnp
from jax import lax
from jax.experimental import pallas as pl
from jax.experimental.pallas import tpu as pltpu

EPS = 1e-6
GROUP = 128
POOL_WINDOWS = (2, 4, 8, 16)
POOL_STATE = max(POOL_WINDOWS) - 1
SC_WIDTH = 3
CM_WIDTH = 31
D_BR = 4 * GROUP
N_BRANCH = 4
CHUNK = 128
X_HEADS = 4
PEER_HEADS = 8
PEER_TOPK = 16
HALO = 32
LANES = 128
ROWS = 32
VMEM_LIMIT = 56 * 2 ** 20

BF16 = jnp.bfloat16
F32 = jnp.float32
NEG_INF = float("-inf")
LOG2E = math.log2(math.e)


def _tile(n, pref, mult=8):
    best = None
    for t in range(mult, min(n, pref) + 1, mult):
        if n % t == 0:
            best = t
    assert best is not None, (n, pref, mult)
    return best


def _params(*sem):
    return pltpu.CompilerParams(dimension_semantics=sem, vmem_limit_bytes=VMEM_LIMIT)


def _rms(x, g):
    ms = jnp.mean(x * x, axis=-1, keepdims=True)
    return x * lax.rsqrt(ms + EPS) * g


def _ln(x, g, b):
    mu = jnp.mean(x, axis=-1, keepdims=True)
    xc = x - mu
    var = jnp.mean(xc * xc, axis=-1, keepdims=True)
    return xc * lax.rsqrt(var + EPS) * g + b


def _sigmoid(x):
    return 1.0 / (1.0 + jnp.exp(-x))


def _dot(a, b):
    return jnp.dot(a, b, preferred_element_type=F32)


def _norm_mm_kernel(x_ref, g_ref, w_ref, o_ref, hn_ref):
    @pl.when(pl.program_id(1) == 0)
    def _():
        hn_ref[...] = _rms(x_ref[...], g_ref[...]).astype(BF16)

    o_ref[...] = _dot(hn_ref[...], w_ref[...])


def norm_mm(x, g, w, l, *, tm, tn):
    t, d = x.shape
    n = w.shape[2]
    return pl.pallas_call(
        _norm_mm_kernel,
        grid=(t // tm, n // tn),
        in_specs=[pl.BlockSpec((tm, d), lambda i, j: (i, 0)),
                  pl.BlockSpec((1, d), lambda i, j: (0, 0)),
                  pl.BlockSpec((None, d, tn), lambda i, j: (l, 0, j))],
        out_specs=pl.BlockSpec((tm, tn), lambda i, j: (i, j)),
        out_shape=jax.ShapeDtypeStruct((t, n), F32),
        scratch_shapes=[pltpu.VMEM((tm, d), BF16)],
        compiler_params=_params("parallel", "arbitrary"),
        name="norm_mm",
    )(x, g.reshape(1, d), w)


def _rows_of(i, n_head, head_ref, tail_ref):
    return jnp.where(i < n_head, head_ref[...], tail_ref[...])


def _attn_out_kernel(a_ref, at_ref, w_ref, r_ref, g_ref, o_ref, ht_ref, *, n_head):
    a = _rows_of(pl.program_id(0), n_head, a_ref, at_ref)
    x = r_ref[...] + _dot(a.astype(BF16), w_ref[...])
    o_ref[...] = x
    ht_ref[...] = _rms(x, g_ref[...]).T.astype(BF16)


def attn_out(a, a_tail, w, l, res, g):
    t, d = res.shape
    tm, k = a_tail.shape
    n_head = a.shape[0] // tm
    assert a.shape[0] == n_head * tm and t == (n_head + 1) * tm and tm % LANES == 0
    row = pl.BlockSpec((tm, d), lambda i: (i, 0))
    return pl.pallas_call(
        functools.partial(_attn_out_kernel, n_head=n_head),
        grid=(t // tm,),
        in_specs=[pl.BlockSpec((tm, k), lambda i: (jnp.minimum(i, n_head - 1), 0)),
                  pl.BlockSpec((tm, k), lambda i: (0, 0)),
                  pl.BlockSpec((None, k, d), lambda i: (l, 0, 0)),
                  row, pl.BlockSpec((1, d), lambda i: (0, 0))],
        out_specs=[row, pl.BlockSpec((d, tm), lambda i: (0, i))],
        out_shape=[jax.ShapeDtypeStruct((t, d), F32), jax.ShapeDtypeStruct((d, t), BF16)],
        compiler_params=_params("parallel"),
        name="attn_out",
    )(a, a_tail, w, res, g.reshape(1, d))


def _pool_branch(window_sums, a, cnts, poolw_ref, pscale):
    outs = []
    for gi in range(len(POOL_WINDOWS)):
        sl = slice(gi * GROUP, (gi + 1) * GROUP)
        pg = window_sums[gi] / cnts[gi] - a[:, sl]
        outs.append(_dot(pg.astype(BF16), poolw_ref[gi]))
    return jnp.concatenate(outs, axis=1) * pscale


def _mixer_prompt_kernel(z_ref, poolw_ref, pscale_ref, scw_ref, cmw_ref, cmb_ref, cmg_ref,
                         cmbeta_ref, sgg_ref, sgbeta_ref, sgw_ref, sgbias_ref,
                         p_ref, pool_o, sc_o, cm_o, ext_ref, shift_ref, *, ts):
    tb = pl.program_id(1)
    h0 = HALO

    @pl.when(tb == 0)
    def _():
        ext_ref[:, 0:h0, :] = jnp.zeros((3, h0, D_BR), F32)

    a = z_ref[:, 0:D_BR]
    bg = z_ref[:, D_BR:2 * D_BR]
    gated = z_ref[:, 2 * D_BR:3 * D_BR] * z_ref[:, 3 * D_BR:4 * D_BR]
    glu = z_ref[:, 4 * D_BR:5 * D_BR] * _sigmoid(z_ref[:, 5 * D_BR:6 * D_BR])
    ext_ref[0, h0:h0 + ts, :] = a
    ext_ref[1, h0:h0 + ts, :] = gated
    ext_ref[2, h0:h0 + ts, :] = glu

    pos = tb * ts + lax.broadcasted_iota(jnp.int32, (ts, 1), 0)
    sums, cnts = [], []
    for gi, w in enumerate(POOL_WINDOWS):
        sl = slice(gi * GROUP, (gi + 1) * GROUP)
        s = a[:, sl]
        for j in range(1, w):
            s = s + ext_ref[0, h0 - j:h0 - j + ts, sl]
        sums.append(s)
        cnts.append(jnp.minimum(w, pos + 1).astype(F32))
    p_ref[:, 0:D_BR] = _pool_branch(sums, a, cnts, poolw_ref, pscale_ref[...]).astype(BF16)

    conv = scw_ref[SC_WIDTH - 1:SC_WIDTH, :] * gated
    for k in range(SC_WIDTH - 1):
        off = h0 - (SC_WIDTH - 1) + k
        conv = conv + scw_ref[k:k + 1, :] * ext_ref[1, off:off + ts, :]
    p_ref[:, D_BR:2 * D_BR] = (bg * conv).astype(BF16)

    span = ts + h0 - 8
    for rho in range(8):
        shift_ref[rho] = ext_ref[2, rho:rho + span, :]
    c = cmw_ref[CM_WIDTH - 1:CM_WIDTH, :] * glu
    for k in range(CM_WIDTH - 1):
        q, rho = divmod(h0 - (CM_WIDTH - 1) + k, 8)
        c = c + cmw_ref[k:k + 1, :] * shift_ref[rho, 8 * q:8 * q + ts, :]
    c = _ln(c + cmb_ref[...], cmg_ref[...], cmbeta_ref[...])
    p_ref[:, 2 * D_BR:3 * D_BR] = (c * _sigmoid(c)).astype(BF16)

    vn = _ln(z_ref[:, 7 * D_BR:8 * D_BR], sgg_ref[...], sgbeta_ref[...])
    row = lax.broadcasted_iota(jnp.int32, (CHUNK, CHUNK), 0)
    col = lax.broadcasted_iota(jnp.int32, (CHUNK, CHUNK), 1)
    tril = row >= col
    for gi in range(D_BR // GROUP):
        sl = slice(gi * GROUP, (gi + 1) * GROUP)
        wg = jnp.where(tril, sgw_ref[gi], 0.0).astype(BF16)
        for ci in range(ts // CHUNK):
            rows = slice(ci * CHUNK, (ci + 1) * CHUNK)
            mixed = _dot(wg, vn[rows, sl].astype(BF16)) + sgbias_ref[:, sl]
            p_ref[rows, 3 * D_BR + gi * GROUP:3 * D_BR + (gi + 1) * GROUP] = (
                z_ref[rows, 6 * D_BR + gi * GROUP:6 * D_BR + (gi + 1) * GROUP] * mixed).astype(BF16)

    end = h0 + ts
    pool_o[0] = ext_ref[0, end - POOL_STATE:end, :]
    sc_o[0] = ext_ref[1, end - (SC_WIDTH - 1):end, :]
    cm_o[0] = ext_ref[2, end - (CM_WIDTH - 1):end, :]
    ext_ref[:, 0:h0, :] = ext_ref[:, ts:ts + h0, :]


def mixer_prompt(z, lp, *, n_seq, seq, ts):
    nb = seq // ts
    full = lambda *shape: pl.BlockSpec(shape, lambda n, t: (0,) * len(shape))
    state = lambda rows: pl.BlockSpec((1, rows, D_BR), lambda n, t: (n, 0, 0))
    return pl.pallas_call(
        functools.partial(_mixer_prompt_kernel, ts=ts),
        grid=(n_seq, nb),
        in_specs=[pl.BlockSpec((ts, 8 * D_BR), lambda n, t: (n * nb + t, 0)),
                  full(4, GROUP, GROUP), full(1, D_BR), full(SC_WIDTH, D_BR),
                  full(CM_WIDTH, D_BR), full(1, D_BR), full(1, D_BR), full(1, D_BR),
                  full(1, D_BR), full(1, D_BR), full(4, CHUNK, CHUNK), full(CHUNK, D_BR)],
        out_specs=[pl.BlockSpec((ts, N_BRANCH * D_BR), lambda n, t: (n * nb + t, 0)),
                   state(POOL_STATE), state(SC_WIDTH - 1), state(CM_WIDTH - 1)],
        out_shape=[jax.ShapeDtypeStruct((n_seq * seq, N_BRANCH * D_BR), BF16),
                   jax.ShapeDtypeStruct((n_seq, POOL_STATE, D_BR), F32),
                   jax.ShapeDtypeStruct((n_seq, SC_WIDTH - 1, D_BR), F32),
                   jax.ShapeDtypeStruct((n_seq, CM_WIDTH - 1, D_BR), F32)],
        scratch_shapes=[pltpu.VMEM((3, HALO + ts, D_BR), F32),
                        pltpu.VMEM((8, HALO + ts - 8, D_BR), F32)],
        compiler_params=_params("parallel", "arbitrary"),
        name="mixer_prompt",
    )(z, lp["pool_w"], lp["pool_scale"], lp["sc_w"], lp["cm_w"], lp["cm_b"], lp["cm_ln_g"],
      lp["cm_ln_b"], lp["sg_ln_g"], lp["sg_ln_b"], lp["sg_w"], lp["sg_bias"])


def _mixer_sample_kernel(z_ref, pool_ref, sc_ref, cm_ref, poolw_ref, pscale_ref, scw_ref, cmw_ref,
                         cmb_ref, cmg_ref, cmbeta_ref, sgg_ref, sgbeta_ref, sgwts_ref, sgbias_ref,
                         p_ref, pool_o, sc_o, cm_o, vn_o, *, td, ns, start_pos):
    rows = lambda t: slice(t * ns, (t + 1) * ns)
    col = lambda k: slice(k * D_BR, (k + 1) * D_BR)

    ext_a = [pool_ref[j] for j in range(POOL_STATE)] + [z_ref[rows(t), col(0)] for t in range(td)]
    a_all = jnp.concatenate(ext_a[POOL_STATE:], axis=0)
    sums, cnts = [], []
    for gi, w in enumerate(POOL_WINDOWS):
        sl = slice(gi * GROUP, (gi + 1) * GROUP)
        per_t = []
        for t in range(td):
            s = ext_a[POOL_STATE + t][:, sl]
            for j in range(1, w):
                s = s + ext_a[POOL_STATE + t - j][:, sl]
            per_t.append(s)
        sums.append(jnp.concatenate(per_t, axis=0))
        cnt = jnp.concatenate(
            [jnp.full((ns, 1), float(min(w, start_pos + t + 1)), F32) for t in range(td)], axis=0)
        cnts.append(cnt)
    p_ref[:, col(0)] = _pool_branch(sums, a_all, cnts, poolw_ref, pscale_ref[...]).astype(BF16)
    for j in range(POOL_STATE):
        pool_o[j] = ext_a[td + j]

    ext_b = [sc_ref[j] for j in range(SC_WIDTH - 1)]
    ext_b += [z_ref[rows(t), col(2)] * z_ref[rows(t), col(3)] for t in range(td)]
    for t in range(td):
        conv = scw_ref[0:1, :] * ext_b[t]
        for k in range(1, SC_WIDTH):
            conv = conv + scw_ref[k:k + 1, :] * ext_b[t + k]
        p_ref[rows(t), col(1)] = (z_ref[rows(t), col(1)] * conv).astype(BF16)
    for j in range(SC_WIDTH - 1):
        sc_o[j] = ext_b[td + j]

    ext_c = [cm_ref[j] for j in range(CM_WIDTH - 1)]
    ext_c += [z_ref[rows(t), col(4)] * _sigmoid(z_ref[rows(t), col(5)]) for t in range(td)]
    for t in range(td):
        c = cmw_ref[0:1, :] * ext_c[t]
        for k in range(1, CM_WIDTH):
            c = c + cmw_ref[k:k + 1, :] * ext_c[t + k]
        c = _ln(c + cmb_ref[...], cmg_ref[...], cmbeta_ref[...])
        p_ref[rows(t), col(2)] = (c * _sigmoid(c)).astype(BF16)
    for j in range(CM_WIDTH - 1):
        cm_o[j] = ext_c[td + j]

    vn = [_ln(z_ref[rows(t), col(7)], sgg_ref[...], sgbeta_ref[...]) for t in range(td)]
    for t in range(td):
        vn_o[t] = vn[t]
        mixed = sgbias_ref[t:t + 1, :]
        for s in range(t + 1):
            mixed = mixed + sgwts_ref[t, s:s + 1, :] * vn[s]
        p_ref[rows(t), col(3)] = (z_ref[rows(t), col(6)] * mixed).astype(BF16)


def mixer_sample(z, pool_tm, sc_tm, cm_tm, lp, *, td, ns, t_prompt, start_pos):
    blk = t_prompt // (td * ns)
    full = lambda *shape: pl.BlockSpec(shape, lambda i: (0,) * len(shape))
    tm_shape = lambda rows: jax.ShapeDtypeStruct((rows, ns, D_BR), F32)
    return pl.pallas_call(
        functools.partial(_mixer_sample_kernel, td=td, ns=ns, start_pos=start_pos),
        grid=(1,),
        in_specs=[pl.BlockSpec((td * ns, 8 * D_BR), lambda i: (blk, 0)),
                  full(POOL_STATE, ns, D_BR), full(SC_WIDTH - 1, ns, D_BR),
                  full(CM_WIDTH - 1, ns, D_BR),
                  full(4, GROUP, GROUP), full(1, D_BR), full(SC_WIDTH, D_BR),
                  full(CM_WIDTH, D_BR), full(1, D_BR), full(1, D_BR), full(1, D_BR),
                  full(1, D_BR), full(1, D_BR), full(td, td, D_BR), full(CHUNK, D_BR)],
        out_specs=[full(td * ns, N_BRANCH * D_BR),
                   full(POOL_STATE, ns, D_BR), full(SC_WIDTH - 1, ns, D_BR),
                   full(CM_WIDTH - 1, ns, D_BR), full(td, ns, D_BR)],
        out_shape=[jax.ShapeDtypeStruct((td * ns, N_BRANCH * D_BR), BF16), tm_shape(POOL_STATE),
                   tm_shape(SC_WIDTH - 1), tm_shape(CM_WIDTH - 1), tm_shape(td)],
        compiler_params=_params("arbitrary"),
        name="mixer_sample",
    )(z, pool_tm, sc_tm, cm_tm, lp["pool_w"], lp["pool_scale"], lp["sc_w"], lp["cm_w"],
      lp["cm_b"], lp["cm_ln_g"], lp["cm_ln_b"], lp["sg_ln_g"], lp["sg_ln_b"], lp["sg_wts"],
      lp["sg_bias"])


def _norm_in_kernel(x_ref, g_ref, w_ref, z_ref, hn_ref):
    @pl.when(pl.program_id(1) == 0)
    def _():
        hn_ref[...] = _rms(x_ref[...], g_ref[...]).astype(BF16)

    z_ref[...] = _dot(hn_ref[...], w_ref[...])


def norm_in(x, g, w, l, n_cols, *, tm, tn):
    t, d = x.shape
    return pl.pallas_call(
        _norm_in_kernel,
        grid=(t // tm, n_cols // tn),
        in_specs=[pl.BlockSpec((tm, d), lambda i, j: (i, 0)),
                  pl.BlockSpec((1, d), lambda i, j: (0, 0)),
                  pl.BlockSpec((None, d, tn), lambda i, j: (l, 0, j))],
        out_specs=[pl.BlockSpec((tm, tn), lambda i, j: (i, j)),
                   pl.BlockSpec((tm, d), lambda i, j: (i, 0))],
        out_shape=[jax.ShapeDtypeStruct((t, n_cols), F32), jax.ShapeDtypeStruct((t, d), BF16)],
        compiler_params=_params("parallel", "arbitrary"),
        name="norm_in",
    )(x, g.reshape(1, d), w)


def _gate_out_kernel(hn_ref, p_ref, pt_ref, g0_ref, g1_ref, g2_ref, g3_ref, w0_ref, w1_ref,
                     w2_ref, w3_ref, b_ref, wo_ref, x_ref, o_ref, merged_ref, *, n_head, tn):
    j = pl.program_id(1)
    p = _rows_of(pl.program_id(0), n_head, p_ref, pt_ref)
    hn = hn_ref[...]
    acc = None
    for i, (g_ref, w_ref) in enumerate(((g0_ref, w0_ref), (g1_ref, w1_ref), (g2_ref, w2_ref),
                                        (g3_ref, w3_ref))):
        gate = _sigmoid(_dot(hn, g_ref[...]) + b_ref[i:i + 1, :])
        term = gate * _dot(p[:, i * D_BR:(i + 1) * D_BR].astype(BF16), w_ref[...])
        acc = term if acc is None else acc + term
    merged_ref[:, pl.ds(pl.multiple_of(j * tn, tn), tn)] = acc.astype(BF16)

    @pl.when(j == pl.num_programs(1) - 1)
    def _():
        o_ref[...] = x_ref[...] + _dot(merged_ref[...], wo_ref[...])


def gate_out(hn, p_head, p_tail, w_in, w_outs, b_gate, w_o, l, x, *, tn):
    t, d = x.shape
    tm = p_tail.shape[0]
    n_head = p_head.shape[0] // tm
    assert p_head.shape[0] == n_head * tm and t == (n_head + 1) * tm
    first = N_BRANCH * 2 * D_BR // tn
    gate_spec = lambda i: pl.BlockSpec((None, d, tn), lambda r, c: (l, 0, first + i * (d // tn) + c))
    w_spec = pl.BlockSpec((None, D_BR, tn), lambda r, c: (l, 0, c))
    row = pl.BlockSpec((tm, d), lambda r, c: (r, 0))
    return pl.pallas_call(
        functools.partial(_gate_out_kernel, n_head=n_head, tn=tn),
        grid=(t // tm, d // tn),
        in_specs=[row,
                  pl.BlockSpec((tm, N_BRANCH * D_BR), lambda r, c: (jnp.minimum(r, n_head - 1), 0)),
                  pl.BlockSpec((tm, N_BRANCH * D_BR), lambda r, c: (0, 0)),
                  gate_spec(0), gate_spec(1), gate_spec(2), gate_spec(3),
                  w_spec, w_spec, w_spec, w_spec,
                  pl.BlockSpec((N_BRANCH, tn), lambda r, c: (0, c)),
                  pl.BlockSpec((None, d, d), lambda r, c: (l, 0, 0), pipeline_mode=pl.Buffered(1)),
                  row],
        out_specs=row,
        out_shape=jax.ShapeDtypeStruct((t, d), F32),
        scratch_shapes=[pltpu.VMEM((tm, d), BF16)],
        compiler_params=_params("parallel", "arbitrary"),
        name="gate_out",
    )(hn, p_head, p_tail, w_in, w_in, w_in, w_in, *w_outs, b_gate, w_o, x)


def _softmax_rows(s):
    m = jnp.max(s, axis=-1, keepdims=True)
    p = jnp.exp(s - m)
    return p / jnp.sum(p, axis=-1, keepdims=True)


def _attn_prompt_kernel(q_ref, k_ref, v_ref, o_ref, *, scale):
    for h in range(X_HEADS):
        sl = slice(h * GROUP, (h + 1) * GROUP)
        s = lax.dot_general(q_ref[:, sl].astype(BF16), k_ref[0, :, sl].astype(BF16),
                            (((1,), (1,)), ((), ())), preferred_element_type=F32) * scale
        o_ref[:, sl] = _dot(_softmax_rows(s).astype(BF16), v_ref[0, :, sl].astype(BF16))


def attn_prompt(q, k, v, *, n_seq, seq, tq):
    dx = q.shape[1]
    mem = k.shape[1]
    nb = seq // tq
    return pl.pallas_call(
        functools.partial(_attn_prompt_kernel, scale=GROUP ** -0.5),
        grid=(n_seq, nb),
        in_specs=[pl.BlockSpec((tq, dx), lambda n, t: (n * nb + t, 0)),
                  pl.BlockSpec((1, mem, dx), lambda n, t: (n, 0, 0)),
                  pl.BlockSpec((1, mem, dx), lambda n, t: (n, 0, 0))],
        out_specs=pl.BlockSpec((tq, dx), lambda n, t: (n * nb + t, 0)),
        out_shape=jax.ShapeDtypeStruct((n_seq * seq, dx), F32),
        compiler_params=_params("parallel", "parallel"),
        name="attn_prompt",
    )(q, k, v)


def _attn_sample_kernel(q_ref, k_ref, v_ref, o_ref, *, scale):
    rows, cols = q_ref.shape[1], k_ref.shape[1]
    row_head = lax.broadcasted_iota(jnp.int32, (rows, cols), 0) % X_HEADS
    col_head = lax.broadcasted_iota(jnp.int32, (rows, cols), 1) % X_HEADS
    s = jnp.einsum("bqd,bkd->bqk", q_ref[...].astype(BF16), k_ref[...].astype(BF16),
                   preferred_element_type=F32) * scale
    s = jnp.where((row_head == col_head)[None], s, NEG_INF)
    o_ref[...] = jnp.einsum("bqk,bkd->bqd", _softmax_rows(s).astype(BF16),
                            v_ref[...].astype(BF16), preferred_element_type=F32)


def attn_sample(q, k, v, l, *, sb):
    n_seq, rows, hd = q.shape
    kv_spec = pl.BlockSpec((None, sb, k.shape[2], hd), lambda i: (l, i, 0, 0))
    return pl.pallas_call(
        functools.partial(_attn_sample_kernel, scale=GROUP ** -0.5),
        grid=(n_seq // sb,),
        in_specs=[pl.BlockSpec((sb, rows, hd), lambda i: (i, 0, 0)), kv_spec, kv_spec],
        out_specs=pl.BlockSpec((sb, rows, hd), lambda i: (i, 0, 0)),
        out_shape=jax.ShapeDtypeStruct((n_seq, rows, hd), F32),
        compiler_params=_params("parallel"),
        name="attn_sample",
    )(q, k, v)


def _merge_exchange_network(n):
    pairs, p = [], 1
    while p < n:
        k = p
        while k >= 1:
            for j in range(k % p, n - k, 2 * k):
                for i in range(min(k, n - j - k)):
                    if (i + j) // (2 * p) == (i + j + k) // (2 * p):
                        pairs.append((i + j, i + j + k))
            k //= 2
        p *= 2
    return pairs


def _sort_descending(values):
    n = pl.next_power_of_2(len(values))
    vals = list(values) + [None] * (n - len(values))
    for i, j in _merge_exchange_network(n):
        hi, lo = vals[i], vals[j]
        if lo is None:
            continue
        if hi is None:
            vals[i], vals[j] = lo, None
        else:
            vals[i], vals[j] = jnp.maximum(hi, lo), jnp.minimum(hi, lo)
    return vals[:len(values)]


_CAND = [(i, j) for i in range(PEER_TOPK + 1) for j in range(PEER_TOPK + 1)
         if (i + 1) * (j + 1) <= PEER_TOPK + 1]


def _peer_pre_kernel(ht_ref, wq_ref, keys_ref, s2_o, th_o, lam_o, q_scr, s_scr, v_scr, *, tb):
    nk = GROUP
    q_scr[...] = _dot(wq_ref[...], ht_ref[...]).astype(BF16)
    for hp in range(2 * PEER_HEADS):
        s_scr[hp] = _dot(keys_ref[hp], q_scr[hp * nk:(hp + 1) * nk, :])

    idx = lax.broadcasted_iota(jnp.int32, (nk, LANES), 0)

    def chunk(ci, carry):
        lanes = pl.ds(pl.multiple_of(ci * LANES, LANES), LANES)
        tied = jnp.zeros((1, LANES), F32)
        for hp in range(2 * PEER_HEADS):
            h, p = divmod(hp, 2)
            cur = s_scr[hp, :, lanes]
            stacks = _sort_descending([cur[8 * g:8 * (g + 1), :] for g in range(nk // 8)])
            for r in range(PEER_TOPK + 1):
                m = jnp.max(stacks[0], axis=0, keepdims=True)
                v_scr[p, r, h:h + 1, lanes] = m
                depth = PEER_TOPK - r
                if depth > 0:
                    hit = stacks[0] == m
                    below = stacks[1:] + [jnp.full_like(stacks[0], NEG_INF)]
                    stacks = [jnp.where(hit, below[q], stacks[q])
                              for q in range(min(depth, len(stacks)))]
            n_ge = jnp.sum(jnp.where(cur >= m, 1.0, 0.0), axis=0, keepdims=True)
            tied = jnp.maximum(tied, jnp.where(n_ge != PEER_TOPK + 1.0, 1.0, 0.0))

        @pl.when(jnp.max(tied) > 0.0)
        def _():
            for hp in range(2 * PEER_HEADS):
                h, p = divmod(hp, 2)
                cur = s_scr[hp, :, lanes]
                for r in range(PEER_TOPK + 1):
                    m = jnp.max(cur, axis=0, keepdims=True)
                    v_scr[p, r, h:h + 1, lanes] = m
                    if r < PEER_TOPK:
                        first = jnp.min(jnp.where(cur == m, idx, nk), axis=0, keepdims=True)
                        cur = jnp.where(idx == first, NEG_INF, cur)

        v1 = [v_scr[0, r, :, lanes] for r in range(PEER_TOPK + 1)]
        v2 = [v_scr[1, r, :, lanes] for r in range(PEER_TOPK + 1)]
        cands = [v1[i] + v2[j] for i, j in _CAND]
        ranked = _sort_descending(cands)
        kth, nxt = ranked[PEER_TOPK - 1], ranked[PEER_TOPK]
        tau = 0.5 * (kth + nxt)
        top = cands[0]
        zsum = jnp.zeros_like(top)
        for ck in cands:
            zsum = zsum + jnp.where(ck >= tau, jnp.exp(ck - top), 0.0)
        shift = top + jnp.log(zsum)
        for h in range(PEER_HEADS):
            s1 = s_scr[2 * h, :, lanes]
            th_o[h, :, lanes] = (tau[h:h + 1, :] - s1) * LOG2E
            lam_o[h, :, lanes] = (s1 - shift[h:h + 1, :]) * LOG2E - 0.5
            s2_o[h * nk:(h + 1) * nk, lanes] = s_scr[2 * h + 1, :, lanes] * LOG2E
        return carry

    lax.fori_loop(0, tb // LANES, chunk, 0)


def peer_pre(ht, wq_t, keys, l, *, tb):
    d, t = ht.shape
    nq = wq_t.shape[1]
    hk = PEER_HEADS * GROUP
    flat = pl.BlockSpec((hk, tb), lambda i: (0, i))
    cube = pl.BlockSpec((PEER_HEADS, GROUP, tb), lambda i: (0, 0, i))
    return pl.pallas_call(
        functools.partial(_peer_pre_kernel, tb=tb),
        grid=(t // tb,),
        in_specs=[pl.BlockSpec((d, tb), lambda i: (0, i)),
                  pl.BlockSpec((None, nq, d), lambda i: (l, 0, 0)),
                  pl.BlockSpec((None, 2 * PEER_HEADS, GROUP, GROUP), lambda i: (l, 0, 0, 0))],
        out_specs=[flat, cube, cube],
        out_shape=[jax.ShapeDtypeStruct((hk, t), F32),
                   jax.ShapeDtypeStruct((PEER_HEADS, GROUP, t), F32),
                   jax.ShapeDtypeStruct((PEER_HEADS, GROUP, t), F32)],
        scratch_shapes=[pltpu.VMEM((nq, tb), BF16),
                        pltpu.VMEM((2 * PEER_HEADS, GROUP, tb), F32),
                        pltpu.VMEM((2, PEER_TOPK + 1, PEER_HEADS, tb), F32)],
        compiler_params=_params("parallel"),
        name="peer_pre",
    )(ht, wq_t, keys)


def _peer_dense_kernel(ht_ref, htn_ref, ua_ref, ub_ref, vt_ref, s2_ref, th_ref, lam_ref, x_ref,
                       o_ref, acc_ref, at_ref, gw_ref, *, tb, te, ne):
    s = pl.program_id(0)
    e_cur = jnp.maximum(s - 1, 0) % ne
    half = te // 2
    tbh = tb // 2
    n_a = half // GROUP
    piece = lambda k, c: 2 * k + c
    cols = lambda c: slice(c * tbh, (c + 1) * tbh)

    @pl.when(s == 0)
    def _():
        at_ref[0:2] = jnp.zeros((2, half, tbh), F32)

    @pl.when(e_cur == 0)
    def _():
        acc_ref[...] = jnp.zeros_like(acc_ref)

    def pre_activations(k, c, u_ref, h_ref):
        at_ref[piece(k, c)] = _dot(u_ref[...], h_ref[:, cols(c)])

    def weighted_activations(k, c):
        p = piece(k, c)
        for cl in range(tbh // LANES):
            lanes = slice(cl * LANES, (cl + 1) * LANES)
            glanes = slice(c * tbh + cl * LANES, c * tbh + (cl + 1) * LANES)
            for r in range(GROUP // ROWS):
                w = [jnp.zeros((ROWS, LANES), F32) for _ in range(n_a)]
                for h in range(PEER_HEADS):
                    s2 = s2_ref[h * GROUP + r * ROWS:h * GROUP + (r + 1) * ROWS, glanes]
                    for a in range(n_a):
                        ag = k * n_a + a
                        keep = s2 >= th_ref[h, ag:ag + 1, glanes]
                        w[a] = w[a] + jnp.where(keep, jnp.exp2(s2 + lam_ref[h, ag:ag + 1, glanes]), 0.0)
                for a in range(n_a):
                    rows = slice(a * GROUP + r * ROWS, a * GROUP + (r + 1) * ROWS)
                    z = at_ref[p, rows, lanes]
                    gw_ref[p, rows, lanes] = (w[a] * (z * (1.0 + lax.erf(z)))).astype(BF16)

    def accumulate(k, c):
        acc_ref[:, cols(c)] += _dot(vt_ref[:, k * half:(k + 1) * half], gw_ref[piece(k, c)])

    pre_activations(1, 0, ub_ref, ht_ref)
    pre_activations(1, 1, ub_ref, ht_ref)
    for c in range(2):
        weighted_activations(0, c)
        accumulate(0, c)
        pre_activations(0, c, ua_ref, htn_ref)
    for c in range(2):
        weighted_activations(1, c)
        accumulate(1, c)

    @pl.when(jnp.logical_and(e_cur == ne - 1, s > 0))
    def _():
        o_ref[...] = x_ref[...] + acc_ref[...].T


def peer_dense(ht, u, vt, l, s2, theta, lam, x, *, tb, te):
    d, t = ht.shape
    ne = u.shape[1] // te
    assert vt.shape[1:] == (ne, d, te)
    n_tiles = (t // tb) * ne
    half = te // 2
    a_blk = te // GROUP
    once = pl.Buffered(1)
    cur = lambda s: jnp.clip(s - 1, 0, n_tiles - 1)
    nxt = lambda s: jnp.minimum(s, n_tiles - 1)
    return pl.pallas_call(
        functools.partial(_peer_dense_kernel, tb=tb, te=te, ne=ne),
        grid=(n_tiles + 1,),
        in_specs=[pl.BlockSpec((d, tb), lambda s: (0, cur(s) // ne), pipeline_mode=once),
                  pl.BlockSpec((d, tb), lambda s: (0, nxt(s) // ne), pipeline_mode=once),
                  pl.BlockSpec((None, half, d), lambda s: (l, 2 * (nxt(s) % ne), 0)),
                  pl.BlockSpec((None, half, d), lambda s: (l, 2 * (cur(s) % ne) + 1, 0)),
                  pl.BlockSpec((None, None, d, te), lambda s: (l, cur(s) % ne, 0, 0)),
                  pl.BlockSpec((PEER_HEADS * GROUP, tb), lambda s: (0, cur(s) // ne),
                               pipeline_mode=once),
                  pl.BlockSpec((PEER_HEADS, a_blk, tb), lambda s: (0, cur(s) % ne, cur(s) // ne)),
                  pl.BlockSpec((PEER_HEADS, a_blk, tb), lambda s: (0, cur(s) % ne, cur(s) // ne)),
                  pl.BlockSpec((tb, d), lambda s: (cur(s) // ne, 0), pipeline_mode=once)],
        out_specs=pl.BlockSpec((tb, d), lambda s: (cur(s) // ne, 0)),
        out_shape=jax.ShapeDtypeStruct((t, d), F32),
        scratch_shapes=[pltpu.VMEM((d, tb), F32), pltpu.VMEM((4, half, tb // 2), F32),
                        pltpu.VMEM((4, half, tb // 2), BF16)],
        compiler_params=_params("arbitrary"),
        name="peer_dense",
    )(ht, ht, u, u, vt, s2, theta, lam, x)


def _final_norm_kernel(x_ref, g_ref, head_o, tail_o, *, n_head):
    i = pl.program_id(0)
    y = _rms(x_ref[...], g_ref[...])

    @pl.when(i < n_head)
    def _():
        head_o[...] = y

    @pl.when(i == n_head)
    def _():
        tail_o[...] = y


def final_norm(x, g, *, tm):
    t, d = x.shape
    n_head = t // tm - 1
    assert t == (n_head + 1) * tm
    return pl.pallas_call(
        functools.partial(_final_norm_kernel, n_head=n_head),
        grid=(n_head + 1,),
        in_specs=[pl.BlockSpec((tm, d), lambda i: (i, 0)), pl.BlockSpec((1, d), lambda i: (0, 0))],
        out_specs=[pl.BlockSpec((tm, d), lambda i: (jnp.minimum(i, n_head - 1), 0)),
                   pl.BlockSpec((tm, d), lambda i: (0, 0))],
        out_shape=[jax.ShapeDtypeStruct((n_head * tm, d), F32),
                   jax.ShapeDtypeStruct((tm, d), F32)],
        compiler_params=_params("arbitrary"),
        name="final_norm",
    )(x, g.reshape(1, d))


def _to_time_major(s):
    return jnp.swapaxes(s, 0, 1)


def kernel(x_prompt, x_sample, mem_prompt, cache_mem_k, cache_mem_v, state_pool, state_sconv, state_cconv, g_mix, w_in, b_gate, pool_w, pool_scale, w_pool_out, sc_w, w_sc_out, cm_w, cm_b, cm_ln_g, cm_ln_b, w_cm_out, sg_ln_g, sg_ln_b, sg_w, sg_b, w_sg_out, w_o, g_x, g_mem, w_xq, w_xk, w_xv, w_xo, g_peer, w_pq, peer_keys, peer_u, peer_v, g_final):
    nb, seq, d = x_prompt.shape
    ns, td, _ = x_sample.shape
    depth = w_in.shape[0]
    mem_len = mem_prompt.shape[1]
    past_len = 16384
    t_prompt = nb * seq
    t_all = t_prompt + ns * td
    dx = X_HEADS * GROUP

    tm_big = _tile(t_all, 1088, 16)
    t_s = ns * td
    assert t_prompt % t_s == 0
    tn_in = _tile(w_in.shape[2], 1024, 128)
    tn_d = _tile(d, 512, 128)
    tn_gate = _tile(d, 256, 128)
    ts = _tile(seq, 256, CHUNK)
    tq = _tile(seq, 512, 8)
    tb = _tile(t_all, 512, LANES)
    tb_pre = _tile(t_all, 256, LANES)
    te = _tile(peer_u.shape[1], 1024, 8 * GROUP)

    x = jnp.concatenate([x_prompt.reshape(t_prompt, d),
                         _to_time_major(x_sample).reshape(ns * td, d)], axis=0)
    mem2d = mem_prompt.reshape(nb * mem_len, d)
    col1 = lambda v: v.reshape(1, -1)
    cache_k = cache_mem_k.reshape(depth, ns, mem_len * X_HEADS, GROUP)
    cache_v = cache_mem_v.reshape(depth, ns, mem_len * X_HEADS, GROUP)

    bf = lambda w: w.astype(BF16)
    w_in_b, w_o_b, w_xq_b, w_xk_b, w_xv_b, w_xo_b = map(bf, (w_in, w_o, w_xq, w_xk, w_xv, w_xo))
    w_branch_b = tuple(map(bf, (w_pool_out, w_sc_out, w_cm_out, w_sg_out)))
    w_pq_t = bf(jnp.swapaxes(w_pq, 1, 2))
    keys_b = bf(peer_keys.reshape(depth, 2 * PEER_HEADS, GROUP, GROUP))
    peer_u_b = bf(peer_u * math.sqrt(0.5))
    peer_vt = bf(jnp.swapaxes(peer_v.reshape(depth, -1, te, d), 2, 3))

    outs = {k: [] for k in ("mk", "mv", "pool_p", "sc_p", "cm_p", "pool_s", "sc_s", "cm_s", "cv_s")}
    for l in range(depth):
        lp = dict(
            pool_w=pool_w[l].astype(BF16), pool_scale=col1(pool_scale[l]), sc_w=sc_w[l],
            cm_w=cm_w[l], cm_b=col1(cm_b[l]), cm_ln_g=col1(cm_ln_g[l]), cm_ln_b=col1(cm_ln_b[l]),
            sg_ln_g=col1(sg_ln_g[l]), sg_ln_b=col1(sg_ln_b[l]), sg_w=sg_w[l],
            sg_bias=jnp.repeat(sg_b[l].T, GROUP, axis=1),
            sg_wts=jnp.repeat(jnp.transpose(sg_w[l][:, :td, :td], (1, 2, 0)), GROUP, axis=2))

        z, hn = norm_in(x, g_mix[l], w_in_b, l, 2 * N_BRANCH * D_BR, tm=tm_big, tn=tn_in)
        p_p, pool_p, sc_p, cm_p = mixer_prompt(z, lp, n_seq=nb, seq=seq, ts=ts)
        p_s, pool_s, sc_s, cm_s, vn_s = mixer_sample(
            z, _to_time_major(state_pool[l]), _to_time_major(state_sconv[l]),
            _to_time_major(state_cconv[l]), lp, td=td, ns=ns, t_prompt=t_prompt, start_pos=past_len)
        x = gate_out(hn, p_p, p_s, w_in_b, w_branch_b, b_gate[l], w_o_b, l, x, tn=tn_gate)

        q = norm_mm(x, g_x[l], w_xq_b, l, tm=tm_big, tn=dx)
        tm_mem = _tile(nb * mem_len, 1024, 16)
        k_p = norm_mm(mem2d, g_mem[l], w_xk_b, l, tm=tm_mem, tn=dx)
        v_p = norm_mm(mem2d, g_mem[l], w_xv_b, l, tm=tm_mem, tn=dx)
        o_p = attn_prompt(q, k_p.reshape(nb, mem_len, dx), v_p.reshape(nb, mem_len, dx),
                          n_seq=nb, seq=seq, tq=tq)
        q_s = jnp.swapaxes(q[t_prompt:].reshape(td, ns, dx), 0, 1).reshape(ns, td * X_HEADS, GROUP)
        o_s = attn_sample(q_s, cache_k, cache_v, l, sb=_tile(ns, 8, 1))
        o_s = jnp.swapaxes(o_s.reshape(ns, td, dx), 0, 1).reshape(t_s, dx)
        x, ht = attn_out(o_p, o_s, w_xo_b, l, x, g_peer[l])

        s2, theta, lam = peer_pre(ht, w_pq_t, keys_b, l, tb=tb_pre)
        x = peer_dense(ht, peer_u_b, peer_vt, l, s2, theta, lam, x, tb=tb, te=te)

        outs["mk"].append(k_p.reshape(nb, mem_len, X_HEADS, GROUP))
        outs["mv"].append(v_p.reshape(nb, mem_len, X_HEADS, GROUP))
        outs["pool_p"].append(pool_p)
        outs["sc_p"].append(sc_p)
        outs["cm_p"].append(cm_p)
        outs["pool_s"].append(_to_time_major(pool_s))
        outs["sc_s"].append(_to_time_major(sc_s))
        outs["cm_s"].append(_to_time_major(cm_s))
        outs["cv_s"].append(_to_time_major(vn_s))

    y_p, y_s = final_norm(x, g_final, tm=t_s)
    y_prompt = y_p.reshape(nb, seq, d)
    y_sample = _to_time_major(y_s.reshape(td, ns, d))
    st = lambda k: jnp.stack(outs[k])
    return (y_prompt, y_sample, st("mk"), st("mv"), st("pool_p"), st("sc_p"), st("cm_p"),
            st("pool_s"), st("sc_s"), st("cm_s"), st("cv_s"))
```

```python
import functools
import math

import jax
import jax.numpy as jnp
from jax import lax
from jax.experimental import pallas as pl
from jax.experimental.pallas import tpu as pltpu

EPS = 1e-6
GROUP = 128
POOL_WINDOWS = (2, 4, 8, 16)
POOL_STATE = max(POOL_WINDOWS) - 1
SC_WIDTH = 3
CM_WIDTH = 31
D_BR = 4 * GROUP
N_BRANCH = 4
CHUNK = 128
X_HEADS = 4
PEER_HEADS = 8
PEER_TOPK = 16
HALO = 32
LANES = 128
ROWS = 32
VMEM_LIMIT = 56 * 2 ** 20

BF16 = jnp.bfloat16
F32 = jnp.float32
NEG_INF = float("-inf")
LOG2E = math.log2(math.e)


def _tile(n, pref, mult=8):
    best = None
    for t in range(mult, min(n, pref) + 1, mult):
        if n % t == 0:
            best = t
    assert best is not None, (n, pref, mult)
    return best


def _params(*sem):
    return pltpu.CompilerParams(dimension_semantics=sem, vmem_limit_bytes=VMEM_LIMIT)


def _rms(x, g):
    ms = jnp.mean(x * x, axis=-1, keepdims=True)
    return x * lax.rsqrt(ms + EPS) * g


def _ln(x, g, b):
    mu = jnp.mean(x, axis=-1, keepdims=True)
    xc = x - mu
    var = jnp.mean(xc * xc, axis=-1, keepdims=True)
    return xc * lax.rsqrt(var + EPS) * g + b


def _sigmoid(x):
    return 1.0 / (1.0 + jnp.exp(-x))


def _dot(a, b):
    return jnp.dot(a, b, preferred_element_type=F32)


def _norm_mm_kernel(x_ref, g_ref, w_ref, o_ref, hn_ref):
    @pl.when(pl.program_id(1) == 0)
    def _():
        hn_ref[...] = _rms(x_ref[...], g_ref[...]).astype(BF16)

    o_ref[...] = _dot(hn_ref[...], w_ref[...])


def norm_mm(x, g, w, l, *, tm, tn):
    t, d = x.shape
    n = w.shape[2]
    return pl.pallas_call(
        _norm_mm_kernel,
        grid=(t // tm, n // tn),
        in_specs=[pl.BlockSpec((tm, d), lambda i, j: (i, 0)),
                  pl.BlockSpec((1, d), lambda i, j: (0, 0)),
                  pl.BlockSpec((None, d, tn), lambda i, j: (l, 0, j))],
        out_specs=pl.BlockSpec((tm, tn), lambda i, j: (i, j)),
        out_shape=jax.ShapeDtypeStruct((t, n), F32),
        scratch_shapes=[pltpu.VMEM((tm, d), BF16)],
        compiler_params=_params("parallel", "arbitrary"),
        name="norm_mm",
    )(x, g.reshape(1, d), w)


def _rows_of(i, n_head, head_ref, tail_ref):
    return jnp.where(i < n_head, head_ref[...], tail_ref[...])


def _attn_out_kernel(a_ref, at_ref, w_ref, r_ref, g_ref, o_ref, ht_ref, *, n_head):
    a = _rows_of(pl.program_id(0), n_head, a_ref, at_ref)
    x = r_ref[...] + _dot(a.astype(BF16), w_ref[...])
    o_ref[...] = x
    ht_ref[...] = _rms(x, g_ref[...]).T.astype(BF16)


def attn_out(a, a_tail, w, l, res, g):
    t, d = res.shape
    tm, k = a_tail.shape
    n_head = a.shape[0] // tm
    assert a.shape[0] == n_head * tm and t == (n_head + 1) * tm and tm % LANES == 0
    row = pl.BlockSpec((tm, d), lambda i: (i, 0))
    return pl.pallas_call(
        functools.partial(_attn_out_kernel, n_head=n_head),
        grid=(t // tm,),
        in_specs=[pl.BlockSpec((tm, k), lambda i: (jnp.minimum(i, n_head - 1), 0)),
                  pl.BlockSpec((tm, k), lambda i: (0, 0)),
                  pl.BlockSpec((None, k, d), lambda i: (l, 0, 0)),
                  row, pl.BlockSpec((1, d), lambda i: (0, 0))],
        out_specs=[row, pl.BlockSpec((d, tm), lambda i: (0, i))],
        out_shape=[jax.ShapeDtypeStruct((t, d), F32), jax.ShapeDtypeStruct((d, t), BF16)],
        compiler_params=_params("parallel"),
        name="attn_out",
    )(a, a_tail, w, res, g.reshape(1, d))


def _pool_branch(window_sums, a, cnts, poolw_ref, pscale):
    outs = []
    for gi in range(len(POOL_WINDOWS)):
        sl = slice(gi * GROUP, (gi + 1) * GROUP)
        pg = window_sums[gi] / cnts[gi] - a[:, sl]
        outs.append(_dot(pg.astype(BF16), poolw_ref[gi]))
    return jnp.concatenate(outs, axis=1) * pscale


def _mixer_prompt_kernel(z_ref, poolw_ref, pscale_ref, scw_ref, cmw_ref, cmb_ref, cmg_ref,
                         cmbeta_ref, sgg_ref, sgbeta_ref, sgw_ref, sgbias_ref,
                         p_ref, pool_o, sc_o, cm_o, ext_ref, shift_ref, *, ts):
    tb = pl.program_id(1)
    h0 = HALO

    @pl.when(tb == 0)
    def _():
        ext_ref[:, 0:h0, :] = jnp.zeros((3, h0, D_BR), F32)

    a = z_ref[:, 0:D_BR]
    bg = z_ref[:, D_BR:2 * D_BR]
    gated = z_ref[:, 2 * D_BR:3 * D_BR] * z_ref[:, 3 * D_BR:4 * D_BR]
    glu = z_ref[:, 4 * D_BR:5 * D_BR] * _sigmoid(z_ref[:, 5 * D_BR:6 * D_BR])
    ext_ref[0, h0:h0 + ts, :] = a
    ext_ref[1, h0:h0 + ts, :] = gated
    ext_ref[2, h0:h0 + ts, :] = glu

    pos = tb * ts + lax.broadcasted_iota(jnp.int32, (ts, 1), 0)
    sums, cnts = [], []
    for gi, w in enumerate(POOL_WINDOWS):
        sl = slice(gi * GROUP, (gi + 1) * GROUP)
        s = a[:, sl]
        for j in range(1, w):
            s = s + ext_ref[0, h0 - j:h0 - j + ts, sl]
        sums.append(s)
        cnts.append(jnp.minimum(w, pos + 1).astype(F32))
    p_ref[:, 0:D_BR] = _pool_branch(sums, a, cnts, poolw_ref, pscale_ref[...]).astype(BF16)

    conv = scw_ref[SC_WIDTH - 1:SC_WIDTH, :] * gated
    for k in range(SC_WIDTH - 1):
        off = h0 - (SC_WIDTH - 1) + k
        conv = conv + scw_ref[k:k + 1, :] * ext_ref[1, off:off + ts, :]
    p_ref[:, D_BR:2 * D_BR] = (bg * conv).astype(BF16)

    span = ts + h0 - 8
    for rho in range(8):
        shift_ref[rho] = ext_ref[2, rho:rho + span, :]
    c = cmw_ref[CM_WIDTH - 1:CM_WIDTH, :] * glu
    for k in range(CM_WIDTH - 1):
        q, rho = divmod(h0 - (CM_WIDTH - 1) + k, 8)
        c = c + cmw_ref[k:k + 1, :] * shift_ref[rho, 8 * q:8 * q + ts, :]
    c = _ln(c + cmb_ref[...], cmg_ref[...], cmbeta_ref[...])
    p_ref[:, 2 * D_BR:3 * D_BR] = (c * _sigmoid(c)).astype(BF16)

    vn = _ln(z_ref[:, 7 * D_BR:8 * D_BR], sgg_ref[...], sgbeta_ref[...])
    row = lax.broadcasted_iota(jnp.int32, (CHUNK, CHUNK), 0)
    col = lax.broadcasted_iota(jnp.int32, (CHUNK, CHUNK), 1)
    tril = row >= col
    for gi in range(D_BR // GROUP):
        sl = slice(gi * GROUP, (gi + 1) * GROUP)
        wg = jnp.where(tril, sgw_ref[gi], 0.0).astype(BF16)
        for ci in range(ts // CHUNK):
            rows = slice(ci * CHUNK, (ci + 1) * CHUNK)
            mixed = _dot(wg, vn[rows, sl].astype(BF16)) + sgbias_ref[:, sl]
            p_ref[rows, 3 * D_BR + gi * GROUP:3 * D_BR + (gi + 1) * GROUP] = (
                z_ref[rows, 6 * D_BR + gi * GROUP:6 * D_BR + (gi + 1) * GROUP] * mixed).astype(BF16)

    end = h0 + ts
    pool_o[0] = ext_ref[0, end - POOL_STATE:end, :]
    sc_o[0] = ext_ref[1, end - (SC_WIDTH - 1):end, :]
    cm_o[0] = ext_ref[2, end - (CM_WIDTH - 1):end, :]
    ext_ref[:, 0:h0, :] = ext_ref[:, ts:ts + h0, :]


def mixer_prompt(z, lp, *, n_seq, seq, ts):
    nb = seq // ts
    full = lambda *shape: pl.BlockSpec(shape, lambda n, t: (0,) * len(shape))
    state = lambda rows: pl.BlockSpec((1, rows, D_BR), lambda n, t: (n, 0, 0))
    return pl.pallas_call(
        functools.partial(_mixer_prompt_kernel, ts=ts),
        grid=(n_seq, nb),
        in_specs=[pl.BlockSpec((ts, 8 * D_BR), lambda n, t: (n * nb + t, 0)),
                  full(4, GROUP, GROUP), full(1, D_BR), full(SC_WIDTH, D_BR),
                  full(CM_WIDTH, D_BR), full(1, D_BR), full(1, D_BR), full(1, D_BR),
                  full(1, D_BR), full(1, D_BR), full(4, CHUNK, CHUNK), full(CHUNK, D_BR)],
        out_specs=[pl.BlockSpec((ts, N_BRANCH * D_BR), lambda n, t: (n * nb + t, 0)),
                   state(POOL_STATE), state(SC_WIDTH - 1), state(CM_WIDTH - 1)],
        out_shape=[jax.ShapeDtypeStruct((n_seq * seq, N_BRANCH * D_BR), BF16),
                   jax.ShapeDtypeStruct((n_seq, POOL_STATE, D_BR), F32),
                   jax.ShapeDtypeStruct((n_seq, SC_WIDTH - 1, D_BR), F32),
                   jax.ShapeDtypeStruct((n_seq, CM_WIDTH - 1, D_BR), F32)],
        scratch_shapes=[pltpu.VMEM((3, HALO + ts, D_BR), F32),
                        pltpu.VMEM((8, HALO + ts - 8, D_BR), F32)],
        compiler_params=_params("parallel", "arbitrary"),
        name="mixer_prompt",
    )(z, lp["pool_w"], lp["pool_scale"], lp["sc_w"], lp["cm_w"], lp["cm_b"], lp["cm_ln_g"],
      lp["cm_ln_b"], lp["sg_ln_g"], lp["sg_ln_b"], lp["sg_w"], lp["sg_bias"])


def _mixer_sample_kernel(z_ref, pool_ref, sc_ref, cm_ref, poolw_ref, pscale_ref, scw_ref, cmw_ref,
                         cmb_ref, cmg_ref, cmbeta_ref, sgg_ref, sgbeta_ref, sgwts_ref, sgbias_ref,
                         p_ref, pool_o, sc_o, cm_o, vn_o, *, td, ns, start_pos):
    rows = lambda t: slice(t * ns, (t + 1) * ns)
    col = lambda k: slice(k * D_BR, (k + 1) * D_BR)

    ext_a = [pool_ref[j] for j in range(POOL_STATE)] + [z_ref[rows(t), col(0)] for t in range(td)]
    a_all = jnp.concatenate(ext_a[POOL_STATE:], axis=0)
    sums, cnts = [], []
    for gi, w in enumerate(POOL_WINDOWS):
        sl = slice(gi * GROUP, (gi + 1) * GROUP)
        per_t = []
        for t in range(td):
            s = ext_a[POOL_STATE + t][:, sl]
            for j in range(1, w):
                s = s + ext_a[POOL_STATE + t - j][:, sl]
            per_t.append(s)
        sums.append(jnp.concatenate(per_t, axis=0))
        cnt = jnp.concatenate(
            [jnp.full((ns, 1), float(min(w, start_pos + t + 1)), F32) for t in range(td)], axis=0)
        cnts.append(cnt)
    p_ref[:, col(0)] = _pool_branch(sums, a_all, cnts, poolw_ref, pscale_ref[...]).astype(BF16)
    for j in range(POOL_STATE):
        pool_o[j] = ext_a[td + j]

    ext_b = [sc_ref[j] for j in range(SC_WIDTH - 1)]
    ext_b += [z_ref[rows(t), col(2)] * z_ref[rows(t), col(3)] for t in range(td)]
    for t in range(td):
        conv = scw_ref[0:1, :] * ext_b[t]
        for k in range(1, SC_WIDTH):
            conv = conv + scw_ref[k:k + 1, :] * ext_b[t + k]
        p_ref[rows(t), col(1)] = (z_ref[rows(t), col(1)] * conv).astype(BF16)
    for j in range(SC_WIDTH - 1):
        sc_o[j] = ext_b[td + j]

    ext_c = [cm_ref[j] for j in range(CM_WIDTH - 1)]
    ext_c += [z_ref[rows(t), col(4)] * _sigmoid(z_ref[rows(t), col(5)]) for t in range(td)]
    for t in range(td):
        c = cmw_ref[0:1, :] * ext_c[t]
        for k in range(1, CM_WIDTH):
            c = c + cmw_ref[k:k + 1, :] * ext_c[t + k]
        c = _ln(c + cmb_ref[...], cmg_ref[...], cmbeta_ref[...])
        p_ref[rows(t), col(2)] = (c * _sigmoid(c)).astype(BF16)
    for j in range(CM_WIDTH - 1):
        cm_o[j] = ext_c[td + j]

    vn = [_ln(z_ref[rows(t), col(7)], sgg_ref[...], sgbeta_ref[...]) for t in range(td)]
    for t in range(td):
        vn_o[t] = vn[t]
        mixed = sgbias_ref[t:t + 1, :]
        for s in range(t + 1):
            mixed = mixed + sgwts_ref[t, s:s + 1, :] * vn[s]
        p_ref[rows(t), col(3)] = (z_ref[rows(t), col(6)] * mixed).astype(BF16)


def mixer_sample(z, pool_tm, sc_tm, cm_tm, lp, *, td, ns, t_prompt, start_pos):
    blk = t_prompt // (td * ns)
    full = lambda *shape: pl.BlockSpec(shape, lambda i: (0,) * len(shape))
    tm_shape = lambda rows: jax.ShapeDtypeStruct((rows, ns, D_BR), F32)
    return pl.pallas_call(
        functools.partial(_mixer_sample_kernel, td=td, ns=ns, start_pos=start_pos),
        grid=(1,),
        in_specs=[pl.BlockSpec((td * ns, 8 * D_BR), lambda i: (blk, 0)),
                  full(POOL_STATE, ns, D_BR), full(SC_WIDTH - 1, ns, D_BR),
                  full(CM_WIDTH - 1, ns, D_BR),
                  full(4, GROUP, GROUP), full(1, D_BR), full(SC_WIDTH, D_BR),
                  full(CM_WIDTH, D_BR), full(1, D_BR), full(1, D_BR), full(1, D_BR),
                  full(1, D_BR), full(1, D_BR), full(td, td, D_BR), full(CHUNK, D_BR)],
        out_specs=[full(td * ns, N_BRANCH * D_BR),
                   full(POOL_STATE, ns, D_BR), full(SC_WIDTH - 1, ns, D_BR),
                   full(CM_WIDTH - 1, ns, D_BR), full(td, ns, D_BR)],
        out_shape=[jax.ShapeDtypeStruct((td * ns, N_BRANCH * D_BR), BF16), tm_shape(POOL_STATE),
                   tm_shape(SC_WIDTH - 1), tm_shape(CM_WIDTH - 1), tm_shape(td)],
        compiler_params=_params("arbitrary"),
        name="mixer_sample",
    )(z, pool_tm, sc_tm, cm_tm, lp["pool_w"], lp["pool_scale"], lp["sc_w"], lp["cm_w"],
      lp["cm_b"], lp["cm_ln_g"], lp["cm_ln_b"], lp["sg_ln_g"], lp["sg_ln_b"], lp["sg_wts"],
      lp["sg_bias"])


def _norm_in_kernel(x_ref, g_ref, w_ref, z_ref, hn_ref):
    @pl.when(pl.program_id(1) == 0)
    def _():
        hn_ref[...] = _rms(x_ref[...], g_ref[...]).astype(BF16)

    z_ref[...] = _dot(hn_ref[...], w_ref[...])


def norm_in(x, g, w, l, n_cols, *, tm, tn):
    t, d = x.shape
    return pl.pallas_call(
        _norm_in_kernel,
        grid=(t // tm, n_cols // tn),
        in_specs=[pl.BlockSpec((tm, d), lambda i, j: (i, 0)),
                  pl.BlockSpec((1, d), lambda i, j: (0, 0)),
                  pl.BlockSpec((None, d, tn), lambda i, j: (l, 0, j))],
        out_specs=[pl.BlockSpec((tm, tn), lambda i, j: (i, j)),
                   pl.BlockSpec((tm, d), lambda i, j: (i, 0))],
        out_shape=[jax.ShapeDtypeStruct((t, n_cols), F32), jax.ShapeDtypeStruct((t, d), BF16)],
        compiler_params=_params("parallel", "arbitrary"),
        name="norm_in",
    )(x, g.reshape(1, d), w)


def _gate_out_kernel(hn_ref, p_ref, pt_ref, g0_ref, g1_ref, g2_ref, g3_ref, w0_ref, w1_ref,
                     w2_ref, w3_ref, b_ref, wo_ref, x_ref, o_ref, merged_ref, *, n_head, tn):
    j = pl.program_id(1)
    p = _rows_of(pl.program_id(0), n_head, p_ref, pt_ref)
    hn = hn_ref[...]
    acc = None
    for i, (g_ref, w_ref) in enumerate(((g0_ref, w0_ref), (g1_ref, w1_ref), (g2_ref, w2_ref),
                                        (g3_ref, w3_ref))):
        gate = _sigmoid(_dot(hn, g_ref[...]) + b_ref[i:i + 1, :])
        term = gate * _dot(p[:, i * D_BR:(i + 1) * D_BR].astype(BF16), w_ref[...])
        acc = term if acc is None else acc + term
    merged_ref[:, pl.ds(pl.multiple_of(j * tn, tn), tn)] = acc.astype(BF16)

    @pl.when(j == pl.num_programs(1) - 1)
    def _():
        o_ref[...] = x_ref[...] + _dot(merged_ref[...], wo_ref[...])


def gate_out(hn, p_head, p_tail, w_in, w_outs, b_gate, w_o, l, x, *, tn):
    t, d = x.shape
    tm = p_tail.shape[0]
    n_head = p_head.shape[0] // tm
    assert p_head.shape[0] == n_head * tm and t == (n_head + 1) * tm
    first = N_BRANCH * 2 * D_BR // tn
    gate_spec = lambda i: pl.BlockSpec((None, d, tn), lambda r, c: (l, 0, first + i * (d // tn) + c))
    w_spec = pl.BlockSpec((None, D_BR, tn), lambda r, c: (l, 0, c))
    row = pl.BlockSpec((tm, d), lambda r, c: (r, 0))
    return pl.pallas_call(
        functools.partial(_gate_out_kernel, n_head=n_head, tn=tn),
        grid=(t // tm, d // tn),
        in_specs=[row,
                  pl.BlockSpec((tm, N_BRANCH * D_BR), lambda r, c: (jnp.minimum(r, n_head - 1), 0)),
                  pl.BlockSpec((tm, N_BRANCH * D_BR), lambda r, c: (0, 0)),
                  gate_spec(0), gate_spec(1), gate_spec(2), gate_spec(3),
                  w_spec, w_spec, w_spec, w_spec,
                  pl.BlockSpec((N_BRANCH, tn), lambda r, c: (0, c)),
                  pl.BlockSpec((None, d, d), lambda r, c: (l, 0, 0), pipeline_mode=pl.Buffered(1)),
                  row],
        out_specs=row,
        out_shape=jax.ShapeDtypeStruct((t, d), F32),
        scratch_shapes=[pltpu.VMEM((tm, d), BF16)],
        compiler_params=_params("parallel", "arbitrary"),
        name="gate_out",
    )(hn, p_head, p_tail, w_in, w_in, w_in, w_in, *w_outs, b_gate, w_o, x)


def _softmax_rows(s):
    m = jnp.max(s, axis=-1, keepdims=True)
    p = jnp.exp(s - m)
    return p / jnp.sum(p, axis=-1, keepdims=True)


def _attn_prompt_kernel(q_ref, k_ref, v_ref, o_ref, *, scale):
    for h in range(X_HEADS):
        sl = slice(h * GROUP, (h + 1) * GROUP)
        s = lax.dot_general(q_ref[:, sl].astype(BF16), k_ref[0, :, sl].astype(BF16),
                            (((1,), (1,)), ((), ())), preferred_element_type=F32) * scale
        o_ref[:, sl] = _dot(_softmax_rows(s).astype(BF16), v_ref[0, :, sl].astype(BF16))


def attn_prompt(q, k, v, *, n_seq, seq, tq):
    dx = q.shape[1]
    mem = k.shape[1]
    nb = seq // tq
    return pl.pallas_call(
        functools.partial(_attn_prompt_kernel, scale=GROUP ** -0.5),
        grid=(n_seq, nb),
        in_specs=[pl.BlockSpec((tq, dx), lambda n, t: (n * nb + t, 0)),
                  pl.BlockSpec((1, mem, dx), lambda n, t: (n, 0, 0)),
                  pl.BlockSpec((1, mem, dx), lambda n, t: (n, 0, 0))],
        out_specs=pl.BlockSpec((tq, dx), lambda n, t: (n * nb + t, 0)),
        out_shape=jax.ShapeDtypeStruct((n_seq * seq, dx), F32),
        compiler_params=_params("parallel", "parallel"),
        name="attn_prompt",
    )(q, k, v)


def _attn_sample_kernel(q_ref, k_ref, v_ref, o_ref, *, scale):
    rows, cols = q_ref.shape[1], k_ref.shape[1]
    row_head = lax.broadcasted_iota(jnp.int32, (rows, cols), 0) % X_HEADS
    col_head = lax.broadcasted_iota(jnp.int32, (rows, cols), 1) % X_HEADS
    s = jnp.einsum("bqd,bkd->bqk", q_ref[...].astype(BF16), k_ref[...].astype(BF16),
                   preferred_element_type=F32) * scale
    s = jnp.where((row_head == col_head)[None], s, NEG_INF)
    o_ref[...] = jnp.einsum("bqk,bkd->bqd", _softmax_rows(s).astype(BF16),
                            v_ref[...].astype(BF16), preferred_element_type=F32)


def attn_sample(q, k, v, l, *, sb):
    n_seq, rows, hd = q.shape
    kv_spec = pl.BlockSpec((None, sb, k.shape[2], hd), lambda i: (l, i, 0, 0))
    return pl.pallas_call(
        functools.partial(_attn_sample_kernel, scale=GROUP ** -0.5),
        grid=(n_seq // sb,),
        in_specs=[pl.BlockSpec((sb, rows, hd), lambda i: (i, 0, 0)), kv_spec, kv_spec],
        out_specs=pl.BlockSpec((sb, rows, hd), lambda i: (i, 0, 0)),
        out_shape=jax.ShapeDtypeStruct((n_seq, rows, hd), F32),
        compiler_params=_params("parallel"),
        name="attn_sample",
    )(q, k, v)


def _merge_exchange_network(n):
    pairs, p = [], 1
    while p < n:
        k = p
        while k >= 1:
            for j in range(k % p, n - k, 2 * k):
                for i in range(min(k, n - j - k)):
                    if (i + j) // (2 * p) == (i + j + k) // (2 * p):
                        pairs.append((i + j, i + j + k))
            k //= 2
        p *= 2
    return pairs


def _sort_descending(values):
    n = pl.next_power_of_2(len(values))
    vals = list(values) + [None] * (n - len(values))
    for i, j in _merge_exchange_network(n):
        hi, lo = vals[i], vals[j]
        if lo is None:
            continue
        if hi is None:
            vals[i], vals[j] = lo, None
        else:
            vals[i], vals[j] = jnp.maximum(hi, lo), jnp.minimum(hi, lo)
    return vals[:len(values)]


_CAND = [(i, j) for i in range(PEER_TOPK + 1) for j in range(PEER_TOPK + 1)
         if (i + 1) * (j + 1) <= PEER_TOPK + 1]


def _peer_pre_kernel(ht_ref, wq_ref, keys_ref, s2_o, sel_o, q_scr, s_scr, v_scr, *, tb):
    nk = GROUP
    q_scr[...] = _dot(wq_ref[...], ht_ref[...]).astype(BF16)
    for hp in range(2 * PEER_HEADS):
        s_scr[hp] = _dot(keys_ref[hp], q_scr[hp * nk:(hp + 1) * nk, :])

    idx = lax.broadcasted_iota(jnp.int32, (nk, LANES), 0)

    def chunk(ci, carry):
        lanes = pl.ds(pl.multiple_of(ci * LANES, LANES), LANES)
        tied = jnp.zeros((1, LANES), F32)
        for hp in range(2 * PEER_HEADS):
            h, p = divmod(hp, 2)
            cur = s_scr[hp, :, lanes]
            stacks = _sort_descending([cur[8 * g:8 * (g + 1), :] for g in range(nk // 8)])
            for r in range(PEER_TOPK + 1):
                m = jnp.max(stacks[0], axis=0, keepdims=True)
                v_scr[p, r, h:h + 1, lanes] = m
                depth = PEER_TOPK - r
                if depth > 0:
                    hit = stacks[0] == m
                    below = stacks[1:] + [jnp.full_like(stacks[0], NEG_INF)]
                    stacks = [jnp.where(hit, below[q], stacks[q])
                              for q in range(min(depth, len(stacks)))]
            n_ge = jnp.sum(jnp.where(cur >= m, 1.0, 0.0), axis=0, keepdims=True)
            tied = jnp.maximum(tied, jnp.where(n_ge != PEER_TOPK + 1.0, 1.0, 0.0))

        @pl.when(jnp.max(tied) > 0.0)
        def _():
            for hp in range(2 * PEER_HEADS):
                h, p = divmod(hp, 2)
                cur = s_scr[hp, :, lanes]
                for r in range(PEER_TOPK + 1):
                    m = jnp.max(cur, axis=0, keepdims=True)
                    v_scr[p, r, h:h + 1, lanes] = m
                    if r < PEER_TOPK:
                        first = jnp.min(jnp.where(cur == m, idx, nk), axis=0, keepdims=True)
                        cur = jnp.where(idx == first, NEG_INF, cur)

        v1 = [v_scr[0, r, :, lanes] for r in range(PEER_TOPK + 1)]
        v2 = [v_scr[1, r, :, lanes] for r in range(PEER_TOPK + 1)]
        cands = [v1[i] + v2[j] for i, j in _CAND]
        ranked = _sort_descending(cands)
        kth, nxt = ranked[PEER_TOPK - 1], ranked[PEER_TOPK]
        tau = 0.5 * (kth + nxt)
        top = cands[0]
        zsum = jnp.zeros_like(top)
        for ck in cands:
            zsum = zsum + jnp.where(ck >= tau, jnp.exp(ck - top), 0.0)
        shift = top + jnp.log(zsum)
        for h in range(PEER_HEADS):
            s1 = s_scr[2 * h, :, lanes]
            sel_o[0, h, :, lanes] = (tau[h:h + 1, :] - s1) * LOG2E
            sel_o[1, h, :, lanes] = (s1 - shift[h:h + 1, :]) * LOG2E - 0.5
            s2_o[h * nk:(h + 1) * nk, lanes] = s_scr[2 * h + 1, :, lanes] * LOG2E
        return carry

    lax.fori_loop(0, tb // LANES, chunk, 0)


def peer_pre(ht, wq_t, keys, l, *, tb):
    d, t = ht.shape
    nq = wq_t.shape[1]
    hk = PEER_HEADS * GROUP
    flat = pl.BlockSpec((hk, tb), lambda i: (0, i))
    cube = pl.BlockSpec((2, PEER_HEADS, GROUP, tb), lambda i: (0, 0, 0, i))
    return pl.pallas_call(
        functools.partial(_peer_pre_kernel, tb=tb),
        grid=(t // tb,),
        in_specs=[pl.BlockSpec((d, tb), lambda i: (0, i)),
                  pl.BlockSpec((None, nq, d), lambda i: (l, 0, 0)),
                  pl.BlockSpec((None, 2 * PEER_HEADS, GROUP, GROUP), lambda i: (l, 0, 0, 0))],
        out_specs=[flat, cube],
        out_shape=[jax.ShapeDtypeStruct((hk, t), F32),
                   jax.ShapeDtypeStruct((2, PEER_HEADS, GROUP, t), F32)],
        scratch_shapes=[pltpu.VMEM((nq, tb), BF16),
                        pltpu.VMEM((2 * PEER_HEADS, GROUP, tb), F32),
                        pltpu.VMEM((2, PEER_TOPK + 1, PEER_HEADS, tb), F32)],
        compiler_params=_params("parallel"),
        name="peer_pre",
    )(ht, wq_t, keys)


def _peer_dense_kernel(ht_ref, htn_ref, ua_ref, ub_ref, vt_ref, s2_ref, sel_ref, x_ref,
                       o_ref, acc_ref, at_ref, gw_ref, *, tb, te, ne):
    s = pl.program_id(0)
    e_cur = jnp.maximum(s - 1, 0) % ne
    half = te // 2
    tbh = tb // 2
    n_a = half // GROUP
    piece = lambda k, c: 2 * k + c
    cols = lambda c: slice(c * tbh, (c + 1) * tbh)

    @pl.when(s == 0)
    def _():
        at_ref[0:2] = jnp.zeros((2, half, tbh), F32)

    @pl.when(e_cur == 0)
    def _():
        acc_ref[...] = jnp.zeros_like(acc_ref)

    def pre_activations(k, c, u_ref, h_ref):
        at_ref[piece(k, c)] = _dot(u_ref[...], h_ref[:, cols(c)])

    def weighted_activations(k, c):
        p = piece(k, c)
        for cl in range(tbh // LANES):
            lanes = slice(cl * LANES, (cl + 1) * LANES)
            glanes = slice(c * tbh + cl * LANES, c * tbh + (cl + 1) * LANES)
            for r in range(GROUP // ROWS):
                w = [jnp.zeros((ROWS, LANES), F32) for _ in range(n_a)]
                for h in range(PEER_HEADS):
                    s2 = s2_ref[h * GROUP + r * ROWS:h * GROUP + (r + 1) * ROWS, glanes]
                    for a in range(n_a):
                        ag = k * n_a + a
                        keep = s2 >= sel_ref[0, h, ag:ag + 1, glanes]
                        w[a] = w[a] + jnp.where(keep, jnp.exp2(s2 + sel_ref[1, h, ag:ag + 1, glanes]), 0.0)
                for a in range(n_a):
                    rows = slice(a * GROUP + r * ROWS, a * GROUP + (r + 1) * ROWS)
                    z = at_ref[p, rows, lanes]
                    gw_ref[p, rows, lanes] = (w[a] * (z * (1.0 + lax.erf(z)))).astype(BF16)

    def accumulate(k, c):
        acc_ref[:, cols(c)] += _dot(vt_ref[:, k * half:(k + 1) * half], gw_ref[piece(k, c)])

    pre_activations(1, 0, ub_ref, ht_ref)
    pre_activations(1, 1, ub_ref, ht_ref)
    for c in range(2):
        weighted_activations(0, c)
        accumulate(0, c)
        pre_activations(0, c, ua_ref, htn_ref)
    for c in range(2):
        weighted_activations(1, c)
        accumulate(1, c)

    @pl.when(jnp.logical_and(e_cur == ne - 1, s > 0))
    def _():
        o_ref[...] = x_ref[...] + acc_ref[...].T


def peer_dense(ht, u, vt, l, s2, sel, x, *, tb, te):
    d, t = ht.shape
    ne = u.shape[1] // te
    assert vt.shape[1:] == (ne, d, te)
    n_tiles = (t // tb) * ne
    half = te // 2
    a_blk = te // GROUP
    once = pl.Buffered(1)
    cur = lambda s: jnp.clip(s - 1, 0, n_tiles - 1)
    nxt = lambda s: jnp.minimum(s, n_tiles - 1)
    return pl.pallas_call(
        functools.partial(_peer_dense_kernel, tb=tb, te=te, ne=ne),
        grid=(n_tiles + 1,),
        in_specs=[pl.BlockSpec((d, tb), lambda s: (0, cur(s) // ne), pipeline_mode=once),
                  pl.BlockSpec((d, tb), lambda s: (0, nxt(s) // ne), pipeline_mode=once),
                  pl.BlockSpec((None, half, d), lambda s: (l, 2 * (nxt(s) % ne), 0)),
                  pl.BlockSpec((None, half, d), lambda s: (l, 2 * (cur(s) % ne) + 1, 0)),
                  pl.BlockSpec((None, None, d, te), lambda s: (l, cur(s) % ne, 0, 0)),
                  pl.BlockSpec((PEER_HEADS * GROUP, tb), lambda s: (0, cur(s) // ne),
                               pipeline_mode=once),
                  pl.BlockSpec((2, PEER_HEADS, a_blk, tb),
                               lambda s: (0, 0, cur(s) % ne, cur(s) // ne)),
                  pl.BlockSpec((tb, d), lambda s: (cur(s) // ne, 0), pipeline_mode=once)],
        out_specs=pl.BlockSpec((tb, d), lambda s: (cur(s) // ne, 0)),
        out_shape=jax.ShapeDtypeStruct((t, d), F32),
        scratch_shapes=[pltpu.VMEM((d, tb), F32), pltpu.VMEM((4, half, tb // 2), F32),
                        pltpu.VMEM((4, half, tb // 2), BF16)],
        compiler_params=_params("arbitrary"),
        name="peer_dense",
    )(ht, ht, u, u, vt, s2, sel, x)


def _final_norm_kernel(x_ref, g_ref, head_o, tail_o, *, n_head):
    i = pl.program_id(0)
    y = _rms(x_ref[...], g_ref[...])

    @pl.when(i < n_head)
    def _():
        head_o[...] = y

    @pl.when(i == n_head)
    def _():
        tail_o[...] = y


def final_norm(x, g, *, tm):
    t, d = x.shape
    n_head = t // tm - 1
    assert t == (n_head + 1) * tm
    return pl.pallas_call(
        functools.partial(_final_norm_kernel, n_head=n_head),
        grid=(n_head + 1,),
        in_specs=[pl.BlockSpec((tm, d), lambda i: (i, 0)), pl.BlockSpec((1, d), lambda i: (0, 0))],
        out_specs=[pl.BlockSpec((tm, d), lambda i: (jnp.minimum(i, n_head - 1), 0)),
                   pl.BlockSpec((tm, d), lambda i: (0, 0))],
        out_shape=[jax.ShapeDtypeStruct((n_head * tm, d), F32),
                   jax.ShapeDtypeStruct((tm, d), F32)],
        compiler_params=_params("arbitrary"),
        name="final_norm",
    )(x, g.reshape(1, d))


def _to_time_major(s):
    return jnp.swapaxes(s, 0, 1)


def kernel(x_prompt, x_sample, mem_prompt, cache_mem_k, cache_mem_v, state_pool, state_sconv, state_cconv, g_mix, w_in, b_gate, pool_w, pool_scale, w_pool_out, sc_w, w_sc_out, cm_w, cm_b, cm_ln_g, cm_ln_b, w_cm_out, sg_ln_g, sg_ln_b, sg_w, sg_b, w_sg_out, w_o, g_x, g_mem, w_xq, w_xk, w_xv, w_xo, g_peer, w_pq, peer_keys, peer_u, peer_v, g_final):
    nb, seq, d = x_prompt.shape
    ns, td, _ = x_sample.shape
    depth = w_in.shape[0]
    mem_len = mem_prompt.shape[1]
    past_len = 16384
    t_prompt = nb * seq
    t_all = t_prompt + ns * td
    dx = X_HEADS * GROUP

    tm_big = _tile(t_all, 1088, 16)
    t_s = ns * td
    assert t_prompt % t_s == 0
    tn_in = _tile(w_in.shape[2], 1024, 128)
    tn_d = _tile(d, 512, 128)
    tn_gate = _tile(d, 256, 128)
    ts = _tile(seq, 256, CHUNK)
    tq = _tile(seq, 512, 8)
    tb = _tile(t_all, 512, LANES)
    tb_pre = _tile(t_all, 256, LANES)
    te = _tile(peer_u.shape[1], 1024, 8 * GROUP)

    x = jnp.concatenate([x_prompt.reshape(t_prompt, d),
                         _to_time_major(x_sample).reshape(ns * td, d)], axis=0)
    mem2d = mem_prompt.reshape(nb * mem_len, d)
    col1 = lambda v: v.reshape(1, -1)
    cache_k = cache_mem_k.reshape(depth, ns, mem_len * X_HEADS, GROUP)
    cache_v = cache_mem_v.reshape(depth, ns, mem_len * X_HEADS, GROUP)

    bf = lambda w: w.astype(BF16)
    w_in_b, w_o_b, w_xq_b, w_xk_b, w_xv_b, w_xo_b = map(bf, (w_in, w_o, w_xq, w_xk, w_xv, w_xo))
    w_branch_b = tuple(map(bf, (w_pool_out, w_sc_out, w_cm_out, w_sg_out)))
    w_pq_t = bf(jnp.swapaxes(w_pq, 1, 2))
    keys_b = bf(peer_keys.reshape(depth, 2 * PEER_HEADS, GROUP, GROUP))
    peer_u_b = bf(peer_u * math.sqrt(0.5))
    peer_vt = bf(jnp.swapaxes(peer_v.reshape(depth, -1, te, d), 2, 3))

    outs = {k: [] for k in ("mk", "mv", "pool_p", "sc_p", "cm_p", "pool_s", "sc_s", "cm_s", "cv_s")}
    for l in range(depth):
        lp = dict(
            pool_w=pool_w[l].astype(BF16), pool_scale=col1(pool_scale[l]), sc_w=sc_w[l],
            cm_w=cm_w[l], cm_b=col1(cm_b[l]), cm_ln_g=col1(cm_ln_g[l]), cm_ln_b=col1(cm_ln_b[l]),
            sg_ln_g=col1(sg_ln_g[l]), sg_ln_b=col1(sg_ln_b[l]), sg_w=sg_w[l],
            sg_bias=jnp.repeat(sg_b[l].T, GROUP, axis=1),
            sg_wts=jnp.repeat(jnp.transpose(sg_w[l][:, :td, :td], (1, 2, 0)), GROUP, axis=2))

        z, hn = norm_in(x, g_mix[l], w_in_b, l, 2 * N_BRANCH * D_BR, tm=tm_big, tn=tn_in)
        p_p, pool_p, sc_p, cm_p = mixer_prompt(z, lp, n_seq=nb, seq=seq, ts=ts)
        p_s, pool_s, sc_s, cm_s, vn_s = mixer_sample(
            z, _to_time_major(state_pool[l]), _to_time_major(state_sconv[l]),
            _to_time_major(state_cconv[l]), lp, td=td, ns=ns, t_prompt=t_prompt, start_pos=past_len)
        x = gate_out(hn, p_p, p_s, w_in_b, w_branch_b, b_gate[l], w_o_b, l, x, tn=tn_gate)

        q = norm_mm(x, g_x[l], w_xq_b, l, tm=tm_big, tn=dx)
        tm_mem = _tile(nb * mem_len, 1024, 16)
        k_p = norm_mm(mem2d, g_mem[l], w_xk_b, l, tm=tm_mem, tn=dx)
        v_p = norm_mm(mem2d, g_mem[l], w_xv_b, l, tm=tm_mem, tn=dx)
        o_p = attn_prompt(q, k_p.reshape(nb, mem_len, dx), v_p.reshape(nb, mem_len, dx),
                          n_seq=nb, seq=seq, tq=tq)
        q_s = jnp.swapaxes(q[t_prompt:].reshape(td, ns, dx), 0, 1).reshape(ns, td * X_HEADS, GROUP)
        o_s = attn_sample(q_s, cache_k, cache_v, l, sb=_tile(ns, 8, 1))
        o_s = jnp.swapaxes(o_s.reshape(ns, td, dx), 0, 1).reshape(t_s, dx)
        x, ht = attn_out(o_p, o_s, w_xo_b, l, x, g_peer[l])

        s2, sel = peer_pre(ht, w_pq_t, keys_b, l, tb=tb_pre)
        x = peer_dense(ht, peer_u_b, peer_vt, l, s2, sel, x, tb=tb, te=te)

        outs["mk"].append(k_p.reshape(nb, mem_len, X_HEADS, GROUP))
        outs["mv"].append(v_p.reshape(nb, mem_len, X_HEADS, GROUP))
        outs["pool_p"].append(pool_p)
        outs["sc_p"].append(sc_p)
        outs["cm_p"].append(cm_p)
        outs["pool_s"].append(_to_time_major(pool_s))
        outs["sc_s"].append(_to_time_major(sc_s))
        outs["cm_s"].append(_to_time_major(cm_s))
        outs["cv_s"].append(_to_time_major(vn_s))

    y_p, y_s = final_norm(x, g_final, tm=t_s)
    y_prompt = y_p.reshape(nb, seq, d)
    y_sample = _to_time_major(y_s.reshape(td, ns, d))
    st = lambda k: jnp.stack(outs[k])
    return (y_prompt, y_sample, st("mk"), st("mv"), st("pool_p"), st("sc_p"), st("cm_p"),
            st("pool_s"), st("sc_s"), st("cm_s"), st("cv_s"))
```

```python
import functools
import math

import jax
import jax.numpy as jnp
from jax import lax
from jax.experimental import pallas as pl
from jax.experimental.pallas import tpu as pltpu

EPS = 1e-6
GROUP = 128
POOL_WINDOWS = (2, 4, 8, 16)
POOL_STATE = max(POOL_WINDOWS) - 1
SC_WIDTH = 3
CM_WIDTH = 31
D_BR = 4 * GROUP
N_BRANCH = 4
CHUNK = 128
X_HEADS = 4
PEER_HEADS = 8
PEER_TOPK = 16
HALO = 32
LANES = 128
ROWS = 32
VMEM_LIMIT = 56 * 2 ** 20

BF16 = jnp.bfloat16
F32 = jnp.float32
NEG_INF = float("-inf")
LOG2E = math.log2(math.e)


def _tile(n, pref, mult=8):
    best = None
    for t in range(mult, min(n, pref) + 1, mult):
        if n % t == 0:
            best = t
    assert best is not None, (n, pref, mult)
    return best


def _params(*sem):
    return pltpu.CompilerParams(dimension_semantics=sem, vmem_limit_bytes=VMEM_LIMIT)


def _rms(x, g):
    ms = jnp.mean(x * x, axis=-1, keepdims=True)
    return x * lax.rsqrt(ms + EPS) * g


def _ln(x, g, b):
    mu = jnp.mean(x, axis=-1, keepdims=True)
    xc = x - mu
    var = jnp.mean(xc * xc, axis=-1, keepdims=True)
    return xc * lax.rsqrt(var + EPS) * g + b


def _sigmoid(x):
    return 1.0 / (1.0 + jnp.exp(-x))


def _dot(a, b):
    return jnp.dot(a, b, preferred_element_type=F32)


def _norm_mm_kernel(x_ref, g_ref, w_ref, o_ref, hn_ref):
    @pl.when(pl.program_id(1) == 0)
    def _():
        hn_ref[...] = _rms(x_ref[...], g_ref[...]).astype(BF16)

    o_ref[...] = _dot(hn_ref[...], w_ref[...])


def norm_mm(x, g, w, l, *, tm, tn):
    t, d = x.shape
    n = w.shape[2]
    return pl.pallas_call(
        _norm_mm_kernel,
        grid=(t // tm, n // tn),
        in_specs=[pl.BlockSpec((tm, d), lambda i, j: (i, 0)),
                  pl.BlockSpec((1, d), lambda i, j: (0, 0)),
                  pl.BlockSpec((None, d, tn), lambda i, j: (l, 0, j))],
        out_specs=pl.BlockSpec((tm, tn), lambda i, j: (i, j)),
        out_shape=jax.ShapeDtypeStruct((t, n), F32),
        scratch_shapes=[pltpu.VMEM((tm, d), BF16)],
        compiler_params=_params("parallel", "arbitrary"),
        name="norm_mm",
    )(x, g.reshape(1, d), w)


def _rows_of(i, n_head, head_ref, tail_ref):
    return jnp.where(i < n_head, head_ref[...], tail_ref[...])


def _attn_out_kernel(a_ref, at_ref, w_ref, r_ref, g_ref, o_ref, ht_ref, *, n_head):
    a = _rows_of(pl.program_id(0), n_head, a_ref, at_ref)
    x = r_ref[...] + _dot(a.astype(BF16), w_ref[...])
    o_ref[...] = x
    ht_ref[...] = _rms(x, g_ref[...]).T.astype(BF16)


def attn_out(a, a_tail, w, l, res, g):
    t, d = res.shape
    tm, k = a_tail.shape
    n_head = a.shape[0] // tm
    assert a.shape[0] == n_head * tm and t == (n_head + 1) * tm and tm % LANES == 0
    row = pl.BlockSpec((tm, d), lambda i: (i, 0))
    return pl.pallas_call(
        functools.partial(_attn_out_kernel, n_head=n_head),
        grid=(t // tm,),
        in_specs=[pl.BlockSpec((tm, k), lambda i: (jnp.minimum(i, n_head - 1), 0)),
                  pl.BlockSpec((tm, k), lambda i: (0, 0)),
                  pl.BlockSpec((None, k, d), lambda i: (l, 0, 0)),
                  row, pl.BlockSpec((1, d), lambda i: (0, 0))],
        out_specs=[row, pl.BlockSpec((d, tm), lambda i: (0, i))],
        out_shape=[jax.ShapeDtypeStruct((t, d), F32), jax.ShapeDtypeStruct((d, t), BF16)],
        compiler_params=_params("parallel"),
        name="attn_out",
    )(a, a_tail, w, res, g.reshape(1, d))


def _pool_branch(window_sums, a, cnts, poolw_ref, pscale):
    outs = []
    for gi in range(len(POOL_WINDOWS)):
        sl = slice(gi * GROUP, (gi + 1) * GROUP)
        pg = window_sums[gi] / cnts[gi] - a[:, sl]
        outs.append(_dot(pg.astype(BF16), poolw_ref[gi]))
    return jnp.concatenate(outs, axis=1) * pscale


def _mixer_prompt_kernel(z_ref, poolw_ref, pscale_ref, scw_ref, cmw_ref, cmb_ref, cmg_ref,
                         cmbeta_ref, sgg_ref, sgbeta_ref, sgw_ref, sgbias_ref,
                         p_ref, pool_o, sc_o, cm_o, ext_ref, shift_ref, *, ts):
    tb = pl.program_id(1)
    h0 = HALO

    @pl.when(tb == 0)
    def _():
        ext_ref[:, 0:h0, :] = jnp.zeros((3, h0, D_BR), F32)

    a = z_ref[:, 0:D_BR]
    bg = z_ref[:, D_BR:2 * D_BR]
    gated = z_ref[:, 2 * D_BR:3 * D_BR] * z_ref[:, 3 * D_BR:4 * D_BR]
    glu = z_ref[:, 4 * D_BR:5 * D_BR] * _sigmoid(z_ref[:, 5 * D_BR:6 * D_BR])
    ext_ref[0, h0:h0 + ts, :] = a
    ext_ref[1, h0:h0 + ts, :] = gated
    ext_ref[2, h0:h0 + ts, :] = glu

    pos = tb * ts + lax.broadcasted_iota(jnp.int32, (ts, 1), 0)
    sums, cnts = [], []
    for gi, w in enumerate(POOL_WINDOWS):
        sl = slice(gi * GROUP, (gi + 1) * GROUP)
        s = a[:, sl]
        for j in range(1, w):
            s = s + ext_ref[0, h0 - j:h0 - j + ts, sl]
        sums.append(s)
        cnts.append(jnp.minimum(w, pos + 1).astype(F32))
    p_ref[:, 0:D_BR] = _pool_branch(sums, a, cnts, poolw_ref, pscale_ref[...]).astype(BF16)

    conv = scw_ref[SC_WIDTH - 1:SC_WIDTH, :] * gated
    for k in range(SC_WIDTH - 1):
        off = h0 - (SC_WIDTH - 1) + k
        conv = conv + scw_ref[k:k + 1, :] * ext_ref[1, off:off + ts, :]
    p_ref[:, D_BR:2 * D_BR] = (bg * conv).astype(BF16)

    span = ts + h0 - 8
    for rho in range(8):
        shift_ref[rho] = ext_ref[2, rho:rho + span, :]
    c = cmw_ref[CM_WIDTH - 1:CM_WIDTH, :] * glu
    for k in range(CM_WIDTH - 1):
        q, rho = divmod(h0 - (CM_WIDTH - 1) + k, 8)
        c = c + cmw_ref[k:k + 1, :] * shift_ref[rho, 8 * q:8 * q + ts, :]
    c = _ln(c + cmb_ref[...], cmg_ref[...], cmbeta_ref[...])
    p_ref[:, 2 * D_BR:3 * D_BR] = (c * _sigmoid(c)).astype(BF16)

    vn = _ln(z_ref[:, 7 * D_BR:8 * D_BR], sgg_ref[...], sgbeta_ref[...])
    row = lax.broadcasted_iota(jnp.int32, (CHUNK, CHUNK), 0)
    col = lax.broadcasted_iota(jnp.int32, (CHUNK, CHUNK), 1)
    tril = row >= col
    for gi in range(D_BR // GROUP):
        sl = slice(gi * GROUP, (gi + 1) * GROUP)
        wg = jnp.where(tril, sgw_ref[gi], 0.0).astype(BF16)
        for ci in range(ts // CHUNK):
            rows = slice(ci * CHUNK, (ci + 1) * CHUNK)
            mixed = _dot(wg, vn[rows, sl].astype(BF16)) + sgbias_ref[:, sl]
            p_ref[rows, 3 * D_BR + gi * GROUP:3 * D_BR + (gi + 1) * GROUP] = (
                z_ref[rows, 6 * D_BR + gi * GROUP:6 * D_BR + (gi + 1) * GROUP] * mixed).astype(BF16)

    end = h0 + ts
    pool_o[0] = ext_ref[0, end - POOL_STATE:end, :]
    sc_o[0] = ext_ref[1, end - (SC_WIDTH - 1):end, :]
    cm_o[0] = ext_ref[2, end - (CM_WIDTH - 1):end, :]
    ext_ref[:, 0:h0, :] = ext_ref[:, ts:ts + h0, :]


def mixer_prompt(z, lp, *, n_seq, seq, ts):
    nb = seq // ts
    full = lambda *shape: pl.BlockSpec(shape, lambda n, t: (0,) * len(shape))
    state = lambda rows: pl.BlockSpec((1, rows, D_BR), lambda n, t: (n, 0, 0))
    return pl.pallas_call(
        functools.partial(_mixer_prompt_kernel, ts=ts),
        grid=(n_seq, nb),
        in_specs=[pl.BlockSpec((ts, 8 * D_BR), lambda n, t: (n * nb + t, 0)),
                  full(4, GROUP, GROUP), full(1, D_BR), full(SC_WIDTH, D_BR),
                  full(CM_WIDTH, D_BR), full(1, D_BR), full(1, D_BR), full(1, D_BR),
                  full(1, D_BR), full(1, D_BR), full(4, CHUNK, CHUNK), full(CHUNK, D_BR)],
        out_specs=[pl.BlockSpec((ts, N_BRANCH * D_BR), lambda n, t: (n * nb + t, 0)),
                   state(POOL_STATE), state(SC_WIDTH - 1), state(CM_WIDTH - 1)],
        out_shape=[jax.ShapeDtypeStruct((n_seq * seq, N_BRANCH * D_BR), BF16),
                   jax.ShapeDtypeStruct((n_seq, POOL_STATE, D_BR), F32),
                   jax.ShapeDtypeStruct((n_seq, SC_WIDTH - 1, D_BR), F32),
                   jax.ShapeDtypeStruct((n_seq, CM_WIDTH - 1, D_BR), F32)],
        scratch_shapes=[pltpu.VMEM((3, HALO + ts, D_BR), F32),
                        pltpu.VMEM((8, HALO + ts - 8, D_BR), F32)],
        compiler_params=_params("parallel", "arbitrary"),
        name="mixer_prompt",
    )(z, lp["pool_w"], lp["pool_scale"], lp["sc_w"], lp["cm_w"], lp["cm_b"], lp["cm_ln_g"],
      lp["cm_ln_b"], lp["sg_ln_g"], lp["sg_ln_b"], lp["sg_w"], lp["sg_bias"])


def _mixer_sample_kernel(z_ref, pool_ref, sc_ref, cm_ref, poolw_ref, pscale_ref, scw_ref, cmw_ref,
                         cmb_ref, cmg_ref, cmbeta_ref, sgg_ref, sgbeta_ref, sgwts_ref, sgbias_ref,
                         p_ref, pool_o, sc_o, cm_o, vn_o, *, td, ns, start_pos):
    col = lambda k: slice(k * D_BR, (k + 1) * D_BR)
    zt = lambda t, k: z_ref[t, :, col(k)]

    ext_a = [pool_ref[:, j, :] for j in range(POOL_STATE)] + [zt(t, 0) for t in range(td)]
    a_all = jnp.concatenate(ext_a[POOL_STATE:], axis=0)
    sums, cnts = [], []
    for gi, w in enumerate(POOL_WINDOWS):
        sl = slice(gi * GROUP, (gi + 1) * GROUP)
        per_t = []
        for t in range(td):
            s = ext_a[POOL_STATE + t][:, sl]
            for j in range(1, w):
                s = s + ext_a[POOL_STATE + t - j][:, sl]
            per_t.append(s)
        sums.append(jnp.concatenate(per_t, axis=0))
        cnt = jnp.concatenate(
            [jnp.full((ns, 1), float(min(w, start_pos + t + 1)), F32) for t in range(td)], axis=0)
        cnts.append(cnt)
    pa = _pool_branch(sums, a_all, cnts, poolw_ref, pscale_ref[...]).astype(BF16)
    for t in range(td):
        p_ref[t, :, col(0)] = pa[t * ns:(t + 1) * ns]
    for j in range(POOL_STATE):
        pool_o[:, j, :] = ext_a[td + j]

    ext_b = [sc_ref[:, j, :] for j in range(SC_WIDTH - 1)]
    ext_b += [zt(t, 2) * zt(t, 3) for t in range(td)]
    for t in range(td):
        conv = scw_ref[0:1, :] * ext_b[t]
        for k in range(1, SC_WIDTH):
            conv = conv + scw_ref[k:k + 1, :] * ext_b[t + k]
        p_ref[t, :, col(1)] = (zt(t, 1) * conv).astype(BF16)
    for j in range(SC_WIDTH - 1):
        sc_o[:, j, :] = ext_b[td + j]

    ext_c = [cm_ref[:, j, :] for j in range(CM_WIDTH - 1)]
    ext_c += [zt(t, 4) * _sigmoid(zt(t, 5)) for t in range(td)]
    for t in range(td):
        c = cmw_ref[0:1, :] * ext_c[t]
        for k in range(1, CM_WIDTH):
            c = c + cmw_ref[k:k + 1, :] * ext_c[t + k]
        c = _ln(c + cmb_ref[...], cmg_ref[...], cmbeta_ref[...])
        p_ref[t, :, col(2)] = (c * _sigmoid(c)).astype(BF16)
    for j in range(CM_WIDTH - 1):
        cm_o[:, j, :] = ext_c[td + j]

    vn = [_ln(zt(t, 7), sgg_ref[...], sgbeta_ref[...]) for t in range(td)]
    for t in range(td):
        vn_o[:, t, :] = vn[t]
        mixed = sgbias_ref[t:t + 1, :]
        for s in range(t + 1):
            mixed = mixed + sgwts_ref[t, s:s + 1, :] * vn[s]
        p_ref[t, :, col(3)] = (zt(t, 6) * mixed).astype(BF16)


def mixer_sample(z, pool, sconv, cconv, l, lp, *, td, ns, t_prompt, start_pos, nsb):
    blk = t_prompt // (td * ns)
    z3 = z.reshape(z.shape[0] // ns, ns, z.shape[1])
    full = lambda *shape: pl.BlockSpec(shape, lambda i: (0,) * len(shape))
    st_in = lambda rows: pl.BlockSpec((None, nsb, rows, D_BR), lambda i: (l, i, 0, 0))
    st_out = lambda rows: pl.BlockSpec((nsb, rows, D_BR), lambda i: (i, 0, 0))
    st_shape = lambda rows: jax.ShapeDtypeStruct((ns, rows, D_BR), F32)
    return pl.pallas_call(
        functools.partial(_mixer_sample_kernel, td=td, ns=nsb, start_pos=start_pos),
        grid=(ns // nsb,),
        in_specs=[pl.BlockSpec((td, nsb, 8 * D_BR), lambda i: (blk, i, 0)),
                  st_in(POOL_STATE), st_in(SC_WIDTH - 1), st_in(CM_WIDTH - 1),
                  full(4, GROUP, GROUP), full(1, D_BR), full(SC_WIDTH, D_BR),
                  full(CM_WIDTH, D_BR), full(1, D_BR), full(1, D_BR), full(1, D_BR),
                  full(1, D_BR), full(1, D_BR), full(td, td, D_BR), full(CHUNK, D_BR)],
        out_specs=[pl.BlockSpec((td, nsb, N_BRANCH * D_BR), lambda i: (0, i, 0)),
                   st_out(POOL_STATE), st_out(SC_WIDTH - 1), st_out(CM_WIDTH - 1), st_out(td)],
        out_shape=[jax.ShapeDtypeStruct((td, ns, N_BRANCH * D_BR), BF16), st_shape(POOL_STATE),
                   st_shape(SC_WIDTH - 1), st_shape(CM_WIDTH - 1), st_shape(td)],
        compiler_params=_params("parallel"),
        name="mixer_sample",
    )(z3, pool, sconv, cconv, lp["pool_w"], lp["pool_scale"], lp["sc_w"], lp["cm_w"],
      lp["cm_b"], lp["cm_ln_g"], lp["cm_ln_b"], lp["sg_ln_g"], lp["sg_ln_b"], lp["sg_wts"],
      lp["sg_bias"])


def _norm_in_kernel(x_ref, g_ref, w_ref, z_ref, hn_ref):
    @pl.when(pl.program_id(1) == 0)
    def _():
        hn_ref[...] = _rms(x_ref[...], g_ref[...]).astype(BF16)

    z_ref[...] = _dot(hn_ref[...], w_ref[...])


def norm_in(x, g, w, l, n_cols, *, tm, tn):
    t, d = x.shape
    return pl.pallas_call(
        _norm_in_kernel,
        grid=(t // tm, n_cols // tn),
        in_specs=[pl.BlockSpec((tm, d), lambda i, j: (i, 0)),
                  pl.BlockSpec((1, d), lambda i, j: (0, 0)),
                  pl.BlockSpec((None, d, tn), lambda i, j: (l, 0, j))],
        out_specs=[pl.BlockSpec((tm, tn), lambda i, j: (i, j)),
                   pl.BlockSpec((tm, d), lambda i, j: (i, 0))],
        out_shape=[jax.ShapeDtypeStruct((t, n_cols), F32), jax.ShapeDtypeStruct((t, d), BF16)],
        compiler_params=_params("parallel", "arbitrary"),
        name="norm_in",
    )(x, g.reshape(1, d), w)


def _gate_out_kernel(hn_ref, p_ref, pt_ref, g0_ref, g1_ref, g2_ref, g3_ref, w0_ref, w1_ref,
                     w2_ref, w3_ref, b_ref, wo_ref, x_ref, o_ref, merged_ref, *, n_head, tn):
    j = pl.program_id(1)
    p = _rows_of(pl.program_id(0), n_head, p_ref, pt_ref)
    hn = hn_ref[...]
    acc = None
    for i, (g_ref, w_ref) in enumerate(((g0_ref, w0_ref), (g1_ref, w1_ref), (g2_ref, w2_ref),
                                        (g3_ref, w3_ref))):
        gate = _sigmoid(_dot(hn, g_ref[...]) + b_ref[i:i + 1, :])
        term = gate * _dot(p[:, i * D_BR:(i + 1) * D_BR].astype(BF16), w_ref[...])
        acc = term if acc is None else acc + term
    merged_ref[:, pl.ds(pl.multiple_of(j * tn, tn), tn)] = acc.astype(BF16)

    @pl.when(j == pl.num_programs(1) - 1)
    def _():
        o_ref[...] = x_ref[...] + _dot(merged_ref[...], wo_ref[...])


def gate_out(hn, p_head, p_tail, w_in, w_outs, b_gate, w_o, l, x, *, tn):
    t, d = x.shape
    tm = p_tail.shape[0]
    n_head = p_head.shape[0] // tm
    assert p_head.shape[0] == n_head * tm and t == (n_head + 1) * tm
    first = N_BRANCH * 2 * D_BR // tn
    gate_spec = lambda i: pl.BlockSpec((None, d, tn), lambda r, c: (l, 0, first + i * (d // tn) + c))
    w_spec = pl.BlockSpec((None, D_BR, tn), lambda r, c: (l, 0, c))
    row = pl.BlockSpec((tm, d), lambda r, c: (r, 0))
    return pl.pallas_call(
        functools.partial(_gate_out_kernel, n_head=n_head, tn=tn),
        grid=(t // tm, d // tn),
        in_specs=[row,
                  pl.BlockSpec((tm, N_BRANCH * D_BR), lambda r, c: (jnp.minimum(r, n_head - 1), 0)),
                  pl.BlockSpec((tm, N_BRANCH * D_BR), lambda r, c: (0, 0)),
                  gate_spec(0), gate_spec(1), gate_spec(2), gate_spec(3),
                  w_spec, w_spec, w_spec, w_spec,
                  pl.BlockSpec((N_BRANCH, tn), lambda r, c: (0, c)),
                  pl.BlockSpec((None, d, d), lambda r, c: (l, 0, 0), pipeline_mode=pl.Buffered(1)),
                  row],
        out_specs=row,
        out_shape=jax.ShapeDtypeStruct((t, d), F32),
        scratch_shapes=[pltpu.VMEM((tm, d), BF16)],
        compiler_params=_params("parallel", "arbitrary"),
        name="gate_out",
    )(hn, p_head, p_tail, w_in, w_in, w_in, w_in, *w_outs, b_gate, w_o, x)


def _softmax_rows(s):
    m = jnp.max(s, axis=-1, keepdims=True)
    p = jnp.exp(s - m)
    return p / jnp.sum(p, axis=-1, keepdims=True)


def _attn_prompt_kernel(q_ref, k_ref, v_ref, o_ref, *, scale):
    for h in range(X_HEADS):
        sl = slice(h * GROUP, (h + 1) * GROUP)
        s = lax.dot_general(q_ref[:, sl].astype(BF16), k_ref[0, :, sl].astype(BF16),
                            (((1,), (1,)), ((), ())), preferred_element_type=F32) * scale
        o_ref[:, sl] = _dot(_softmax_rows(s).astype(BF16), v_ref[0, :, sl].astype(BF16))


def attn_prompt(q, k, v, *, n_seq, seq, tq):
    dx = q.shape[1]
    mem = k.shape[1]
    nb = seq // tq
    return pl.pallas_call(
        functools.partial(_attn_prompt_kernel, scale=GROUP ** -0.5),
        grid=(n_seq, nb),
        in_specs=[pl.BlockSpec((tq, dx), lambda n, t: (n * nb + t, 0)),
                  pl.BlockSpec((1, mem, dx), lambda n, t: (n, 0, 0)),
                  pl.BlockSpec((1, mem, dx), lambda n, t: (n, 0, 0))],
        out_specs=pl.BlockSpec((tq, dx), lambda n, t: (n * nb + t, 0)),
        out_shape=jax.ShapeDtypeStruct((n_seq * seq, dx), F32),
        compiler_params=_params("parallel", "parallel"),
        name="attn_prompt",
    )(q, k, v)


def _attn_sample_kernel(q_ref, k_ref, v_ref, o_ref, *, scale):
    rows, cols = q_ref.shape[1], k_ref.shape[1]
    row_head = lax.broadcasted_iota(jnp.int32, (rows, cols), 0) % X_HEADS
    col_head = lax.broadcasted_iota(jnp.int32, (rows, cols), 1) % X_HEADS
    s = jnp.einsum("bqd,bkd->bqk", q_ref[...].astype(BF16), k_ref[...].astype(BF16),
                   preferred_element_type=F32) * scale
    s = jnp.where((row_head == col_head)[None], s, NEG_INF)
    o_ref[...] = jnp.einsum("bqk,bkd->bqd", _softmax_rows(s).astype(BF16),
                            v_ref[...].astype(BF16), preferred_element_type=F32)


def attn_sample(q, k, v, l, *, sb):
    n_seq, rows, hd = q.shape
    kv_spec = pl.BlockSpec((None, sb, k.shape[2], hd), lambda i: (l, i, 0, 0))
    return pl.pallas_call(
        functools.partial(_attn_sample_kernel, scale=GROUP ** -0.5),
        grid=(n_seq // sb,),
        in_specs=[pl.BlockSpec((sb, rows, hd), lambda i: (i, 0, 0)), kv_spec, kv_spec],
        out_specs=pl.BlockSpec((sb, rows, hd), lambda i: (i, 0, 0)),
        out_shape=jax.ShapeDtypeStruct((n_seq, rows, hd), F32),
        compiler_params=_params("parallel"),
        name="attn_sample",
    )(q, k, v)


def _merge_exchange_network(n):
    pairs, p = [], 1
    while p < n:
        k = p
        while k >= 1:
            for j in range(k % p, n - k, 2 * k):
                for i in range(min(k, n - j - k)):
                    if (i + j) // (2 * p) == (i + j + k) // (2 * p):
                        pairs.append((i + j, i + j + k))
            k //= 2
        p *= 2
    return pairs


def _sort_descending(values):
    n = pl.next_power_of_2(len(values))
    vals = list(values) + [None] * (n - len(values))
    for i, j in _merge_exchange_network(n):
        hi, lo = vals[i], vals[j]
        if lo is None:
            continue
        if hi is None:
            vals[i], vals[j] = lo, None
        else:
            vals[i], vals[j] = jnp.maximum(hi, lo), jnp.minimum(hi, lo)
    return vals[:len(values)]


_CAND = [(i, j) for i in range(PEER_TOPK + 1) for j in range(PEER_TOPK + 1)
         if (i + 1) * (j + 1) <= PEER_TOPK + 1]


def _peer_pre_kernel(ht_ref, wq_ref, keys_ref, s2_o, sel_o, q_scr, s_scr, v_scr, *, tb):
    nk = GROUP
    q_scr[...] = _dot(wq_ref[...], ht_ref[...]).astype(BF16)
    for hp in range(2 * PEER_HEADS):
        s_scr[hp] = _dot(keys_ref[hp], q_scr[hp * nk:(hp + 1) * nk, :])

    idx = lax.broadcasted_iota(jnp.int32, (nk, LANES), 0)

    def chunk(ci, carry):
        lanes = pl.ds(pl.multiple_of(ci * LANES, LANES), LANES)
        tied = jnp.zeros((1, LANES), F32)
        for hp in range(2 * PEER_HEADS):
            h, p = divmod(hp, 2)
            cur = s_scr[hp, :, lanes]
            stacks = _sort_descending([cur[8 * g:8 * (g + 1), :] for g in range(nk // 8)])
            for r in range(PEER_TOPK + 1):
                m = jnp.max(stacks[0], axis=0, keepdims=True)
                v_scr[p, r, h:h + 1, lanes] = m
                depth = PEER_TOPK - r
                if depth > 0:
                    hit = stacks[0] == m
                    below = stacks[1:] + [jnp.full_like(stacks[0], NEG_INF)]
                    stacks = [jnp.where(hit, below[q], stacks[q])
                              for q in range(min(depth, len(stacks)))]
            n_ge = jnp.sum(jnp.where(cur >= m, 1.0, 0.0), axis=0, keepdims=True)
            tied = jnp.maximum(tied, jnp.where(n_ge != PEER_TOPK + 1.0, 1.0, 0.0))

        @pl.when(jnp.max(tied) > 0.0)
        def _():
            for hp in range(2 * PEER_HEADS):
                h, p = divmod(hp, 2)
                cur = s_scr[hp, :, lanes]
                for r in range(PEER_TOPK + 1):
                    m = jnp.max(cur, axis=0, keepdims=True)
                    v_scr[p, r, h:h + 1, lanes] = m
                    if r < PEER_TOPK:
                        first = jnp.min(jnp.where(cur == m, idx, nk), axis=0, keepdims=True)
                        cur = jnp.where(idx == first, NEG_INF, cur)

        v1 = [v_scr[0, r, :, lanes] for r in range(PEER_TOPK + 1)]
        v2 = [v_scr[1, r, :, lanes] for r in range(PEER_TOPK + 1)]
        cands = [v1[i] + v2[j] for i, j in _CAND]
        ranked = _sort_descending(cands)
        kth, nxt = ranked[PEER_TOPK - 1], ranked[PEER_TOPK]
        tau = 0.5 * (kth + nxt)
        top = cands[0]
        zsum = jnp.zeros_like(top)
        for ck in cands:
            zsum = zsum + jnp.where(ck >= tau, jnp.exp(ck - top), 0.0)
        shift = top + jnp.log(zsum)
        for h in range(PEER_HEADS):
            s1 = s_scr[2 * h, :, lanes]
            sel_o[0, h, :, lanes] = (tau[h:h + 1, :] - s1) * LOG2E
            sel_o[1, h, :, lanes] = (s1 - shift[h:h + 1, :]) * LOG2E - 0.5
            s2_o[h * nk:(h + 1) * nk, lanes] = s_scr[2 * h + 1, :, lanes] * LOG2E
        return carry

    lax.fori_loop(0, tb // LANES, chunk, 0)


def peer_pre(ht, wq_t, keys, l, *, tb):
    d, t = ht.shape
    nq = wq_t.shape[1]
    hk = PEER_HEADS * GROUP
    flat = pl.BlockSpec((hk, tb), lambda i: (0, i))
    cube = pl.BlockSpec((2, PEER_HEADS, GROUP, tb), lambda i: (0, 0, 0, i))
    return pl.pallas_call(
        functools.partial(_peer_pre_kernel, tb=tb),
        grid=(t // tb,),
        in_specs=[pl.BlockSpec((d, tb), lambda i: (0, i)),
                  pl.BlockSpec((None, nq, d), lambda i: (l, 0, 0)),
                  pl.BlockSpec((None, 2 * PEER_HEADS, GROUP, GROUP), lambda i: (l, 0, 0, 0))],
        out_specs=[flat, cube],
        out_shape=[jax.ShapeDtypeStruct((hk, t), F32),
                   jax.ShapeDtypeStruct((2, PEER_HEADS, GROUP, t), F32)],
        scratch_shapes=[pltpu.VMEM((nq, tb), BF16),
                        pltpu.VMEM((2 * PEER_HEADS, GROUP, tb), F32),
                        pltpu.VMEM((2, PEER_TOPK + 1, PEER_HEADS, tb), F32)],
        compiler_params=_params("parallel"),
        name="peer_pre",
    )(ht, wq_t, keys)


def _peer_dense_kernel(ht_ref, htn_ref, ua_ref, ub_ref, vt_ref, s2_ref, sel_ref, x_ref,
                       o_ref, acc_ref, at_ref, gw_ref, *, tb, te, ne):
    s = pl.program_id(0)
    e_cur = jnp.maximum(s - 1, 0) % ne
    half = te // 2
    tbh = tb // 2
    n_a = half // GROUP
    piece = lambda k, c: 2 * k + c
    cols = lambda c: slice(c * tbh, (c + 1) * tbh)

    @pl.when(s == 0)
    def _():
        at_ref[0:2] = jnp.zeros((2, half, tbh), F32)

    @pl.when(e_cur == 0)
    def _():
        acc_ref[...] = jnp.zeros_like(acc_ref)

    def pre_activations(k, c, u_ref, h_ref):
        at_ref[piece(k, c)] = _dot(u_ref[...], h_ref[:, cols(c)])

    def weighted_activations(k, c):
        p = piece(k, c)
        for cl in range(tbh // LANES):
            lanes = slice(cl * LANES, (cl + 1) * LANES)
            glanes = slice(c * tbh + cl * LANES, c * tbh + (cl + 1) * LANES)
            for r in range(GROUP // ROWS):
                w = [jnp.zeros((ROWS, LANES), F32) for _ in range(n_a)]
                for h in range(PEER_HEADS):
                    s2 = s2_ref[h * GROUP + r * ROWS:h * GROUP + (r + 1) * ROWS, glanes]
                    for a in range(n_a):
                        ag = k * n_a + a
                        keep = s2 >= sel_ref[0, h, ag:ag + 1, glanes]
                        w[a] = w[a] + jnp.where(keep, jnp.exp2(s2 + sel_ref[1, h, ag:ag + 1, glanes]), 0.0)
                for a in range(n_a):
                    rows = slice(a * GROUP + r * ROWS, a * GROUP + (r + 1) * ROWS)
                    z = at_ref[p, rows, lanes]
                    gw_ref[p, rows, lanes] = (w[a] * (z * (1.0 + lax.erf(z)))).astype(BF16)

    def accumulate(k, c):
        acc_ref[:, cols(c)] += _dot(vt_ref[:, k * half:(k + 1) * half], gw_ref[piece(k, c)])

    pre_activations(1, 0, ub_ref, ht_ref)
    pre_activations(1, 1, ub_ref, ht_ref)
    for c in range(2):
        weighted_activations(0, c)
        accumulate(0, c)
        pre_activations(0, c, ua_ref, htn_ref)
    for c in range(2):
        weighted_activations(1, c)
        accumulate(1, c)

    @pl.when(jnp.logical_and(e_cur == ne - 1, s > 0))
    def _():
        o_ref[...] = x_ref[...] + acc_ref[...].T


def peer_dense(ht, u, vt, l, s2, sel, x, *, tb, te):
    d, t = ht.shape
    ne = u.shape[1] // te
    assert vt.shape[1:] == (ne, d, te)
    n_tiles = (t // tb) * ne
    half = te // 2
    a_blk = te // GROUP
    once = pl.Buffered(1)
    cur = lambda s: jnp.clip(s - 1, 0, n_tiles - 1)
    nxt = lambda s: jnp.minimum(s, n_tiles - 1)
    return pl.pallas_call(
        functools.partial(_peer_dense_kernel, tb=tb, te=te, ne=ne),
        grid=(n_tiles + 1,),
        in_specs=[pl.BlockSpec((d, tb), lambda s: (0, cur(s) // ne), pipeline_mode=once),
                  pl.BlockSpec((d, tb), lambda s: (0, nxt(s) // ne), pipeline_mode=once),
                  pl.BlockSpec((None, half, d), lambda s: (l, 2 * (nxt(s) % ne), 0)),
                  pl.BlockSpec((None, half, d), lambda s: (l, 2 * (cur(s) % ne) + 1, 0)),
                  pl.BlockSpec((None, None, d, te), lambda s: (l, cur(s) % ne, 0, 0)),
                  pl.BlockSpec((PEER_HEADS * GROUP, tb), lambda s: (0, cur(s) // ne),
                               pipeline_mode=once),
                  pl.BlockSpec((2, PEER_HEADS, a_blk, tb),
                               lambda s: (0, 0, cur(s) % ne, cur(s) // ne)),
                  pl.BlockSpec((tb, d), lambda s: (cur(s) // ne, 0), pipeline_mode=once)],
        out_specs=pl.BlockSpec((tb, d), lambda s: (cur(s) // ne, 0)),
        out_shape=jax.ShapeDtypeStruct((t, d), F32),
        scratch_shapes=[pltpu.VMEM((d, tb), F32), pltpu.VMEM((4, half, tb // 2), F32),
                        pltpu.VMEM((4, half, tb // 2), BF16)],
        compiler_params=_params("arbitrary"),
        name="peer_dense",
    )(ht, ht, u, u, vt, s2, sel, x)


def _final_norm_kernel(x_ref, g_ref, head_o, tail_o, *, n_head):
    i = pl.program_id(0)
    y = _rms(x_ref[...], g_ref[...])

    @pl.when(i < n_head)
    def _():
        head_o[...] = y

    @pl.when(i == n_head)
    def _():
        tail_o[...] = y


def final_norm(x, g, *, tm):
    t, d = x.shape
    n_head = t // tm - 1
    assert t == (n_head + 1) * tm
    return pl.pallas_call(
        functools.partial(_final_norm_kernel, n_head=n_head),
        grid=(n_head + 1,),
        in_specs=[pl.BlockSpec((tm, d), lambda i: (i, 0)), pl.BlockSpec((1, d), lambda i: (0, 0))],
        out_specs=[pl.BlockSpec((tm, d), lambda i: (jnp.minimum(i, n_head - 1), 0)),
                   pl.BlockSpec((tm, d), lambda i: (0, 0))],
        out_shape=[jax.ShapeDtypeStruct((n_head * tm, d), F32),
                   jax.ShapeDtypeStruct((tm, d), F32)],
        compiler_params=_params("arbitrary"),
        name="final_norm",
    )(x, g.reshape(1, d))


def _to_time_major(s):
    return jnp.swapaxes(s, 0, 1)


def kernel(x_prompt, x_sample, mem_prompt, cache_mem_k, cache_mem_v, state_pool, state_sconv, state_cconv, g_mix, w_in, b_gate, pool_w, pool_scale, w_pool_out, sc_w, w_sc_out, cm_w, cm_b, cm_ln_g, cm_ln_b, w_cm_out, sg_ln_g, sg_ln_b, sg_w, sg_b, w_sg_out, w_o, g_x, g_mem, w_xq, w_xk, w_xv, w_xo, g_peer, w_pq, peer_keys, peer_u, peer_v, g_final):
    nb, seq, d = x_prompt.shape
    ns, td, _ = x_sample.shape
    depth = w_in.shape[0]
    mem_len = mem_prompt.shape[1]
    past_len = 16384
    t_prompt = nb * seq
    t_all = t_prompt + ns * td
    dx = X_HEADS * GROUP

    tm_big = _tile(t_all, 1088, 16)
    t_s = ns * td
    assert t_prompt % t_s == 0
    tn_in = _tile(w_in.shape[2], 1024, 128)
    tn_d = _tile(d, 512, 128)
    tn_gate = _tile(d, 256, 128)
    ts = _tile(seq, 256, CHUNK)
    tq = _tile(seq, 512, 8)
    tb = _tile(t_all, 512, LANES)
    tb_pre = _tile(t_all, 256, LANES)
    te = _tile(peer_u.shape[1], 1024, 8 * GROUP)

    x = jnp.concatenate([x_prompt.reshape(t_prompt, d),
                         _to_time_major(x_sample).reshape(ns * td, d)], axis=0)
    mem2d = mem_prompt.reshape(nb * mem_len, d)
    col1 = lambda v: v.reshape(1, -1)
    cache_k = cache_mem_k.reshape(depth, ns, mem_len * X_HEADS, GROUP)
    cache_v = cache_mem_v.reshape(depth, ns, mem_len * X_HEADS, GROUP)

    bf = lambda w: w.astype(BF16)
    w_in_b, w_o_b, w_xq_b, w_xk_b, w_xv_b, w_xo_b = map(bf, (w_in, w_o, w_xq, w_xk, w_xv, w_xo))
    w_branch_b = tuple(map(bf, (w_pool_out, w_sc_out, w_cm_out, w_sg_out)))
    w_pq_t = bf(jnp.swapaxes(w_pq, 1, 2))
    keys_b = bf(peer_keys.reshape(depth, 2 * PEER_HEADS, GROUP, GROUP))
    peer_u_b = bf(peer_u * math.sqrt(0.5))
    peer_vt = bf(jnp.swapaxes(peer_v.reshape(depth, -1, te, d), 2, 3))

    outs = {k: [] for k in ("mk", "mv", "pool_p", "sc_p", "cm_p", "pool_s", "sc_s", "cm_s", "cv_s")}
    for l in range(depth):
        lp = dict(
            pool_w=pool_w[l].astype(BF16), pool_scale=col1(pool_scale[l]), sc_w=sc_w[l],
            cm_w=cm_w[l], cm_b=col1(cm_b[l]), cm_ln_g=col1(cm_ln_g[l]), cm_ln_b=col1(cm_ln_b[l]),
            sg_ln_g=col1(sg_ln_g[l]), sg_ln_b=col1(sg_ln_b[l]), sg_w=sg_w[l],
            sg_bias=jnp.repeat(sg_b[l].T, GROUP, axis=1),
            sg_wts=jnp.repeat(jnp.transpose(sg_w[l][:, :td, :td], (1, 2, 0)), GROUP, axis=2))

        z, hn = norm_in(x, g_mix[l], w_in_b, l, 2 * N_BRANCH * D_BR, tm=tm_big, tn=tn_in)
        p_p, pool_p, sc_p, cm_p = mixer_prompt(z, lp, n_seq=nb, seq=seq, ts=ts)
        p_s, pool_s, sc_s, cm_s, vn_s = mixer_sample(
            z, state_pool, state_sconv, state_cconv, l, lp, td=td, ns=ns, t_prompt=t_prompt,
            start_pos=past_len, nsb=_tile(ns, 32, 8))
        p_s = p_s.reshape(t_s, N_BRANCH * D_BR)
        x = gate_out(hn, p_p, p_s, w_in_b, w_branch_b, b_gate[l], w_o_b, l, x, tn=tn_gate)

        q = norm_mm(x, g_x[l], w_xq_b, l, tm=tm_big, tn=dx)
        tm_mem = _tile(nb * mem_len, 1024, 16)
        k_p = norm_mm(mem2d, g_mem[l], w_xk_b, l, tm=tm_mem, tn=dx)
        v_p = norm_mm(mem2d, g_mem[l], w_xv_b, l, tm=tm_mem, tn=dx)
        o_p = attn_prompt(q, k_p.reshape(nb, mem_len, dx), v_p.reshape(nb, mem_len, dx),
                          n_seq=nb, seq=seq, tq=tq)
        q_s = jnp.swapaxes(q[t_prompt:].reshape(td, ns, dx), 0, 1).reshape(ns, td * X_HEADS, GROUP)
        o_s = attn_sample(q_s, cache_k, cache_v, l, sb=_tile(ns, 8, 1))
        o_s = jnp.swapaxes(o_s.reshape(ns, td, dx), 0, 1).reshape(t_s, dx)
        x, ht = attn_out(o_p, o_s, w_xo_b, l, x, g_peer[l])

        s2, sel = peer_pre(ht, w_pq_t, keys_b, l, tb=tb_pre)
        x = peer_dense(ht, peer_u_b, peer_vt, l, s2, sel, x, tb=tb, te=te)

        outs["mk"].append(k_p.reshape(nb, mem_len, X_HEADS, GROUP))
        outs["mv"].append(v_p.reshape(nb, mem_len, X_HEADS, GROUP))
        outs["pool_p"].append(pool_p)
        outs["sc_p"].append(sc_p)
        outs["cm_p"].append(cm_p)
        outs["pool_s"].append(pool_s)
        outs["sc_s"].append(sc_s)
        outs["cm_s"].append(cm_s)
        outs["cv_s"].append(vn_s)

    y_p, y_s = final_norm(x, g_final, tm=t_s)
    y_prompt = y_p.reshape(nb, seq, d)
    y_sample = _to_time_major(y_s.reshape(td, ns, d))
    st = lambda k: jnp.stack(outs[k])
    return (y_prompt, y_sample, st("mk"), st("mv"), st("pool_p"), st("sc_p"), st("cm_p"),
            st("pool_s"), st("sc_s"), st("cm_s"), st("cv_s"))
```

```python
import functools
import math

import jax
import jax.numpy as jnp
from jax import lax
from jax.experimental import pallas as pl
from jax.experimental.pallas import tpu as pltpu

EPS = 1e-6
GROUP = 128
POOL_WINDOWS = (2, 4, 8, 16)
POOL_STATE = max(POOL_WINDOWS) - 1
SC_WIDTH = 3
CM_WIDTH = 31
D_BR = 4 * GROUP
N_BRANCH = 4
CHUNK = 128
X_HEADS = 4
PEER_HEADS = 8
PEER_TOPK = 16
HALO = 32
LANES = 128
ROWS = 32
VMEM_LIMIT = 56 * 2 ** 20

BF16 = jnp.bfloat16
F32 = jnp.float32
NEG_INF = float("-inf")
LOG2E = math.log2(math.e)


def _tile(n, pref, mult=8):
    best = None
    for t in range(mult, min(n, pref) + 1, mult):
        if n % t == 0:
            best = t
    assert best is not None, (n, pref, mult)
    return best


def _params(*sem):
    return pltpu.CompilerParams(dimension_semantics=sem, vmem_limit_bytes=VMEM_LIMIT)


def _rms(x, g):
    ms = jnp.mean(x * x, axis=-1, keepdims=True)
    return x * lax.rsqrt(ms + EPS) * g


def _ln(x, g, b):
    mu = jnp.mean(x, axis=-1, keepdims=True)
    xc = x - mu
    var = jnp.mean(xc * xc, axis=-1, keepdims=True)
    return xc * lax.rsqrt(var + EPS) * g + b


def _sigmoid(x):
    return 1.0 / (1.0 + jnp.exp(-x))


def _dot(a, b):
    return jnp.dot(a, b, preferred_element_type=F32)


def _norm_mm_kernel(x_ref, g_ref, w_ref, o_ref, hn_ref):
    @pl.when(pl.program_id(1) == 0)
    def _():
        hn_ref[...] = _rms(x_ref[...], g_ref[...]).astype(BF16)

    o_ref[...] = _dot(hn_ref[...], w_ref[...])


def norm_mm(x, g, w, l, *, tm, tn):
    t, d = x.shape
    n = w.shape[2]
    return pl.pallas_call(
        _norm_mm_kernel,
        grid=(t // tm, n // tn),
        in_specs=[pl.BlockSpec((tm, d), lambda i, j: (i, 0)),
                  pl.BlockSpec((1, d), lambda i, j: (0, 0)),
                  pl.BlockSpec((None, d, tn), lambda i, j: (l, 0, j))],
        out_specs=pl.BlockSpec((tm, tn), lambda i, j: (i, j)),
        out_shape=jax.ShapeDtypeStruct((t, n), F32),
        scratch_shapes=[pltpu.VMEM((tm, d), BF16)],
        compiler_params=_params("parallel", "arbitrary"),
        name="norm_mm",
    )(x, g.reshape(1, d), w)


def _rows_of(i, n_head, head_ref, tail_ref):
    return jnp.where(i < n_head, head_ref[...], tail_ref[...])


def _attn_out_kernel(a_ref, at_ref, w_ref, r_ref, g_ref, o_ref, ht_ref, *, n_head):
    a = _rows_of(pl.program_id(0), n_head, a_ref, at_ref)
    x = r_ref[...] + _dot(a.astype(BF16), w_ref[...])
    o_ref[...] = x
    ht_ref[...] = _rms(x, g_ref[...]).T.astype(BF16)


def attn_out(a, a_tail, w, l, res, g):
    t, d = res.shape
    tm, k = a_tail.shape
    n_head = a.shape[0] // tm
    assert a.shape[0] == n_head * tm and t == (n_head + 1) * tm and tm % LANES == 0
    row = pl.BlockSpec((tm, d), lambda i: (i, 0))
    return pl.pallas_call(
        functools.partial(_attn_out_kernel, n_head=n_head),
        grid=(t // tm,),
        in_specs=[pl.BlockSpec((tm, k), lambda i: (jnp.minimum(i, n_head - 1), 0)),
                  pl.BlockSpec((tm, k), lambda i: (0, 0)),
                  pl.BlockSpec((None, k, d), lambda i: (l, 0, 0)),
                  row, pl.BlockSpec((1, d), lambda i: (0, 0))],
        out_specs=[row, pl.BlockSpec((d, tm), lambda i: (0, i))],
        out_shape=[jax.ShapeDtypeStruct((t, d), F32), jax.ShapeDtypeStruct((d, t), BF16)],
        compiler_params=_params("parallel"),
        name="attn_out",
    )(a, a_tail, w, res, g.reshape(1, d))


def _pool_branch(window_sums, a, cnts, poolw_ref, pscale):
    outs = []
    for gi in range(len(POOL_WINDOWS)):
        sl = slice(gi * GROUP, (gi + 1) * GROUP)
        pg = window_sums[gi] / cnts[gi] - a[:, sl]
        outs.append(_dot(pg.astype(BF16), poolw_ref[gi]))
    return jnp.concatenate(outs, axis=1) * pscale


def _mixer_prompt_kernel(z_ref, poolw_ref, pscale_ref, scw_ref, cmw_ref, cmb_ref, cmg_ref,
                         cmbeta_ref, sgg_ref, sgbeta_ref, sgw_ref, sgbias_ref,
                         p_ref, pool_o, sc_o, cm_o, ext_ref, shift_ref, *, ts):
    tb = pl.program_id(1)
    h0 = HALO

    @pl.when(tb == 0)
    def _():
        ext_ref[:, 0:h0, :] = jnp.zeros((3, h0, D_BR), F32)

    a = z_ref[:, 0:D_BR]
    bg = z_ref[:, D_BR:2 * D_BR]
    gated = z_ref[:, 2 * D_BR:3 * D_BR] * z_ref[:, 3 * D_BR:4 * D_BR]
    glu = z_ref[:, 4 * D_BR:5 * D_BR] * _sigmoid(z_ref[:, 5 * D_BR:6 * D_BR])
    ext_ref[0, h0:h0 + ts, :] = a
    ext_ref[1, h0:h0 + ts, :] = gated
    ext_ref[2, h0:h0 + ts, :] = glu

    pos = tb * ts + lax.broadcasted_iota(jnp.int32, (ts, 1), 0)
    sums, cnts = [], []
    for gi, w in enumerate(POOL_WINDOWS):
        sl = slice(gi * GROUP, (gi + 1) * GROUP)
        s = a[:, sl]
        for j in range(1, w):
            s = s + ext_ref[0, h0 - j:h0 - j + ts, sl]
        sums.append(s)
        cnts.append(jnp.minimum(w, pos + 1).astype(F32))
    p_ref[:, 0:D_BR] = _pool_branch(sums, a, cnts, poolw_ref, pscale_ref[...]).astype(BF16)

    conv = scw_ref[SC_WIDTH - 1:SC_WIDTH, :] * gated
    for k in range(SC_WIDTH - 1):
        off = h0 - (SC_WIDTH - 1) + k
        conv = conv + scw_ref[k:k + 1, :] * ext_ref[1, off:off + ts, :]
    p_ref[:, D_BR:2 * D_BR] = (bg * conv).astype(BF16)

    span = ts + h0 - 8
    for rho in range(8):
        shift_ref[rho] = ext_ref[2, rho:rho + span, :]
    c = cmw_ref[CM_WIDTH - 1:CM_WIDTH, :] * glu
    for k in range(CM_WIDTH - 1):
        q, rho = divmod(h0 - (CM_WIDTH - 1) + k, 8)
        c = c + cmw_ref[k:k + 1, :] * shift_ref[rho, 8 * q:8 * q + ts, :]
    c = _ln(c + cmb_ref[...], cmg_ref[...], cmbeta_ref[...])
    p_ref[:, 2 * D_BR:3 * D_BR] = (c * _sigmoid(c)).astype(BF16)

    vn = _ln(z_ref[:, 7 * D_BR:8 * D_BR], sgg_ref[...], sgbeta_ref[...])
    row = lax.broadcasted_iota(jnp.int32, (CHUNK, CHUNK), 0)
    col = lax.broadcasted_iota(jnp.int32, (CHUNK, CHUNK), 1)
    tril = row >= col
    for gi in range(D_BR // GROUP):
        sl = slice(gi * GROUP, (gi + 1) * GROUP)
        wg = jnp.where(tril, sgw_ref[gi], 0.0).astype(BF16)
        for ci in range(ts // CHUNK):
            rows = slice(ci * CHUNK, (ci + 1) * CHUNK)
            mixed = _dot(wg, vn[rows, sl].astype(BF16)) + sgbias_ref[:, sl]
            p_ref[rows, 3 * D_BR + gi * GROUP:3 * D_BR + (gi + 1) * GROUP] = (
                z_ref[rows, 6 * D_BR + gi * GROUP:6 * D_BR + (gi + 1) * GROUP] * mixed).astype(BF16)

    end = h0 + ts
    pool_o[0] = ext_ref[0, end - POOL_STATE:end, :]
    sc_o[0] = ext_ref[1, end - (SC_WIDTH - 1):end, :]
    cm_o[0] = ext_ref[2, end - (CM_WIDTH - 1):end, :]
    ext_ref[:, 0:h0, :] = ext_ref[:, ts:ts + h0, :]


def mixer_prompt(z, lp, *, n_seq, seq, ts):
    nb = seq // ts
    full = lambda *shape: pl.BlockSpec(shape, lambda n, t: (0,) * len(shape))
    state = lambda rows: pl.BlockSpec((1, rows, D_BR), lambda n, t: (n, 0, 0))
    return pl.pallas_call(
        functools.partial(_mixer_prompt_kernel, ts=ts),
        grid=(n_seq, nb),
        in_specs=[pl.BlockSpec((ts, 8 * D_BR), lambda n, t: (n * nb + t, 0)),
                  full(4, GROUP, GROUP), full(1, D_BR), full(SC_WIDTH, D_BR),
                  full(CM_WIDTH, D_BR), full(1, D_BR), full(1, D_BR), full(1, D_BR),
                  full(1, D_BR), full(1, D_BR), full(4, CHUNK, CHUNK), full(CHUNK, D_BR)],
        out_specs=[pl.BlockSpec((ts, N_BRANCH * D_BR), lambda n, t: (n * nb + t, 0)),
                   state(POOL_STATE), state(SC_WIDTH - 1), state(CM_WIDTH - 1)],
        out_shape=[jax.ShapeDtypeStruct((n_seq * seq, N_BRANCH * D_BR), BF16),
                   jax.ShapeDtypeStruct((n_seq, POOL_STATE, D_BR), F32),
                   jax.ShapeDtypeStruct((n_seq, SC_WIDTH - 1, D_BR), F32),
                   jax.ShapeDtypeStruct((n_seq, CM_WIDTH - 1, D_BR), F32)],
        scratch_shapes=[pltpu.VMEM((3, HALO + ts, D_BR), F32),
                        pltpu.VMEM((8, HALO + ts - 8, D_BR), F32)],
        compiler_params=_params("parallel", "arbitrary"),
        name="mixer_prompt",
    )(z, lp["pool_w"], lp["pool_scale"], lp["sc_w"], lp["cm_w"], lp["cm_b"], lp["cm_ln_g"],
      lp["cm_ln_b"], lp["sg_ln_g"], lp["sg_ln_b"], lp["sg_w"], lp["sg_bias"])


def _mixer_sample_kernel(z_ref, pool_ref, sc_ref, cm_ref, poolw_ref, pscale_ref, scw_ref, cmw_ref,
                         cmb_ref, cmg_ref, cmbeta_ref, sgg_ref, sgbeta_ref, sgwts_ref, sgbias_ref,
                         p_ref, pool_o, sc_o, cm_o, vn_o, *, td, ns, start_pos):
    rows = lambda t: slice(t * ns, (t + 1) * ns)
    col = lambda k: slice(k * D_BR, (k + 1) * D_BR)

    ext_a = [pool_ref[j] for j in range(POOL_STATE)] + [z_ref[rows(t), col(0)] for t in range(td)]
    a_all = jnp.concatenate(ext_a[POOL_STATE:], axis=0)
    sums, cnts = [], []
    for gi, w in enumerate(POOL_WINDOWS):
        sl = slice(gi * GROUP, (gi + 1) * GROUP)
        per_t = []
        for t in range(td):
            s = ext_a[POOL_STATE + t][:, sl]
            for j in range(1, w):
                s = s + ext_a[POOL_STATE + t - j][:, sl]
            per_t.append(s)
        sums.append(jnp.concatenate(per_t, axis=0))
        cnt = jnp.concatenate(
            [jnp.full((ns, 1), float(min(w, start_pos + t + 1)), F32) for t in range(td)], axis=0)
        cnts.append(cnt)
    p_ref[:, col(0)] = _pool_branch(sums, a_all, cnts, poolw_ref, pscale_ref[...]).astype(BF16)
    for j in range(POOL_STATE):
        pool_o[j] = ext_a[td + j]

    ext_b = [sc_ref[j] for j in range(SC_WIDTH - 1)]
    ext_b += [z_ref[rows(t), col(2)] * z_ref[rows(t), col(3)] for t in range(td)]
    for t in range(td):
        conv = scw_ref[0:1, :] * ext_b[t]
        for k in range(1, SC_WIDTH):
            conv = conv + scw_ref[k:k + 1, :] * ext_b[t + k]
        p_ref[rows(t), col(1)] = (z_ref[rows(t), col(1)] * conv).astype(BF16)
    for j in range(SC_WIDTH - 1):
        sc_o[j] = ext_b[td + j]

    ext_c = [cm_ref[j] for j in range(CM_WIDTH - 1)]
    ext_c += [z_ref[rows(t), col(4)] * _sigmoid(z_ref[rows(t), col(5)]) for t in range(td)]
    for t in range(td):
        c = cmw_ref[0:1, :] * ext_c[t]
        for k in range(1, CM_WIDTH):
            c = c + cmw_ref[k:k + 1, :] * ext_c[t + k]
        c = _ln(c + cmb_ref[...], cmg_ref[...], cmbeta_ref[...])
        p_ref[rows(t), col(2)] = (c * _sigmoid(c)).astype(BF16)
    for j in range(CM_WIDTH - 1):
        cm_o[j] = ext_c[td + j]

    vn = [_ln(z_ref[rows(t), col(7)], sgg_ref[...], sgbeta_ref[...]) for t in range(td)]
    for t in range(td):
        vn_o[t] = vn[t]
        mixed = sgbias_ref[t:t + 1, :]
        for s in range(t + 1):
            mixed = mixed + sgwts_ref[t, s:s + 1, :] * vn[s]
        p_ref[rows(t), col(3)] = (z_ref[rows(t), col(6)] * mixed).astype(BF16)


def mixer_sample(z, pool_tm, sc_tm, cm_tm, lp, *, td, ns, t_prompt, start_pos):
    blk = t_prompt // (td * ns)
    full = lambda *shape: pl.BlockSpec(shape, lambda i: (0,) * len(shape))
    tm_shape = lambda rows: jax.ShapeDtypeStruct((rows, ns, D_BR), F32)
    return pl.pallas_call(
        functools.partial(_mixer_sample_kernel, td=td, ns=ns, start_pos=start_pos),
        grid=(1,),
        in_specs=[pl.BlockSpec((td * ns, 8 * D_BR), lambda i: (blk, 0)),
                  full(POOL_STATE, ns, D_BR), full(SC_WIDTH - 1, ns, D_BR),
                  full(CM_WIDTH - 1, ns, D_BR),
                  full(4, GROUP, GROUP), full(1, D_BR), full(SC_WIDTH, D_BR),
                  full(CM_WIDTH, D_BR), full(1, D_BR), full(1, D_BR), full(1, D_BR),
                  full(1, D_BR), full(1, D_BR), full(td, td, D_BR), full(CHUNK, D_BR)],
        out_specs=[full(td * ns, N_BRANCH * D_BR),
                   full(POOL_STATE, ns, D_BR), full(SC_WIDTH - 1, ns, D_BR),
                   full(CM_WIDTH - 1, ns, D_BR), full(td, ns, D_BR)],
        out_shape=[jax.ShapeDtypeStruct((td * ns, N_BRANCH * D_BR), BF16), tm_shape(POOL_STATE),
                   tm_shape(SC_WIDTH - 1), tm_shape(CM_WIDTH - 1), tm_shape(td)],
        compiler_params=_params("arbitrary"),
        name="mixer_sample",
    )(z, pool_tm, sc_tm, cm_tm, lp["pool_w"], lp["pool_scale"], lp["sc_w"], lp["cm_w"],
      lp["cm_b"], lp["cm_ln_g"], lp["cm_ln_b"], lp["sg_ln_g"], lp["sg_ln_b"], lp["sg_wts"],
      lp["sg_bias"])


def _norm_in_kernel(x_ref, g_ref, w_ref, z_ref, hn_ref):
    @pl.when(pl.program_id(1) == 0)
    def _():
        hn_ref[...] = _rms(x_ref[...], g_ref[...]).astype(BF16)

    z_ref[...] = _dot(hn_ref[...], w_ref[...])


def norm_in(x, g, w, l, n_cols, *, tm, tn):
    t, d = x.shape
    return pl.pallas_call(
        _norm_in_kernel,
        grid=(t // tm, n_cols // tn),
        in_specs=[pl.BlockSpec((tm, d), lambda i, j: (i, 0)),
                  pl.BlockSpec((1, d), lambda i, j: (0, 0)),
                  pl.BlockSpec((None, d, tn), lambda i, j: (l, 0, j))],
        out_specs=[pl.BlockSpec((tm, tn), lambda i, j: (i, j)),
                   pl.BlockSpec((tm, d), lambda i, j: (i, 0))],
        out_shape=[jax.ShapeDtypeStruct((t, n_cols), F32), jax.ShapeDtypeStruct((t, d), BF16)],
        compiler_params=_params("parallel", "arbitrary"),
        name="norm_in",
    )(x, g.reshape(1, d), w)


def _gate_out_kernel(hn_ref, p_ref, pt_ref, g0_ref, g1_ref, g2_ref, g3_ref, w0_ref, w1_ref,
                     w2_ref, w3_ref, b_ref, wo_ref, x_ref, o_ref, merged_ref, *, n_head, tn):
    j = pl.program_id(1)
    p = _rows_of(pl.program_id(0), n_head, p_ref, pt_ref)
    hn = hn_ref[...]
    acc = None
    for i, (g_ref, w_ref) in enumerate(((g0_ref, w0_ref), (g1_ref, w1_ref), (g2_ref, w2_ref),
                                        (g3_ref, w3_ref))):
        gate = _sigmoid(_dot(hn, g_ref[...]) + b_ref[i:i + 1, :])
        term = gate * _dot(p[:, i * D_BR:(i + 1) * D_BR].astype(BF16), w_ref[...])
        acc = term if acc is None else acc + term
    merged_ref[:, pl.ds(pl.multiple_of(j * tn, tn), tn)] = acc.astype(BF16)

    @pl.when(j == pl.num_programs(1) - 1)
    def _():
        o_ref[...] = x_ref[...] + _dot(merged_ref[...], wo_ref[...])


def gate_out(hn, p_head, p_tail, w_in, w_outs, b_gate, w_o, l, x, *, tn):
    t, d = x.shape
    tm = p_tail.shape[0]
    n_head = p_head.shape[0] // tm
    assert p_head.shape[0] == n_head * tm and t == (n_head + 1) * tm
    first = N_BRANCH * 2 * D_BR // tn
    gate_spec = lambda i: pl.BlockSpec((None, d, tn), lambda r, c: (l, 0, first + i * (d // tn) + c))
    w_spec = pl.BlockSpec((None, D_BR, tn), lambda r, c: (l, 0, c))
    row = pl.BlockSpec((tm, d), lambda r, c: (r, 0))
    return pl.pallas_call(
        functools.partial(_gate_out_kernel, n_head=n_head, tn=tn),
        grid=(t // tm, d // tn),
        in_specs=[row,
                  pl.BlockSpec((tm, N_BRANCH * D_BR), lambda r, c: (jnp.minimum(r, n_head - 1), 0)),
                  pl.BlockSpec((tm, N_BRANCH * D_BR), lambda r, c: (0, 0)),
                  gate_spec(0), gate_spec(1), gate_spec(2), gate_spec(3),
                  w_spec, w_spec, w_spec, w_spec,
                  pl.BlockSpec((N_BRANCH, tn), lambda r, c: (0, c)),
                  pl.BlockSpec((None, d, d), lambda r, c: (l, 0, 0), pipeline_mode=pl.Buffered(1)),
                  row],
        out_specs=row,
        out_shape=jax.ShapeDtypeStruct((t, d), F32),
        scratch_shapes=[pltpu.VMEM((tm, d), BF16)],
        compiler_params=_params("parallel", "arbitrary"),
        name="gate_out",
    )(hn, p_head, p_tail, w_in, w_in, w_in, w_in, *w_outs, b_gate, w_o, x)


def _softmax_rows(s):
    m = jnp.max(s, axis=-1, keepdims=True)
    p = jnp.exp(s - m)
    return p / jnp.sum(p, axis=-1, keepdims=True)


def _attn_prompt_kernel(q_ref, k_ref, v_ref, o_ref, *, scale):
    for h in range(X_HEADS):
        sl = slice(h * GROUP, (h + 1) * GROUP)
        s = lax.dot_general(q_ref[:, sl].astype(BF16), k_ref[0, :, sl].astype(BF16),
                            (((1,), (1,)), ((), ())), preferred_element_type=F32) * scale
        o_ref[:, sl] = _dot(_softmax_rows(s).astype(BF16), v_ref[0, :, sl].astype(BF16))


def attn_prompt(q, kv, *, n_seq, seq, tq):
    dx = q.shape[1]
    mem = kv.shape[1]
    nb = seq // tq
    return pl.pallas_call(
        functools.partial(_attn_prompt_kernel, scale=GROUP ** -0.5),
        grid=(n_seq, nb),
        in_specs=[pl.BlockSpec((tq, dx), lambda n, t: (n * nb + t, 0)),
                  pl.BlockSpec((1, mem, dx), lambda n, t: (n, 0, 0)),
                  pl.BlockSpec((1, mem, dx), lambda n, t: (n, 0, 1))],
        out_specs=pl.BlockSpec((tq, dx), lambda n, t: (n * nb + t, 0)),
        out_shape=jax.ShapeDtypeStruct((n_seq * seq, dx), F32),
        compiler_params=_params("parallel", "parallel"),
        name="attn_prompt",
    )(q, kv, kv)


def _attn_sample_kernel(q_ref, k_ref, v_ref, o_ref, *, scale):
    rows, cols = q_ref.shape[1], k_ref.shape[1]
    row_head = lax.broadcasted_iota(jnp.int32, (rows, cols), 0) % X_HEADS
    col_head = lax.broadcasted_iota(jnp.int32, (rows, cols), 1) % X_HEADS
    s = jnp.einsum("bqd,bkd->bqk", q_ref[...].astype(BF16), k_ref[...].astype(BF16),
                   preferred_element_type=F32) * scale
    s = jnp.where((row_head == col_head)[None], s, NEG_INF)
    o_ref[...] = jnp.einsum("bqk,bkd->bqd", _softmax_rows(s).astype(BF16),
                            v_ref[...].astype(BF16), preferred_element_type=F32)


def attn_sample(q, k, v, l, *, sb):
    n_seq, rows, hd = q.shape
    kv_spec = pl.BlockSpec((None, sb, k.shape[2], hd), lambda i: (l, i, 0, 0))
    return pl.pallas_call(
        functools.partial(_attn_sample_kernel, scale=GROUP ** -0.5),
        grid=(n_seq // sb,),
        in_specs=[pl.BlockSpec((sb, rows, hd), lambda i: (i, 0, 0)), kv_spec, kv_spec],
        out_specs=pl.BlockSpec((sb, rows, hd), lambda i: (i, 0, 0)),
        out_shape=jax.ShapeDtypeStruct((n_seq, rows, hd), F32),
        compiler_params=_params("parallel"),
        name="attn_sample",
    )(q, k, v)


def _merge_exchange_network(n):
    pairs, p = [], 1
    while p < n:
        k = p
        while k >= 1:
            for j in range(k % p, n - k, 2 * k):
                for i in range(min(k, n - j - k)):
                    if (i + j) // (2 * p) == (i + j + k) // (2 * p):
                        pairs.append((i + j, i + j + k))
            k //= 2
        p *= 2
    return pairs


def _sort_descending(values):
    n = pl.next_power_of_2(len(values))
    vals = list(values) + [None] * (n - len(values))
    for i, j in _merge_exchange_network(n):
        hi, lo = vals[i], vals[j]
        if lo is None:
            continue
        if hi is None:
            vals[i], vals[j] = lo, None
        else:
            vals[i], vals[j] = jnp.maximum(hi, lo), jnp.minimum(hi, lo)
    return vals[:len(values)]


_CAND = [(i, j) for i in range(PEER_TOPK + 1) for j in range(PEER_TOPK + 1)
         if (i + 1) * (j + 1) <= PEER_TOPK + 1]


def _peer_pre_kernel(ht_ref, wq_ref, keys_ref, s2_o, sel_o, q_scr, s_scr, v_scr, *, tb):
    nk = GROUP
    q_scr[...] = _dot(wq_ref[...], ht_ref[...]).astype(BF16)
    for hp in range(2 * PEER_HEADS):
        s_scr[hp] = _dot(keys_ref[hp], q_scr[hp * nk:(hp + 1) * nk, :])

    idx = lax.broadcasted_iota(jnp.int32, (nk, LANES), 0)

    def chunk(ci, carry):
        lanes = pl.ds(pl.multiple_of(ci * LANES, LANES), LANES)
        tied = jnp.zeros((1, LANES), F32)
        for hp in range(2 * PEER_HEADS):
            h, p = divmod(hp, 2)
            cur = s_scr[hp, :, lanes]
            stacks = _sort_descending([cur[8 * g:8 * (g + 1), :] for g in range(nk // 8)])
            for r in range(PEER_TOPK + 1):
                m = jnp.max(stacks[0], axis=0, keepdims=True)
                v_scr[p, r, h:h + 1, lanes] = m
                depth = PEER_TOPK - r
                if depth > 0:
                    hit = stacks[0] == m
                    below = stacks[1:] + [jnp.full_like(stacks[0], NEG_INF)]
                    stacks = [jnp.where(hit, below[q], stacks[q])
                              for q in range(min(depth, len(stacks)))]
            n_ge = jnp.sum(jnp.where(cur >= m, 1.0, 0.0), axis=0, keepdims=True)
            tied = jnp.maximum(tied, jnp.where(n_ge != PEER_TOPK + 1.0, 1.0, 0.0))

        @pl.when(jnp.max(tied) > 0.0)
        def _():
            for hp in range(2 * PEER_HEADS):
                h, p = divmod(hp, 2)
                cur = s_scr[hp, :, lanes]
                for r in range(PEER_TOPK + 1):
                    m = jnp.max(cur, axis=0, keepdims=True)
                    v_scr[p, r, h:h + 1, lanes] = m
                    if r < PEER_TOPK:
                        first = jnp.min(jnp.where(cur == m, idx, nk), axis=0, keepdims=True)
                        cur = jnp.where(idx == first, NEG_INF, cur)

        v1 = [v_scr[0, r, :, lanes] for r in range(PEER_TOPK + 1)]
        v2 = [v_scr[1, r, :, lanes] for r in range(PEER_TOPK + 1)]
        cands = [v1[i] + v2[j] for i, j in _CAND]
        ranked = _sort_descending(cands)
        kth, nxt = ranked[PEER_TOPK - 1], ranked[PEER_TOPK]
        tau = 0.5 * (kth + nxt)
        top = cands[0]
        zsum = jnp.zeros_like(top)
        for ck in cands:
            zsum = zsum + jnp.where(ck >= tau, jnp.exp(ck - top), 0.0)
        shift = top + jnp.log(zsum)
        for h in range(PEER_HEADS):
            s1 = s_scr[2 * h, :, lanes]
            sel_o[0, h, :, lanes] = (tau[h:h + 1, :] - s1) * LOG2E
            sel_o[1, h, :, lanes] = (s1 - shift[h:h + 1, :]) * LOG2E - 0.5
            s2_o[h * nk:(h + 1) * nk, lanes] = s_scr[2 * h + 1, :, lanes] * LOG2E
        return carry

    lax.fori_loop(0, tb // LANES, chunk, 0)


def peer_pre(ht, wq_t, keys, l, *, tb):
    d, t = ht.shape
    nq = wq_t.shape[1]
    hk = PEER_HEADS * GROUP
    flat = pl.BlockSpec((hk, tb), lambda i: (0, i))
    cube = pl.BlockSpec((2, PEER_HEADS, GROUP, tb), lambda i: (0, 0, 0, i))
    return pl.pallas_call(
        functools.partial(_peer_pre_kernel, tb=tb),
        grid=(t // tb,),
        in_specs=[pl.BlockSpec((d, tb), lambda i: (0, i)),
                  pl.BlockSpec((None, nq, d), lambda i: (l, 0, 0)),
                  pl.BlockSpec((None, 2 * PEER_HEADS, GROUP, GROUP), lambda i: (l, 0, 0, 0))],
        out_specs=[flat, cube],
        out_shape=[jax.ShapeDtypeStruct((hk, t), F32),
                   jax.ShapeDtypeStruct((2, PEER_HEADS, GROUP, t), F32)],
        scratch_shapes=[pltpu.VMEM((nq, tb), BF16),
                        pltpu.VMEM((2 * PEER_HEADS, GROUP, tb), F32),
                        pltpu.VMEM((2, PEER_TOPK + 1, PEER_HEADS, tb), F32)],
        compiler_params=_params("parallel"),
        name="peer_pre",
    )(ht, wq_t, keys)


def _peer_dense_kernel(ht_ref, htn_ref, ua_ref, ub_ref, vt_ref, s2_ref, sel_ref, x_ref,
                       o_ref, acc_ref, at_ref, gw_ref, *, tb, te, ne):
    s = pl.program_id(0)
    e_cur = jnp.maximum(s - 1, 0) % ne
    half = te // 2
    tbh = tb // 2
    n_a = half // GROUP
    piece = lambda k, c: 2 * k + c
    cols = lambda c: slice(c * tbh, (c + 1) * tbh)

    @pl.when(s == 0)
    def _():
        at_ref[0:2] = jnp.zeros((2, half, tbh), F32)

    @pl.when(e_cur == 0)
    def _():
        acc_ref[...] = jnp.zeros_like(acc_ref)

    def pre_activations(k, c, u_ref, h_ref):
        at_ref[piece(k, c)] = _dot(u_ref[...], h_ref[:, cols(c)])

    def weighted_activations(k, c):
        p = piece(k, c)
        for cl in range(tbh // LANES):
            lanes = slice(cl * LANES, (cl + 1) * LANES)
            glanes = slice(c * tbh + cl * LANES, c * tbh + (cl + 1) * LANES)
            for r in range(GROUP // ROWS):
                w = [jnp.zeros((ROWS, LANES), F32) for _ in range(n_a)]
                for h in range(PEER_HEADS):
                    s2 = s2_ref[h * GROUP + r * ROWS:h * GROUP + (r + 1) * ROWS, glanes]
                    for a in range(n_a):
                        ag = k * n_a + a
                        keep = s2 >= sel_ref[0, h, ag:ag + 1, glanes]
                        w[a] = w[a] + jnp.where(keep, jnp.exp2(s2 + sel_ref[1, h, ag:ag + 1, glanes]), 0.0)
                for a in range(n_a):
                    rows = slice(a * GROUP + r * ROWS, a * GROUP + (r + 1) * ROWS)
                    z = at_ref[p, rows, lanes]
                    gw_ref[p, rows, lanes] = (w[a] * (z * (1.0 + lax.erf(z)))).astype(BF16)

    def accumulate(k, c):
        acc_ref[:, cols(c)] += _dot(vt_ref[:, k * half:(k + 1) * half], gw_ref[piece(k, c)])

    pre_activations(1, 0, ub_ref, ht_ref)
    pre_activations(1, 1, ub_ref, ht_ref)
    for c in range(2):
        weighted_activations(0, c)
        accumulate(0, c)
        pre_activations(0, c, ua_ref, htn_ref)
    for c in range(2):
        weighted_activations(1, c)
        accumulate(1, c)

    @pl.when(jnp.logical_and(e_cur == ne - 1, s > 0))
    def _():
        o_ref[...] = x_ref[...] + acc_ref[...].T


def peer_dense(ht, u, vt, l, s2, sel, x, *, tb, te):
    d, t = ht.shape
    ne = u.shape[1] // te
    assert vt.shape[1:] == (ne, d, te)
    n_tiles = (t // tb) * ne
    half = te // 2
    a_blk = te // GROUP
    once = pl.Buffered(1)
    cur = lambda s: jnp.clip(s - 1, 0, n_tiles - 1)
    nxt = lambda s: jnp.minimum(s, n_tiles - 1)
    return pl.pallas_call(
        functools.partial(_peer_dense_kernel, tb=tb, te=te, ne=ne),
        grid=(n_tiles + 1,),
        in_specs=[pl.BlockSpec((d, tb), lambda s: (0, cur(s) // ne), pipeline_mode=once),
                  pl.BlockSpec((d, tb), lambda s: (0, nxt(s) // ne), pipeline_mode=once),
                  pl.BlockSpec((None, half, d), lambda s: (l, 2 * (nxt(s) % ne), 0)),
                  pl.BlockSpec((None, half, d), lambda s: (l, 2 * (cur(s) % ne) + 1, 0)),
                  pl.BlockSpec((None, None, d, te), lambda s: (l, cur(s) % ne, 0, 0)),
                  pl.BlockSpec((PEER_HEADS * GROUP, tb), lambda s: (0, cur(s) // ne),
                               pipeline_mode=once),
                  pl.BlockSpec((2, PEER_HEADS, a_blk, tb),
                               lambda s: (0, 0, cur(s) % ne, cur(s) // ne)),
                  pl.BlockSpec((tb, d), lambda s: (cur(s) // ne, 0), pipeline_mode=once)],
        out_specs=pl.BlockSpec((tb, d), lambda s: (cur(s) // ne, 0)),
        out_shape=jax.ShapeDtypeStruct((t, d), F32),
        scratch_shapes=[pltpu.VMEM((d, tb), F32), pltpu.VMEM((4, half, tb // 2), F32),
                        pltpu.VMEM((4, half, tb // 2), BF16)],
        compiler_params=_params("arbitrary"),
        name="peer_dense",
    )(ht, ht, u, u, vt, s2, sel, x)


def _final_norm_kernel(x_ref, g_ref, head_o, tail_o, *, n_head):
    i = pl.program_id(0)
    y = _rms(x_ref[...], g_ref[...])

    @pl.when(i < n_head)
    def _():
        head_o[...] = y

    @pl.when(i == n_head)
    def _():
        tail_o[...] = y


def final_norm(x, g, *, tm):
    t, d = x.shape
    n_head = t // tm - 1
    assert t == (n_head + 1) * tm
    return pl.pallas_call(
        functools.partial(_final_norm_kernel, n_head=n_head),
        grid=(n_head + 1,),
        in_specs=[pl.BlockSpec((tm, d), lambda i: (i, 0)), pl.BlockSpec((1, d), lambda i: (0, 0))],
        out_specs=[pl.BlockSpec((tm, d), lambda i: (jnp.minimum(i, n_head - 1), 0)),
                   pl.BlockSpec((tm, d), lambda i: (0, 0))],
        out_shape=[jax.ShapeDtypeStruct((n_head * tm, d), F32),
                   jax.ShapeDtypeStruct((tm, d), F32)],
        compiler_params=_params("arbitrary"),
        name="final_norm",
    )(x, g.reshape(1, d))


def _to_time_major(s):
    return jnp.swapaxes(s, 0, 1)


def kernel(x_prompt, x_sample, mem_prompt, cache_mem_k, cache_mem_v, state_pool, state_sconv, state_cconv, g_mix, w_in, b_gate, pool_w, pool_scale, w_pool_out, sc_w, w_sc_out, cm_w, cm_b, cm_ln_g, cm_ln_b, w_cm_out, sg_ln_g, sg_ln_b, sg_w, sg_b, w_sg_out, w_o, g_x, g_mem, w_xq, w_xk, w_xv, w_xo, g_peer, w_pq, peer_keys, peer_u, peer_v, g_final):
    nb, seq, d = x_prompt.shape
    ns, td, _ = x_sample.shape
    depth = w_in.shape[0]
    mem_len = mem_prompt.shape[1]
    past_len = 16384
    t_prompt = nb * seq
    t_all = t_prompt + ns * td
    dx = X_HEADS * GROUP

    tm_big = _tile(t_all, 1088, 16)
    t_s = ns * td
    assert t_prompt % t_s == 0
    tn_in = _tile(w_in.shape[2], 1024, 128)
    tn_d = _tile(d, 512, 128)
    tn_gate = _tile(d, 256, 128)
    ts = _tile(seq, 256, CHUNK)
    tq = _tile(seq, 512, 8)
    tb = _tile(t_all, 512, LANES)
    tb_pre = _tile(t_all, 256, LANES)
    te = _tile(peer_u.shape[1], 1024, 8 * GROUP)

    x = jnp.concatenate([x_prompt.reshape(t_prompt, d),
                         _to_time_major(x_sample).reshape(ns * td, d)], axis=0)
    mem2d = mem_prompt.reshape(nb * mem_len, d)
    col1 = lambda v: v.reshape(1, -1)
    cache_k = cache_mem_k.reshape(depth, ns, mem_len * X_HEADS, GROUP)
    cache_v = cache_mem_v.reshape(depth, ns, mem_len * X_HEADS, GROUP)

    bf = lambda w: w.astype(BF16)
    w_in_b, w_o_b, w_xq_b, w_xo_b = map(bf, (w_in, w_o, w_xq, w_xo))
    w_xkv_b = bf(jnp.concatenate([w_xk, w_xv], axis=2))
    w_branch_b = tuple(map(bf, (w_pool_out, w_sc_out, w_cm_out, w_sg_out)))
    w_pq_t = bf(jnp.swapaxes(w_pq, 1, 2))
    keys_b = bf(peer_keys.reshape(depth, 2 * PEER_HEADS, GROUP, GROUP))
    peer_u_b = bf(peer_u * math.sqrt(0.5))
    peer_vt = bf(jnp.swapaxes(peer_v.reshape(depth, -1, te, d), 2, 3))

    outs = {k: [] for k in ("mk", "mv", "pool_p", "sc_p", "cm_p", "pool_s", "sc_s", "cm_s", "cv_s")}
    for l in range(depth):
        lp = dict(
            pool_w=pool_w[l].astype(BF16), pool_scale=col1(pool_scale[l]), sc_w=sc_w[l],
            cm_w=cm_w[l], cm_b=col1(cm_b[l]), cm_ln_g=col1(cm_ln_g[l]), cm_ln_b=col1(cm_ln_b[l]),
            sg_ln_g=col1(sg_ln_g[l]), sg_ln_b=col1(sg_ln_b[l]), sg_w=sg_w[l],
            sg_bias=jnp.repeat(sg_b[l].T, GROUP, axis=1),
            sg_wts=jnp.repeat(jnp.transpose(sg_w[l][:, :td, :td], (1, 2, 0)), GROUP, axis=2))

        z, hn = norm_in(x, g_mix[l], w_in_b, l, 2 * N_BRANCH * D_BR, tm=tm_big, tn=tn_in)
        p_p, pool_p, sc_p, cm_p = mixer_prompt(z, lp, n_seq=nb, seq=seq, ts=ts)
        p_s, pool_s, sc_s, cm_s, vn_s = mixer_sample(
            z, _to_time_major(state_pool[l]), _to_time_major(state_sconv[l]),
            _to_time_major(state_cconv[l]), lp, td=td, ns=ns, t_prompt=t_prompt, start_pos=past_len)
        x = gate_out(hn, p_p, p_s, w_in_b, w_branch_b, b_gate[l], w_o_b, l, x, tn=tn_gate)

        q = norm_mm(x, g_x[l], w_xq_b, l, tm=tm_big, tn=dx)
        tm_mem = _tile(nb * mem_len, 1024, 16)
        kv_p = norm_mm(mem2d, g_mem[l], w_xkv_b, l, tm=tm_mem, tn=dx)
        k_p, v_p = kv_p[:, :dx], kv_p[:, dx:]
        o_p = attn_prompt(q, kv_p.reshape(nb, mem_len, 2 * dx), n_seq=nb, seq=seq, tq=tq)
        q_s = jnp.swapaxes(q[t_prompt:].reshape(td, ns, dx), 0, 1).reshape(ns, td * X_HEADS, GROUP)
        o_s = attn_sample(q_s, cache_k, cache_v, l, sb=_tile(ns, 8, 1))
        o_s = jnp.swapaxes(o_s.reshape(ns, td, dx), 0, 1).reshape(t_s, dx)
        x, ht = attn_out(o_p, o_s, w_xo_b, l, x, g_peer[l])

        s2, sel = peer_pre(ht, w_pq_t, keys_b, l, tb=tb_pre)
        x = peer_dense(ht, peer_u_b, peer_vt, l, s2, sel, x, tb=tb, te=te)

        outs["mk"].append(k_p.reshape(nb, mem_len, X_HEADS, GROUP))
        outs["mv"].append(v_p.reshape(nb, mem_len, X_HEADS, GROUP))
        outs["pool_p"].append(pool_p)
        outs["sc_p"].append(sc_p)
        outs["cm_p"].append(cm_p)
        outs["pool_s"].append(_to_time_major(pool_s))
        outs["sc_s"].append(_to_time_major(sc_s))
        outs["cm_s"].append(_to_time_major(cm_s))
        outs["cv_s"].append(_to_time_major(vn_s))

    y_p, y_s = final_norm(x, g_final, tm=t_s)
    y_prompt = y_p.reshape(nb, seq, d)
    y_sample = _to_time_major(y_s.reshape(td, ns, d))
    st = lambda k: jnp.stack(outs[k])
    return (y_prompt, y_sample, st("mk"), st("mv"), st("pool_p"), st("sc_p"), st("cm_p"),
            st("pool_s"), st("sc_s"), st("cm_s"), st("cv_s"))
```
